```python
import math
import jax, jax.numpy as jnp
from jax import lax
import numpy as np

D_MODEL = 1024
BATCH = 2
SEQ = 8192
DEPTH = 1

HEAD_DIM = 64
DIFF_HEADS = 4
DIFF_VDIM = 2 * HEAD_DIM
DIFF_QK_WIDTH = DIFF_HEADS * 2 * HEAD_DIM
DIFF_WIDTH = DIFF_HEADS * DIFF_VDIM
NA_HEADS = 8
NA_WIDTH = NA_HEADS * HEAD_DIM
MIX_WIDTH = DIFF_WIDTH + NA_WIDTH
IN_COLS = 2 * DIFF_QK_WIDTH + DIFF_WIDTH + 3 * NA_WIDTH
GRID_W = 64
NA_ROWS = 8
NA_COLS = 16
MEM_TOKENS = 256
MEM_HEADS = 4
MEM_HEAD_DIM = 64
MEM_WIDTH = MEM_HEADS * MEM_HEAD_DIM
N_EXPERTS = 32
TOP_K = 4
D_FF = 1024
SWIGLU_LIMIT = 7.0
SWIGLU_ALPHA = 1.702
Q_BLOCK = 128
MOE_BLOCK = 128
RMS_EPS = 1e-5

kernel_name = "hybrid_diffattn_natten_moe_encoder"


def rmsnorm(x, g):
    xf = x.astype(jnp.float32)
    y = xf * lax.rsqrt(jnp.mean(xf * xf, axis=-1, keepdims=True) + RMS_EPS)
    return (y * g.astype(jnp.float32)).astype(x.dtype)


def alibi_slopes(n):
    return jnp.asarray(np.array([2.0 ** (-8.0 * (i + 1) / n) for i in range(n)], dtype=np.float32))


def lambda_init_for(layer):
    return 0.8 - 0.6 * math.exp(-0.3 * layer)


def diff_attention(qa, ka, va, lq1, lk1, lq2, lk2, subln_g, lam_init):
    B, S = qa.shape[:2]
    nb = S // Q_BLOCK
    scale = 1.0 / math.sqrt(HEAD_DIM)
    lam = (jnp.exp(jnp.sum(lq1.astype(jnp.float32) * lk1.astype(jnp.float32)))
           - jnp.exp(jnp.sum(lq2.astype(jnp.float32) * lk2.astype(jnp.float32))) + lam_init)
    slopes = alibi_slopes(DIFF_HEADS)
    q = qa.transpose(0, 2, 3, 1, 4)
    k = ka.transpose(0, 2, 3, 1, 4)
    v = va.transpose(0, 2, 1, 3)
    qb = q.reshape(B, DIFF_HEADS, 2, nb, Q_BLOCK, HEAD_DIM).transpose(3, 0, 1, 2, 4, 5)
    pos_k = jnp.arange(S, dtype=jnp.float32)

    def block(args):
        qblk, i = args
        pos_q = (i * Q_BLOCK + jnp.arange(Q_BLOCK)).astype(jnp.float32)
        dist = jnp.abs(pos_q[:, None] - pos_k[None, :])
        bias = -slopes[:, None, None] * dist
        s = jnp.einsum('bhcqd,bhckd->bhcqk', qblk, k).astype(jnp.float32) * scale + bias[None, :, None]
        p = jax.nn.softmax(s, axis=-1)
        a = p[:, :, 0] - lam * p[:, :, 1]
        return jnp.einsum('bhqk,bhkv->bhqv', a.astype(v.dtype), v)

    o = lax.map(block, (qb, jnp.arange(nb)))
    o = rmsnorm(o, subln_g) * (1.0 - lam_init)
    return o.transpose(1, 0, 3, 2, 4).reshape(B, S, DIFF_WIDTH)


def neighborhood_attention(qn, kn, vn, rpb):
    B, S = qn.shape[:2]
    rows = S // GRID_W
    kr = min(NA_ROWS, rows)
    scale = 1.0 / math.sqrt(HEAD_DIM)
    qg = qn.reshape(B, rows, GRID_W, NA_HEADS, HEAD_DIM).transpose(1, 0, 3, 2, 4)
    kg = kn.reshape(B, rows, GRID_W, NA_HEADS, HEAD_DIM).transpose(0, 3, 1, 2, 4)
    vg = vn.reshape(B, rows, GRID_W, NA_HEADS, HEAD_DIM).transpose(0, 3, 1, 2, 4)
    col = np.arange(GRID_W)
    col_start = np.clip(col - NA_COLS // 2, 0, GRID_W - NA_COLS)
    col_idx = col_start[:, None] + np.arange(NA_COLS)[None, :]
    dc = col_idx - col[:, None] + (NA_COLS - 1)

    def row_block(args):
        q_r, r = args
        rs = jnp.clip(r - kr // 2, 0, rows - kr)
        k_rows = lax.dynamic_slice_in_dim(kg, rs, kr, axis=2)
        v_rows = lax.dynamic_slice_in_dim(vg, rs, kr, axis=2)
        k_win = k_rows[:, :, :, col_idx]
        v_win = v_rows[:, :, :, col_idx]
        dr = rs + jnp.arange(kr) - r + (NA_ROWS - 1)
        bias = rpb[:, dr[:, None, None], dc[None]].astype(jnp.float32)
        s = jnp.einsum('bhcd,bhicjd->bhcij', q_r, k_win).astype(jnp.float32) * scale
        s = s + bias.transpose(0, 2, 1, 3)[None]
        p = jax.nn.softmax(s.reshape(B, NA_HEADS, GRID_W, kr * NA_COLS), axis=-1)
        p = p.reshape(B, NA_HEADS, GRID_W, kr, NA_COLS)
        return jnp.einsum('bhcij,bhicjd->bhcd', p.astype(v_win.dtype), v_win)

    o = lax.map(row_block, (qg, jnp.arange(rows)))
    return o.transpose(1, 0, 3, 2, 4).reshape(B, S, NA_WIDTH)


def memory_cross_attention(h, mem_n, w_mq, w_mkv, w_mo):
    B, S = h.shape[:2]
    M = mem_n.shape[1]
    q = (h @ w_mq).reshape(B, S, MEM_HEADS, MEM_HEAD_DIM)
    kv = (mem_n @ w_mkv).reshape(B, M, 2, MEM_HEADS, MEM_HEAD_DIM)
    k, v = kv[:, :, 0], kv[:, :, 1]
    s = jnp.einsum('bshd,bmhd->bhsm', q, k).astype(jnp.float32) * (1.0 / math.sqrt(MEM_HEAD_DIM))
    p = jax.nn.softmax(s, axis=-1)
    o = jnp.einsum('bhsm,bmhd->bshd', p.astype(v.dtype), v).reshape(B, S, MEM_WIDTH)
    return o @ w_mo


def moe_ffn(h, w_router, b_router, w1, b1, w2, b2):
    N, D = h.shape
    logits = (h @ w_router).astype(jnp.float32) + b_router.astype(jnp.float32)
    top_v, top_i = lax.top_k(logits, TOP_K)
    gates = jax.nn.softmax(top_v, axis=-1)
    e_flat = top_i.reshape(-1)
    g_flat = gates.reshape(-1)
    tok_flat = jnp.arange(N * TOP_K, dtype=jnp.int32) // TOP_K
    order = jnp.argsort(e_flat)
    se, st, sg = e_flat[order], tok_flat[order], g_flat[order]
    counts = jnp.zeros((N_EXPERTS,), jnp.int32).at[e_flat].add(1)
    pcounts = ((counts + MOE_BLOCK - 1) // MOE_BLOCK) * MOE_BLOCK
    starts = jnp.cumsum(counts) - counts
    pends = jnp.cumsum(pcounts)
    pstarts = pends - pcounts
    dest = pstarts[se] + (jnp.arange(N * TOP_K, dtype=jnp.int32) - starts[se])
    P = N * TOP_K + N_EXPERTS * MOE_BLOCK
    nblk = P // MOE_BLOCK
    row_tok = jnp.full((P,), N, jnp.int32).at[dest].set(st)
    h_pad = jnp.concatenate([h, jnp.zeros((1, D), h.dtype)], axis=0)
    xin = h_pad[row_tok].reshape(nblk, MOE_BLOCK, D)
    blk_start = jnp.arange(nblk, dtype=jnp.int32) * MOE_BLOCK
    blk_e = jnp.minimum(jnp.sum(blk_start[:, None] >= pends[None, :], axis=1), N_EXPERTS - 1)

    def expert_block(args):
        xb, e = args
        gu = xb @ w1[e] + b1[e]
        gate = jnp.minimum(gu[:, :D_FF], SWIGLU_LIMIT)
        lin = jnp.clip(gu[:, D_FF:], -SWIGLU_LIMIT, SWIGLU_LIMIT)
        act = gate * jax.nn.sigmoid(SWIGLU_ALPHA * gate) * (lin + 1.0)
        return act @ w2[e] + b2[e]

    y_pad = lax.map(expert_block, (xin, blk_e)).reshape(P, D)
    y = y_pad[dest] * sg[:, None].astype(y_pad.dtype)
    return jax.ops.segment_sum(y, st, num_segments=N)


def setup_inputs(seed: int = 0) -> dict:
    key = jax.random.key(seed)
    ks = jax.random.split(key, 24)
    L = DEPTH
    nrm = lambda k, shape, s: jax.random.normal(k, shape, jnp.float32) * s
    gain = lambda k, shape: 1.0 + nrm(k, shape, 0.05)
    return {
        "x": nrm(ks[0], (BATCH, SEQ, D_MODEL), 1.0),
        "mem": nrm(ks[1], (BATCH, MEM_TOKENS, D_MODEL), 1.0),
        "norm_mix_g": gain(ks[2], (L, D_MODEL)),
        "w_in": nrm(ks[3], (L, D_MODEL, IN_COLS), D_MODEL ** -0.5),
        "lambda_q1": nrm(ks[4], (L, HEAD_DIM), 0.1),
        "lambda_k1": nrm(ks[5], (L, HEAD_DIM), 0.1),
        "lambda_q2": nrm(ks[6], (L, HEAD_DIM), 0.1),
        "lambda_k2": nrm(ks[7], (L, HEAD_DIM), 0.1),
        "subln_g": gain(ks[8], (L, DIFF_VDIM)),
        "rpb": nrm(ks[9], (L, NA_HEADS, 2 * NA_ROWS - 1, 2 * NA_COLS - 1), 0.1),
        "w_out": nrm(ks[10], (L, MIX_WIDTH, D_MODEL), MIX_WIDTH ** -0.5),
        "norm_mem_q_g": gain(ks[11], (L, D_MODEL)),
        "norm_mem_kv_g": gain(ks[12], (L, D_MODEL)),
        "w_mq": nrm(ks[13], (L, D_MODEL, MEM_WIDTH), D_MODEL ** -0.5),
        "w_mkv": nrm(ks[14], (L, D_MODEL, 2 * MEM_WIDTH), D_MODEL ** -0.5),
        "w_mo": nrm(ks[15], (L, MEM_WIDTH, D_MODEL), MEM_WIDTH ** -0.5),
        "norm_ffn_g": gain(ks[16], (L, D_MODEL)),
        "w_router": nrm(ks[17], (L, D_MODEL, N_EXPERTS), D_MODEL ** -0.5),
        "b_router": nrm(ks[18], (L, N_EXPERTS), 0.01),
        "w1": nrm(ks[19], (L, N_EXPERTS, D_MODEL, 2 * D_FF), D_MODEL ** -0.5),
        "b1": nrm(ks[20], (L, N_EXPERTS, 2 * D_FF), 0.01),
        "w2": nrm(ks[21], (L, N_EXPERTS, D_FF, D_MODEL), D_FF ** -0.5),
        "b2": nrm(ks[22], (L, N_EXPERTS, D_MODEL), 0.01),
        "norm_final_g": gain(ks[23], (D_MODEL,)),
    }


def reference(x, mem, norm_mix_g, w_in, lambda_q1, lambda_k1, lambda_q2, lambda_k2, subln_g, rpb,
              w_out, norm_mem_q_g, norm_mem_kv_g, w_mq, w_mkv, w_mo, norm_ffn_g, w_router, b_router,
              w1, b1, w2, b2, norm_final_g):
    B, S = x.shape[:2]
    splits = np.cumsum([DIFF_QK_WIDTH, DIFF_QK_WIDTH, DIFF_WIDTH, NA_WIDTH, NA_WIDTH])
    for l in range(DEPTH):
        h = rmsnorm(x, norm_mix_g[l])
        qkv = h @ w_in[l]
        qa, ka, va, qn, kn, vn = jnp.split(qkv, splits, axis=-1)
        qa = qa.reshape(B, S, DIFF_HEADS, 2, HEAD_DIM)
        ka = ka.reshape(B, S, DIFF_HEADS, 2, HEAD_DIM)
        va = va.reshape(B, S, DIFF_HEADS, DIFF_VDIM)
        qn = qn.reshape(B, S, NA_HEADS, HEAD_DIM)
        kn = kn.reshape(B, S, NA_HEADS, HEAD_DIM)
        vn = vn.reshape(B, S, NA_HEADS, HEAD_DIM)
        o_diff = diff_attention(qa, ka, va, lambda_q1[l], lambda_k1[l], lambda_q2[l], lambda_k2[l],
                                subln_g[l], lambda_init_for(l))
        o_na = neighborhood_attention(qn, kn, vn, rpb[l])
        x = x + jnp.concatenate([o_diff, o_na], axis=-1) @ w_out[l]
        hm = rmsnorm(x, norm_mem_q_g[l])
        mem_n = rmsnorm(mem, norm_mem_kv_g[l])
        x = x + memory_cross_attention(hm, mem_n, w_mq[l], w_mkv[l], w_mo[l])
        hf = rmsnorm(x, norm_ffn_g[l]).reshape(B * S, D_MODEL)
        x = x + moe_ffn(hf, w_router[l], b_router[l], w1[l], b1[l], w2[l], b2[l]).reshape(B, S, D_MODEL)
    return rmsnorm(x, norm_final_g)
```

```python
import functools
import math

import jax
import jax.numpy as jnp
import numpy as np
from jax import lax
from jax.experimental import pallas as pl
from jax.experimental.pallas import tpu as pltpu

HEAD_DIM = 64
DIFF_HEADS = 4
DIFF_VDIM = 2 * HEAD_DIM
DIFF_QK_WIDTH = DIFF_HEADS * 2 * HEAD_DIM
DIFF_WIDTH = DIFF_HEADS * DIFF_VDIM
NA_HEADS = 8
NA_WIDTH = NA_HEADS * HEAD_DIM
GRID_W = 64
NA_ROWS = 8
NA_COLS = 16
MEM_HEADS = 4
MEM_HEAD_DIM = 64
MEM_WIDTH = MEM_HEADS * MEM_HEAD_DIM
N_EXPERTS = 32
TOP_K = 4
SWIGLU_LIMIT = 7.0
SWIGLU_ALPHA = 1.702
RMS_EPS = 1e-5

LANES = 128
SUBLANES = 8
NEG_BIG = -1e30

F32 = jnp.float32
BF16 = jnp.bfloat16

QKV_TM = 512
DIFF_TQ = 256
DIFF_TK = 512
NA_R = 8
POST_TM = 256
DISP_TM = 256
FFN_BLK = 256
COMB_TM = 128
VMEM_LIMIT = 56 * 1024 * 1024


def _params(sem):
    return pltpu.CompilerParams(dimension_semantics=sem, vmem_limit_bytes=VMEM_LIMIT)


def _rms(x, g):
    return x * lax.rsqrt(jnp.mean(x * x, axis=-1, keepdims=True) + RMS_EPS) * g


def _dot(a, b):
    return jnp.dot(a, b, preferred_element_type=F32)


def _dot_nt(a, b):
    return lax.dot_general(a, b, (((1,), (1,)), ((), ())), preferred_element_type=F32)


def _rows_to_tiles(ref, val, rows):
    for j in range(SUBLANES):
        ref[pl.ds(j, rows, stride=SUBLANES), :] = val[:, j * LANES:(j + 1) * LANES]


def _tiles_to_rows(ref, rows, lead=None):
    if lead is None:
        parts = [ref[pl.ds(j, rows, stride=SUBLANES), :] for j in range(SUBLANES)]
    else:
        parts = [ref[lead, pl.ds(j, rows, stride=SUBLANES), :] for j in range(SUBLANES)]
    return jnp.concatenate(parts, axis=1)


def _qkv_kernel(x_ref, g_ref, w_ref, o_ref):
    h = _rms(x_ref[...], g_ref[...]).astype(BF16)
    cols = o_ref.shape[1]
    step = 512
    for c in range(cols // step):
        o_ref[:, c * step:(c + 1) * step] = _dot(h, w_ref[:, c * step:(c + 1) * step]).astype(o_ref.dtype)


def _qkv_proj(x2d, g, w_bf16):
    n, d = x2d.shape
    cols = w_bf16.shape[1]
    return pl.pallas_call(
        _qkv_kernel,
        grid=(n // QKV_TM,),
        in_specs=[pl.BlockSpec((QKV_TM, d), lambda i: (i, 0)),
                  pl.BlockSpec((1, d), lambda i: (0, 0)),
                  pl.BlockSpec((d, cols), lambda i: (0, 0))],
        out_specs=pl.BlockSpec((QKV_TM, cols), lambda i: (i, 0)),
        out_shape=jax.ShapeDtypeStruct((n, cols), BF16),
        compiler_params=_params(("parallel",)),
        name="qkv_proj",
    )(x2d, g.reshape(1, d), w_bf16)


def _diff_kernel(lam_ref, slope_ref, q_ref, k_ref, v_ref, g_ref, o_ref, m_sc, l_sc, acc_sc, *,
                 tq, tk, seq, out_scale):
    h = pl.program_id(1)
    i = pl.program_id(2)
    slope = slope_ref[h]
    lam = lam_ref[0]
    q = q_ref[0] * jnp.asarray(1.0 / math.sqrt(HEAD_DIM), BF16)
    lane = lax.broadcasted_iota(jnp.int32, (tq, LANES), 1)
    zero = jnp.zeros_like(q)
    qs = jnp.concatenate([jnp.where(lane < HEAD_DIM, q, zero), jnp.where(lane >= HEAD_DIM, q, zero)], axis=0)
    m_sc[...] = jnp.full(m_sc.shape, NEG_BIG, F32)
    l_sc[...] = jnp.zeros(l_sc.shape, F32)
    acc_sc[...] = jnp.zeros(acc_sc.shape, F32)
    rel = (lax.broadcasted_iota(jnp.int32, (tq, tk), 0) - lax.broadcasted_iota(jnp.int32, (tq, tk), 1)).astype(F32)

    def body(j, carry):
        start = pl.multiple_of(j * tk, tk)
        kb = k_ref[0, pl.ds(start, tk), :]
        vb = v_ref[0, pl.ds(start, tk), :]
        s = _dot_nt(qs, kb)
        off = (i * tq - j * tk).astype(F32)
        bias = -slope * jnp.abs(rel + off)
        s = (s.reshape(2, tq, tk) + bias[None]).reshape(2 * tq, tk)
        m_prev = m_sc[...]
        m_new = jnp.maximum(m_prev, jnp.max(s, axis=1, keepdims=True))
        alpha = jnp.exp(m_prev - m_new)
        p = jnp.exp(s - m_new)
        l_sc[...] = alpha * l_sc[...] + jnp.sum(p, axis=1, keepdims=True)
        acc_sc[...] = alpha * acc_sc[...] + _dot(p.astype(BF16), vb)
        m_sc[...] = m_new
        return carry

    lax.fori_loop(0, seq // tk, body, 0)
    o = acc_sc[...] / l_sc[...]
    d = o[:tq] - lam * o[tq:]
    o_ref[0] = (_rms(d, g_ref[...]) * out_scale).astype(o_ref.dtype)


def _diff_attn(qkv3, lam, slopes, subln_g, lam_init):
    b, s, _ = qkv3.shape
    tq, tk = DIFF_TQ, DIFF_TK
    kcol = DIFF_QK_WIDTH // LANES
    vcol = 2 * DIFF_QK_WIDTH // LANES
    kern = functools.partial(_diff_kernel, tq=tq, tk=tk, seq=s, out_scale=1.0 - lam_init)
    return pl.pallas_call(
        kern,
        grid=(b, DIFF_HEADS, s // tq),
        in_specs=[pl.BlockSpec(memory_space=pltpu.SMEM),
                  pl.BlockSpec(memory_space=pltpu.SMEM),
                  pl.BlockSpec((1, tq, LANES), lambda bi, h, i: (bi, i, h)),
                  pl.BlockSpec((1, s, LANES), lambda bi, h, i: (bi, 0, kcol + h)),
                  pl.BlockSpec((1, s, LANES), lambda bi, h, i: (bi, 0, vcol + h)),
                  pl.BlockSpec((1, DIFF_VDIM), lambda bi, h, i: (0, 0))],
        out_specs=pl.BlockSpec((1, tq, LANES), lambda bi, h, i: (bi, i, h)),
        out_shape=jax.ShapeDtypeStruct((b, s, DIFF_WIDTH), BF16),
        scratch_shapes=[pltpu.VMEM((2 * tq, 1), F32), pltpu.VMEM((2 * tq, 1), F32),
                        pltpu.VMEM((2 * tq, DIFF_VDIM), F32)],
        compiler_params=_params(("parallel", "parallel", "parallel")),
        name="diff_attn",
    )(lam.reshape(1), slopes, qkv3, qkv3, qkv3, subln_g.reshape(1, DIFF_VDIM))


def _na_bias_table(rpb, rows):
    kr = min(NA_ROWS, rows)
    c = np.arange(kr)[:, None, None, None]
    i = np.arange(kr)[None, :, None, None]
    q = np.arange(GRID_W)[None, None, :, None]
    j = np.arange(GRID_W)[None, None, None, :]
    dr = np.broadcast_to(i - c + (NA_ROWS - 1), (kr, kr, GRID_W, GRID_W))
    cs = np.clip(q - NA_COLS // 2, 0, GRID_W - NA_COLS)
    valid = np.broadcast_to((j >= cs) & (j < cs + NA_COLS), (kr, kr, GRID_W, GRID_W))
    dc = np.broadcast_to(np.clip(j - q + (NA_COLS - 1), 0, 2 * NA_COLS - 2), (kr, kr, GRID_W, GRID_W))
    tbl = rpb.astype(F32)[:, dr, dc]
    tbl = jnp.where(valid[None], tbl, NEG_BIG)
    tbl = tbl.transpose(1, 0, 3, 2, 4).reshape(kr, NA_HEADS // 2, 2 * GRID_W, kr * GRID_W)
    return tbl


def _na_kernel(q_ref, k_ref, v_ref, bias_ref, o_ref, *, rows_per_step, rows, kr):
    i = pl.program_id(2)
    lane = lax.broadcasted_iota(jnp.int32, (GRID_W, LANES), 1)
    win = kr * GRID_W

    def row(rr, carry):
        r = i * rows_per_step + rr
        rs = jnp.clip(r - kr // 2, 0, rows - kr)
        case = r - rs
        q = q_ref[0, pl.ds(pl.multiple_of(rr * GRID_W, GRID_W), GRID_W), :]
        q = q * jnp.asarray(1.0 / math.sqrt(HEAD_DIM), BF16)
        zero = jnp.zeros_like(q)
        qs = jnp.concatenate([jnp.where(lane < HEAD_DIM, q, zero), jnp.where(lane >= HEAD_DIM, q, zero)], axis=0)
        start = pl.multiple_of(rs * GRID_W, GRID_W)
        kw = k_ref[0, pl.ds(start, win), :]
        vw = v_ref[0, pl.ds(start, win), :]
        s = _dot_nt(qs, kw) + bias_ref[case, 0]
        m = jnp.max(s, axis=1, keepdims=True)
        p = jnp.exp(s - m)
        l = jnp.sum(p, axis=1, keepdims=True)
        o = _dot(p.astype(BF16), vw) / l
        out = jnp.where(lane < HEAD_DIM, o[:GRID_W], o[GRID_W:])
        o_ref[0, pl.ds(pl.multiple_of(rr * GRID_W, GRID_W), GRID_W), :] = out.astype(o_ref.dtype)
        return carry

    lax.fori_loop(0, rows_per_step, row, 0)


def _na_attn(qkv3, bias_tbl):
    b, s, _ = qkv3.shape
    rows = s // GRID_W
    kr = min(NA_ROWS, rows)
    base = (2 * DIFF_QK_WIDTH + DIFF_WIDTH) // LANES
    pairs = NA_HEADS // 2
    rstep = NA_R
    kern = functools.partial(_na_kernel, rows_per_step=rstep, rows=rows, kr=kr)
    return pl.pallas_call(
        kern,
        grid=(b, pairs, rows // rstep),
        in_specs=[pl.BlockSpec((1, rstep * GRID_W, LANES), lambda bi, p, i: (bi, i, base + p)),
                  pl.BlockSpec((1, s, LANES), lambda bi, p, i: (bi, 0, base + pairs + p)),
                  pl.BlockSpec((1, s, LANES), lambda bi, p, i: (bi, 0, base + 2 * pairs + p)),
                  pl.BlockSpec((kr, 1, 2 * GRID_W, kr * GRID_W), lambda bi, p, i: (0, p, 0, 0))],
        out_specs=pl.BlockSpec((1, rstep * GRID_W, LANES), lambda bi, p, i: (bi, i, p)),
        out_shape=jax.ShapeDtypeStruct((b, s, NA_WIDTH), BF16),
        compiler_params=_params(("parallel", "parallel", "parallel")),
        name="na_attn",
    )(qkv3, qkv3, qkv3, bias_tbl)


def _memkv_kernel(mem_ref, g_ref, w_ref, o_ref):
    h = _rms(mem_ref[0], g_ref[...]).astype(BF16)
    o_ref[0] = _dot(h, w_ref[...]).astype(o_ref.dtype)


def _mem_kv(mem, g, w_bf16):
    b, m, d = mem.shape
    cols = w_bf16.shape[1]
    return pl.pallas_call(
        _memkv_kernel,
        grid=(b,),
        in_specs=[pl.BlockSpec((1, m, d), lambda i: (i, 0, 0)),
                  pl.BlockSpec((1, d), lambda i: (0, 0)),
                  pl.BlockSpec((d, cols), lambda i: (0, 0))],
        out_specs=pl.BlockSpec((1, m, cols), lambda i: (i, 0, 0)),
        out_shape=jax.ShapeDtypeStruct((b, m, cols), BF16),
        compiler_params=_params(("parallel",)),
        name="mem_kv",
    )(mem, g.reshape(1, d), w_bf16)


def _lane_pack(cols, tm):
    lane = lax.broadcasted_iota(jnp.int32, (tm, LANES), 1)
    out = jnp.zeros((tm, LANES), cols[0].dtype)
    for k, c in enumerate(cols):
        out = jnp.where(lane == k, c, out)
    return out


def _post_kernel(x_ref, od_ref, on_ref, wout_ref, gq_ref, wmq_ref, kv_ref, wmo_ref, gf_ref,
                 wrh_ref, wrl_ref, br_ref,
                 x2_ref, hf_ref, ti_ref, tg_ref, tr_ref, cnt_ref, carry_sc, *, tm):
    step = pl.program_id(0)

    @pl.when(step == 0)
    def _():
        carry_sc[...] = jnp.zeros(carry_sc.shape, F32)

    x1 = x_ref[...] + _dot(od_ref[...], wout_ref[:DIFF_WIDTH, :]) + _dot(on_ref[...], wout_ref[DIFF_WIDTH:, :])

    hm = _rms(x1, gq_ref[...]).astype(BF16)
    q = (_dot(hm, wmq_ref[...]) * (1.0 / math.sqrt(MEM_HEAD_DIM))).astype(BF16)
    head_of_lane = lax.broadcasted_iota(jnp.int32, (tm, MEM_WIDTH), 1) // MEM_HEAD_DIM
    zero = jnp.zeros_like(q)
    qs = jnp.concatenate([jnp.where(head_of_lane == h, q, zero) for h in range(MEM_HEADS)], axis=0)
    kv = kv_ref[0]
    s = _dot_nt(qs, kv[:, :MEM_WIDTH])
    m = jnp.max(s, axis=1, keepdims=True)
    p = jnp.exp(s - m)
    l = jnp.sum(p, axis=1, keepdims=True)
    o4 = _dot(p.astype(BF16), kv[:, MEM_WIDTH:]) / l
    o = jnp.zeros((tm, MEM_WIDTH), F32)
    for h in range(MEM_HEADS):
        o = jnp.where(head_of_lane == h, o4[h * tm:(h + 1) * tm], o)
    x2 = x1 + _dot(o.astype(BF16), wmo_ref[...])
    x2_ref[...] = x2

    hf = _rms(x2, gf_ref[...])
    _rows_to_tiles(hf_ref, hf, tm)
    hi = hf.astype(BF16)
    lo = (hf - hi.astype(F32)).astype(BF16)
    logits = _dot(hi, wrh_ref[...]) + _dot(hi, wrl_ref[...]) + _dot(lo, wrh_ref[...]) + br_ref[...]

    e_iota = lax.broadcasted_iota(jnp.int32, (tm, N_EXPERTS), 1)
    vals, idxs, hots = [], [], []
    cur = logits
    for _ in range(TOP_K):
        mx = jnp.max(cur, axis=1, keepdims=True)
        idx = jnp.min(jnp.where(cur == mx, e_iota, N_EXPERTS), axis=1, keepdims=True)
        hot = e_iota == idx
        vals.append(mx)
        idxs.append(idx)
        hots.append(hot)
        cur = jnp.where(hot, -jnp.inf, cur)
    exps = [jnp.exp(v - vals[0]) for v in vals]
    den = exps[0] + exps[1] + exps[2] + exps[3]
    gates = [e / den for e in exps]

    chosen = jnp.zeros((tm, N_EXPERTS), F32)
    for hot in hots:
        chosen = chosen + hot.astype(F32)
    r_i = lax.broadcasted_iota(jnp.int32, (tm, tm), 0)
    c_i = lax.broadcasted_iota(jnp.int32, (tm, tm), 1)
    strict_lower = (c_i < r_i).astype(BF16)
    before = _dot(strict_lower, chosen.astype(BF16)) + carry_sc[...]
    ranks = [jnp.sum(jnp.where(hot, before, 0.0), axis=1, keepdims=True).astype(jnp.int32) for hot in hots]
    carry_sc[...] = carry_sc[...] + jnp.sum(chosen, axis=0, keepdims=True)
    cnt_ref[...] = carry_sc[...]

    ti_ref[...] = _lane_pack(idxs, tm)
    tg_ref[...] = _lane_pack(gates, tm)
    tr_ref[...] = _lane_pack(ranks, tm)


def _post_attn(x2d, o_diff, o_na, w_out, gq, w_mq, kv, w_mo, gf, wr_hi, wr_lo, b_router, seq):
    n, d = x2d.shape
    tm = POST_TM
    m_tok = kv.shape[1]
    const = lambda i: (0, 0)
    row = lambda i: (i, 0)
    kern = functools.partial(_post_kernel, tm=tm)
    return pl.pallas_call(
        kern,
        grid=(n // tm,),
        in_specs=[pl.BlockSpec((tm, d), row),
                  pl.BlockSpec((tm, DIFF_WIDTH), row),
                  pl.BlockSpec((tm, NA_WIDTH), row),
                  pl.BlockSpec(w_out.shape, const),
                  pl.BlockSpec((1, d), const),
                  pl.BlockSpec(w_mq.shape, const),
                  pl.BlockSpec((1, m_tok, 2 * MEM_WIDTH), lambda i: ((i * tm) // seq, 0, 0)),
                  pl.BlockSpec(w_mo.shape, const),
                  pl.BlockSpec((1, d), const),
                  pl.BlockSpec(wr_hi.shape, const),
                  pl.BlockSpec(wr_lo.shape, const),
                  pl.BlockSpec((1, N_EXPERTS), const)],
        out_specs=[pl.BlockSpec((tm, d), row),
                   pl.BlockSpec((tm * SUBLANES, LANES), row),
                   pl.BlockSpec((tm, LANES), row),
                   pl.BlockSpec((tm, LANES), row),
                   pl.BlockSpec((tm, LANES), row),
                   pl.BlockSpec((1, N_EXPERTS), const)],
        out_shape=[jax.ShapeDtypeStruct((n, d), F32),
                   jax.ShapeDtypeStruct((n * SUBLANES, LANES), F32),
                   jax.ShapeDtypeStruct((n, LANES), jnp.int32),
                   jax.ShapeDtypeStruct((n, LANES), F32),
                   jax.ShapeDtypeStruct((n, LANES), jnp.int32),
                   jax.ShapeDtypeStruct((1, N_EXPERTS), F32)],
        scratch_shapes=[pltpu.VMEM((1, N_EXPERTS), F32)],
        compiler_params=_params(("arbitrary",)),
        name="post_attn",
    )(x2d, o_diff, o_na, w_out, gq.reshape(1, d), w_mq, kv, w_mo, gf.reshape(1, d), wr_hi, wr_lo,
      b_router.reshape(1, N_EXPERTS))


def _row_copy(src, src_row, dst, dst_row, sem):
    return pltpu.make_async_copy(
        src.at[pl.ds(pl.multiple_of(src_row * SUBLANES, SUBLANES), SUBLANES), :],
        dst.at[pl.ds(pl.multiple_of(dst_row * SUBLANES, SUBLANES), SUBLANES), :],
        sem)


def _dispatch_kernel(dest_ref, hf_ref, xin_init_ref, xin_ref, sem, *, tm):
    del xin_init_ref

    def issue(t, carry):
        for k in range(TOP_K):
            _row_copy(hf_ref, t, xin_ref, dest_ref[t * TOP_K + k], sem).start()
        return carry

    lax.fori_loop(0, tm, issue, 0)

    def drain(t, carry):
        for k in range(TOP_K):
            _row_copy(hf_ref, t, xin_ref, dest_ref[t * TOP_K + k], sem).wait()
        return carry

    lax.fori_loop(0, tm, drain, 0)


def _dispatch(dest_flat, hf_tiles, xin_zero):
    n = hf_tiles.shape[0] // SUBLANES
    tm = DISP_TM
    kern = functools.partial(_dispatch_kernel, tm=tm)
    return pl.pallas_call(
        kern,
        grid=(n // tm,),
        in_specs=[pl.BlockSpec((tm * TOP_K,), lambda i: (i,), memory_space=pltpu.SMEM),
                  pl.BlockSpec((tm * SUBLANES, LANES), lambda i: (i, 0)),
                  pl.BlockSpec(memory_space=pl.ANY)],
        out_specs=pl.BlockSpec(memory_space=pl.ANY),
        out_shape=jax.ShapeDtypeStruct(xin_zero.shape, F32),
        scratch_shapes=[pltpu.SemaphoreType.DMA(())],
        input_output_aliases={2: 0},
        compiler_params=_params(("arbitrary",)),
        name="dispatch",
    )(dest_flat, hf_tiles, xin_zero)


def _ffn_kernel(blk_e_ref, nact_ref, xin_ref, w1_ref, b1_ref, w2_ref, b2_ref, y_ref, *, blk, d_ff):
    del blk_e_ref
    i = pl.program_id(0)

    @pl.when(i < nact_ref[0])
    def _():
        x = _tiles_to_rows(xin_ref, blk).astype(BF16)
        gu = _dot(x, w1_ref[0]) + b1_ref[0]
        gate = jnp.minimum(gu[:, :d_ff], SWIGLU_LIMIT)
        lin = jnp.clip(gu[:, d_ff:], -SWIGLU_LIMIT, SWIGLU_LIMIT)
        act = gate * jax.nn.sigmoid(SWIGLU_ALPHA * gate) * (lin + 1.0)
        y = _dot(act.astype(BF16), w2_ref[0]) + b2_ref[0]
        _rows_to_tiles(y_ref, y, blk)

    @pl.when(i >= nact_ref[0])
    def _():
        y_ref[...] = jnp.zeros(y_ref.shape, F32)


def _expert_ffn(blk_e, nact, xin, w1, b1, w2, b2):
    blk = FFN_BLK
    nblk = xin.shape[0] // (blk * SUBLANES)
    _, d, two_ff = w1.shape
    d_ff = two_ff // 2
    kern = functools.partial(_ffn_kernel, blk=blk, d_ff=d_ff)
    grid_spec = pltpu.PrefetchScalarGridSpec(
        num_scalar_prefetch=2,
        grid=(nblk,),
        in_specs=[pl.BlockSpec((blk * SUBLANES, LANES), lambda i, be, na: (i, 0)),
                  pl.BlockSpec((1, d, two_ff), lambda i, be, na: (be[i], 0, 0)),
                  pl.BlockSpec((1, 1, two_ff), lambda i, be, na: (be[i], 0, 0)),
                  pl.BlockSpec((1, d_ff, d), lambda i, be, na: (be[i], 0, 0)),
                  pl.BlockSpec((1, 1, d), lambda i, be, na: (be[i], 0, 0))],
        out_specs=pl.BlockSpec((blk * SUBLANES, LANES), lambda i, be, na: (i, 0)),
    )
    return pl.pallas_call(
        kern,
        grid_spec=grid_spec,
        out_shape=jax.ShapeDtypeStruct(xin.shape, F32),
        compiler_params=_params(("arbitrary",)),
        name="expert_ffn",
    )(blk_e, nact, xin, w1, b1.reshape(N_EXPERTS, 1, two_ff), w2, b2.reshape(N_EXPERTS, 1, d))


def _combine_kernel(dest_ref, x2_ref, tg_ref, gfin_ref, y_ref, o_ref, buf, sem, *, tm, final_norm):
    def issue(t, carry):
        for k in range(TOP_K):
            _row_copy(y_ref, dest_ref[t * TOP_K + k], buf.at[k], t, sem).start()
        return carry

    lax.fori_loop(0, tm, issue, 0)

    def drain(t, carry):
        for k in range(TOP_K):
            _row_copy(y_ref, dest_ref[t * TOP_K + k], buf.at[k], t, sem).wait()
        return carry

    lax.fori_loop(0, tm, drain, 0)

    gates = tg_ref[...]
    acc = x2_ref[...]
    for k in range(TOP_K):
        acc = acc + gates[:, k:k + 1] * _tiles_to_rows(buf, tm, lead=k)
    if final_norm:
        acc = _rms(acc, gfin_ref[...])
    o_ref[...] = acc


def _combine(dest_flat, x2, gates, g_final, y_pad, final_norm):
    n, d = x2.shape
    tm = COMB_TM
    kern = functools.partial(_combine_kernel, tm=tm, final_norm=final_norm)
    return pl.pallas_call(
        kern,
        grid=(n // tm,),
        in_specs=[pl.BlockSpec((tm * TOP_K,), lambda i: (i,), memory_space=pltpu.SMEM),
                  pl.BlockSpec((tm, d), lambda i: (i, 0)),
                  pl.BlockSpec((tm, LANES), lambda i: (i, 0)),
                  pl.BlockSpec((1, d), lambda i: (0, 0)),
                  pl.BlockSpec(memory_space=pl.ANY)],
        out_specs=pl.BlockSpec((tm, d), lambda i: (i, 0)),
        out_shape=jax.ShapeDtypeStruct((n, d), F32),
        scratch_shapes=[pltpu.VMEM((TOP_K, tm * SUBLANES, LANES), F32), pltpu.SemaphoreType.DMA(())],
        compiler_params=_params(("arbitrary",)),
        name="combine",
    )(dest_flat, x2, gates, g_final.reshape(1, d), y_pad)


def _routing_tables(counts_f32, top_i, rank, n_assign):
    blk = FFN_BLK
    counts = counts_f32.astype(jnp.int32)
    pcounts = ((counts + blk - 1) // blk) * blk
    pends = jnp.cumsum(pcounts)
    pstarts = pends - pcounts
    dest = pstarts[top_i] + rank
    p_rows = n_assign + N_EXPERTS * blk
    nblk = p_rows // blk
    blk_start = jnp.arange(nblk, dtype=jnp.int32) * blk
    blk_e = jnp.minimum(jnp.sum(blk_start[:, None] >= pends[None, :], axis=1), N_EXPERTS - 1).astype(jnp.int32)
    nact = (pends[-1] // blk).astype(jnp.int32).reshape(1)
    return dest.reshape(-1).astype(jnp.int32), blk_e, nact, p_rows


def _lambda_init(layer):
    return 0.8 - 0.6 * math.exp(-0.3 * layer)


def kernel(x, mem, norm_mix_g, w_in, lambda_q1, lambda_k1, lambda_q2, lambda_k2, subln_g, rpb, w_out,
           norm_mem_q_g, norm_mem_kv_g, w_mq, w_mkv, w_mo, norm_ffn_g, w_router, b_router, w1, b1, w2, b2,
           norm_final_g):
    b, s, d = x.shape
    n = b * s
    depth = w_in.shape[0]
    assert d == SUBLANES * LANES and s % GRID_W == 0
    slopes = jnp.asarray([2.0 ** (-8.0 * (i + 1) / DIFF_HEADS) for i in range(DIFF_HEADS)], F32)
    x2d = x.reshape(n, d)
    for l in range(depth):
        lam_init = _lambda_init(l)
        lam = (jnp.exp(jnp.sum(lambda_q1[l] * lambda_k1[l])) - jnp.exp(jnp.sum(lambda_q2[l] * lambda_k2[l]))
               + lam_init).astype(F32)
        qkv = _qkv_proj(x2d, norm_mix_g[l], w_in[l].astype(BF16))
        qkv3 = qkv.reshape(b, s, qkv.shape[1])
        o_diff = _diff_attn(qkv3, lam, slopes, subln_g[l], lam_init).reshape(n, DIFF_WIDTH)
        o_na = _na_attn(qkv3, _na_bias_table(rpb[l], s // GRID_W)).reshape(n, NA_WIDTH)
        kv = _mem_kv(mem, norm_mem_kv_g[l], w_mkv[l].astype(BF16))
        wr = w_router[l]
        wr_hi = wr.astype(BF16)
        wr_lo = (wr - wr_hi.astype(F32)).astype(BF16)
        x2, hf_tiles, ti, tg, tr, counts = _post_attn(
            x2d, o_diff, o_na, w_out[l].astype(BF16), norm_mem_q_g[l], w_mq[l].astype(BF16), kv,
            w_mo[l].astype(BF16), norm_ffn_g[l], wr_hi, wr_lo, b_router[l], s)
        dest, blk_e, nact, p_rows = _routing_tables(counts[0], ti[:, :TOP_K], tr[:, :TOP_K], n * TOP_K)
        xin = _dispatch(dest, hf_tiles, jnp.zeros((p_rows * SUBLANES, LANES), F32))
        y_pad = _expert_ffn(blk_e, nact, xin, w1[l].astype(BF16), b1[l], w2[l].astype(BF16), b2[l])
        x2d = _combine(dest, x2, tg, norm_final_g, y_pad, final_norm=(l == depth - 1))
    return x2d.reshape(b, s, d)
```

```python
import functools
import math

import jax
import jax.numpy as jnp
import numpy as np
from jax import lax
from jax.experimental import pallas as pl
from jax.experimental.pallas import tpu as pltpu

HEAD_DIM = 64
DIFF_HEADS = 4
DIFF_VDIM = 2 * HEAD_DIM
DIFF_QK_WIDTH = DIFF_HEADS * 2 * HEAD_DIM
DIFF_WIDTH = DIFF_HEADS * DIFF_VDIM
NA_HEADS = 8
NA_WIDTH = NA_HEADS * HEAD_DIM
GRID_W = 64
NA_ROWS = 8
NA_COLS = 16
MEM_HEADS = 4
MEM_HEAD_DIM = 64
MEM_WIDTH = MEM_HEADS * MEM_HEAD_DIM
N_EXPERTS = 32
TOP_K = 4
SWIGLU_LIMIT = 7.0
SWIGLU_ALPHA = 1.702
RMS_EPS = 1e-5

LANES = 128
SUBLANES = 8
NEG_BIG = -1e30

F32 = jnp.float32
BF16 = jnp.bfloat16

QKV_TM = 512
DIFF_TQ = 256
DIFF_TK = 512
NA_R = 8
POST_TM = 256
DISP_TM = 256
FFN_BLK = 256
COMB_TM = 128
VMEM_LIMIT = 56 * 1024 * 1024


def _params(sem):
    return pltpu.CompilerParams(dimension_semantics=sem, vmem_limit_bytes=VMEM_LIMIT)


def _rms(x, g):
    return x * lax.rsqrt(jnp.mean(x * x, axis=-1, keepdims=True) + RMS_EPS) * g


def _dot(a, b):
    return jnp.dot(a, b, preferred_element_type=F32)


def _dot_nt(a, b):
    return lax.dot_general(a, b, (((1,), (1,)), ((), ())), preferred_element_type=F32)


def _rows_to_tiles(ref, val, rows):
    for j in range(SUBLANES):
        ref[pl.ds(j, rows, stride=SUBLANES), :] = val[:, j * LANES:(j + 1) * LANES]


def _tiles_to_rows(ref, rows, lead=None):
    if lead is None:
        parts = [ref[pl.ds(j, rows, stride=SUBLANES), :] for j in range(SUBLANES)]
    else:
        parts = [ref[lead, pl.ds(j, rows, stride=SUBLANES), :] for j in range(SUBLANES)]
    return jnp.concatenate(parts, axis=1)


def _qkv_kernel(x_ref, g_ref, w_ref, o_ref):
    h = _rms(x_ref[...], g_ref[...]).astype(BF16)
    cols = o_ref.shape[1]
    step = 512
    for c in range(cols // step):
        o_ref[:, c * step:(c + 1) * step] = _dot(h, w_ref[:, c * step:(c + 1) * step]).astype(o_ref.dtype)


def _qkv_proj(x2d, g, w_bf16):
    n, d = x2d.shape
    cols = w_bf16.shape[1]
    return pl.pallas_call(
        _qkv_kernel,
        grid=(n // QKV_TM,),
        in_specs=[pl.BlockSpec((QKV_TM, d), lambda i: (i, 0)),
                  pl.BlockSpec((1, d), lambda i: (0, 0)),
                  pl.BlockSpec((d, cols), lambda i: (0, 0))],
        out_specs=pl.BlockSpec((QKV_TM, cols), lambda i: (i, 0)),
        out_shape=jax.ShapeDtypeStruct((n, cols), BF16),
        compiler_params=_params(("parallel",)),
        name="qkv_proj",
    )(x2d, g.reshape(1, d), w_bf16)


ALIBI_AUG_LEFT, ALIBI_AUG_RIGHT, ALIBI_AUG_NONE = 0, 1, 2
ONES_ROWS = 16


def _bf16_exact_split(n):
    lo_bits = max(0, int(n.max()).bit_length() - 8)
    lo = n % (1 << lo_bits)
    return n - lo, lo


def _alibi_aug_tables(tq, tk):
    ii_hi, ii_lo = _bf16_exact_split(np.arange(tq))
    jj_hi, jj_lo = _bf16_exact_split(np.arange(tk))
    qa = np.zeros((DIFF_HEADS, tq, LANES), np.float32)
    ka = np.zeros((DIFF_HEADS, 3, tk, LANES), np.float32)
    for h in range(DIFF_HEADS):
        sl = 2.0 ** (-8.0 * (h + 1) / DIFF_HEADS)
        qa[h, :, 0], qa[h, :, 1], qa[h, :, 2], qa[h, :, 3] = -sl * ii_hi, -sl * ii_lo, 1.0, 1.0
        qa[h, :, 4], qa[h, :, 5], qa[h, :, 6], qa[h, :, 7] = sl * ii_hi, sl * ii_lo, 1.0, 1.0
        ka[h, ALIBI_AUG_LEFT, :, 0], ka[h, ALIBI_AUG_LEFT, :, 1] = 1.0, 1.0
        ka[h, ALIBI_AUG_LEFT, :, 2], ka[h, ALIBI_AUG_LEFT, :, 3] = sl * jj_hi, sl * jj_lo
        ka[h, ALIBI_AUG_RIGHT, :, 4], ka[h, ALIBI_AUG_RIGHT, :, 5] = 1.0, 1.0
        ka[h, ALIBI_AUG_RIGHT, :, 6], ka[h, ALIBI_AUG_RIGHT, :, 7] = -sl * jj_hi, -sl * jj_lo
    return jnp.asarray(qa, BF16), jnp.asarray(ka, BF16)


def _diff_kernel(lam_ref, slope_ref, q_ref, k_ref, vt_ref, qaug_ref, kaug_ref, g_ref, o_ref,
                 qa_sc, s0_sc, s1_sc, mb_sc, m_sc, acc_sc, *, tq, tk, seq, out_scale):
    h = pl.program_id(1)
    i = pl.program_id(2)
    slope = slope_ref[h]
    lam = lam_ref[0]
    nk = seq // tk
    i0 = i * tq
    jd = i0 // tk

    q = q_ref[0] * jnp.asarray(1.0 / math.sqrt(HEAD_DIM), BF16)
    lane = lax.broadcasted_iota(jnp.int32, (tq, LANES), 1)
    zero = jnp.zeros_like(q)
    qa_sc[:tq, :LANES] = jnp.where(lane < HEAD_DIM, q, zero)
    qa_sc[tq:, :LANES] = jnp.where(lane >= HEAD_DIM, q, zero)
    qa_sc[:tq, LANES:] = qaug_ref[0]
    qa_sc[tq:, LANES:] = qaug_ref[0]
    m_sc[...] = jnp.full(m_sc.shape, NEG_BIG, F32)
    acc_sc[...] = jnp.zeros(acc_sc.shape, F32)

    def block_of(t):
        return jnp.where(t - 1 < jd, t - 1, t)

    def scores(j, side):
        start = pl.multiple_of(j * tk, tk)
        kba = jnp.concatenate([k_ref[0, pl.ds(start, tk), :], kaug_ref[0, side]], axis=1)
        return _dot_nt(kba, qa_sc[...])

    s_bufs = (s0_sc, s1_sc)

    def stash(slot, st):
        s_bufs[slot][...] = st
        mb_sc[slot] = jnp.max(st, axis=0, keepdims=True)

    def consume(slot, j, c):
        m_prev = m_sc[...]
        m_new = jnp.maximum(m_prev, mb_sc[slot] + c)
        alpha = jnp.exp(m_prev - m_new)
        p_t = jnp.exp(s_bufs[slot][...] - (m_new - c)).astype(BF16)
        start = pl.multiple_of(j * tk, tk)
        vta = jnp.concatenate([vt_ref[0, :, pl.ds(start, tk)], jnp.ones((ONES_ROWS, tk), BF16)], axis=0)
        acc_sc[...] = alpha * acc_sc[...] + _dot(vta, p_t)
        m_sc[...] = m_new

    def block_const(j):
        return -slope * jnp.abs(i0 - j * tk).astype(F32)

    rel_t = (lax.broadcasted_iota(jnp.int32, (tk, tq), 0) - lax.broadcasted_iota(jnp.int32, (tk, tq), 1)).astype(F32)
    bias_t = -slope * jnp.abs(rel_t + (jd * tk - i0).astype(F32))
    stash(0, scores(jd, ALIBI_AUG_NONE) + jnp.concatenate([bias_t, bias_t], axis=1))

    def produce(t, slot):
        j = block_of(t)
        stash(slot, scores(j, jnp.where(j < jd, ALIBI_AUG_LEFT, ALIBI_AUG_RIGHT)))

    def consume_step(t, slot):
        j = jnp.where(t == 0, jd, block_of(t))
        consume(slot, j, jnp.where(t == 0, 0.0, block_const(j)))

    def body(u, carry):
        t = 2 * u
        produce(t + 1, 1)
        consume_step(t, 0)
        produce(t + 2, 0)
        consume_step(t + 1, 1)
        return carry

    lax.fori_loop(0, nk // 2 - 1, body, 0)
    produce(nk - 1, 1)
    consume_step(nk - 2, 0)
    consume_step(nk - 1, 1)

    acc = acc_sc[...]
    o = acc[:DIFF_VDIM] * (1.0 / acc[DIFF_VDIM:DIFF_VDIM + 1])
    d = o[:, :tq] - lam * o[:, tq:]
    y = d * lax.rsqrt(jnp.mean(d * d, axis=0, keepdims=True) + RMS_EPS) * g_ref[...] * out_scale
    o_ref[0] = y.T.astype(o_ref.dtype)


def _diff_attn(qkv3, lam, slopes, subln_g, lam_init):
    b, s, _ = qkv3.shape
    tq, tk = DIFF_TQ, DIFF_TK
    assert tk % tq == 0 and s % (2 * tk) == 0
    kcol = DIFF_QK_WIDTH // LANES
    v_t = qkv3[:, :, 2 * DIFF_QK_WIDTH:2 * DIFF_QK_WIDTH + DIFF_WIDTH].transpose(0, 2, 1)
    qaug, kaug = _alibi_aug_tables(tq, tk)
    kern = functools.partial(_diff_kernel, tq=tq, tk=tk, seq=s, out_scale=1.0 - lam_init)
    return pl.pallas_call(
        kern,
        grid=(b, DIFF_HEADS, s // tq),
        in_specs=[pl.BlockSpec(memory_space=pltpu.SMEM),
                  pl.BlockSpec(memory_space=pltpu.SMEM),
                  pl.BlockSpec((1, tq, LANES), lambda bi, h, i: (bi, i, h)),
                  pl.BlockSpec((1, s, LANES), lambda bi, h, i: (bi, 0, kcol + h)),
                  pl.BlockSpec((1, DIFF_VDIM, s), lambda bi, h, i: (bi, h, 0)),
                  pl.BlockSpec((1, tq, LANES), lambda bi, h, i: (h, 0, 0)),
                  pl.BlockSpec((1, 3, tk, LANES), lambda bi, h, i: (h, 0, 0, 0)),
                  pl.BlockSpec((DIFF_VDIM, 1), lambda bi, h, i: (0, 0))],
        out_specs=pl.BlockSpec((1, tq, LANES), lambda bi, h, i: (bi, i, h)),
        out_shape=jax.ShapeDtypeStruct((b, s, DIFF_WIDTH), BF16),
        scratch_shapes=[pltpu.VMEM((2 * tq, 2 * LANES), BF16),
                        pltpu.VMEM((tk, 2 * tq), F32),
                        pltpu.VMEM((tk, 2 * tq), F32),
                        pltpu.VMEM((2, 1, 2 * tq), F32),
                        pltpu.VMEM((1, 2 * tq), F32),
                        pltpu.VMEM((DIFF_VDIM + ONES_ROWS, 2 * tq), F32)],
        compiler_params=_params(("parallel", "parallel", "parallel")),
        name="diff_attn",
    )(lam.reshape(1), slopes, qkv3, qkv3, v_t, qaug, kaug, subln_g.reshape(DIFF_VDIM, 1))


def _na_bias_table(rpb, rows):
    kr = min(NA_ROWS, rows)
    q = np.arange(GRID_W)[:, None]
    j = np.arange(GRID_W)[None, :]
    cs = np.clip(q - NA_COLS // 2, 0, GRID_W - NA_COLS)
    valid = (j >= cs) & (j < cs + NA_COLS)
    dc = j - q + (NA_COLS - 1)
    sel = ((dc[:, :, None] == np.arange(2 * NA_COLS - 1)[None, None, :]) & valid[:, :, None]).astype(np.float32)
    col = jnp.einsum('hrd,qjd->hrqj', rpb.astype(F32), jnp.asarray(sel), precision=lax.Precision.HIGHEST)
    col = jnp.where(valid[None, None], col, NEG_BIG)
    cases = [col[:, NA_ROWS - 1 - c:NA_ROWS - 1 - c + kr] for c in range(kr)]
    tbl = jnp.stack(cases, axis=0).transpose(0, 1, 3, 2, 4)
    return tbl.reshape(kr, NA_HEADS // 2, 2 * GRID_W, kr * GRID_W)


def _na_kernel(q_ref, k_ref, v_ref, bias_ref, o_ref, *, rows_per_step, rows, kr):
    i = pl.program_id(2)
    lane = lax.broadcasted_iota(jnp.int32, (GRID_W, LANES), 1)
    win = kr * GRID_W

    def row(rr, carry):
        r = i * rows_per_step + rr
        rs = jnp.clip(r - kr // 2, 0, rows - kr)
        case = r - rs
        q = q_ref[0, pl.ds(pl.multiple_of(rr * GRID_W, GRID_W), GRID_W), :]
        q = q * jnp.asarray(1.0 / math.sqrt(HEAD_DIM), BF16)
        zero = jnp.zeros_like(q)
        qs = jnp.concatenate([jnp.where(lane < HEAD_DIM, q, zero), jnp.where(lane >= HEAD_DIM, q, zero)], axis=0)
        start = pl.multiple_of(rs * GRID_W, GRID_W)
        kw = k_ref[0, pl.ds(start, win), :]
        vw = v_ref[0, pl.ds(start, win), :]
        s = _dot_nt(qs, kw) + bias_ref[case, 0]
        m = jnp.max(s, axis=1, keepdims=True)
        p = jnp.exp(s - m)
        l = jnp.sum(p, axis=1, keepdims=True)
        o = _dot(p.astype(BF16), vw) / l
        out = jnp.where(lane < HEAD_DIM, o[:GRID_W], o[GRID_W:])
        o_ref[0, pl.ds(pl.multiple_of(rr * GRID_W, GRID_W), GRID_W), :] = out.astype(o_ref.dtype)
        return carry

    lax.fori_loop(0, rows_per_step, row, 0)


def _na_attn(qkv3, bias_tbl):
    b, s, _ = qkv3.shape
    rows = s // GRID_W
    kr = min(NA_ROWS, rows)
    base = (2 * DIFF_QK_WIDTH + DIFF_WIDTH) // LANES
    pairs = NA_HEADS // 2
    rstep = NA_R
    kern = functools.partial(_na_kernel, rows_per_step=rstep, rows=rows, kr=kr)
    return pl.pallas_call(
        kern,
        grid=(b, pairs, rows // rstep),
        in_specs=[pl.BlockSpec((1, rstep * GRID_W, LANES), lambda bi, p, i: (bi, i, base + p)),
                  pl.BlockSpec((1, s, LANES), lambda bi, p, i: (bi, 0, base + pairs + p)),
                  pl.BlockSpec((1, s, LANES), lambda bi, p, i: (bi, 0, base + 2 * pairs + p)),
                  pl.BlockSpec((kr, 1, 2 * GRID_W, kr * GRID_W), lambda bi, p, i: (0, p, 0, 0))],
        out_specs=pl.BlockSpec((1, rstep * GRID_W, LANES), lambda bi, p, i: (bi, i, p)),
        out_shape=jax.ShapeDtypeStruct((b, s, NA_WIDTH), BF16),
        compiler_params=_params(("parallel", "parallel", "parallel")),
        name="na_attn",
    )(qkv3, qkv3, qkv3, bias_tbl)


def _memkv_kernel(mem_ref, g_ref, w_ref, o_ref):
    h = _rms(mem_ref[0], g_ref[...]).astype(BF16)
    o_ref[0] = _dot(h, w_ref[...]).astype(o_ref.dtype)


def _mem_kv(mem, g, w_bf16):
    b, m, d = mem.shape
    cols = w_bf16.shape[1]
    return pl.pallas_call(
        _memkv_kernel,
        grid=(b,),
        in_specs=[pl.BlockSpec((1, m, d), lambda i: (i, 0, 0)),
                  pl.BlockSpec((1, d), lambda i: (0, 0)),
                  pl.BlockSpec((d, cols), lambda i: (0, 0))],
        out_specs=pl.BlockSpec((1, m, cols), lambda i: (i, 0, 0)),
        out_shape=jax.ShapeDtypeStruct((b, m, cols), BF16),
        compiler_params=_params(("parallel",)),
        name="mem_kv",
    )(mem, g.reshape(1, d), w_bf16)


def _lane_pack(cols, tm):
    lane = lax.broadcasted_iota(jnp.int32, (tm, LANES), 1)
    out = jnp.zeros((tm, LANES), cols[0].dtype)
    for k, c in enumerate(cols):
        out = jnp.where(lane == k, c, out)
    return out


def _post_kernel(x_ref, od_ref, on_ref, wout_ref, gq_ref, wmq_ref, kv_ref, wmo_ref, gf_ref,
                 wrh_ref, wrl_ref, br_ref,
                 x2_ref, hf_ref, ti_ref, tg_ref, tr_ref, cnt_ref, carry_sc, *, tm):
    step = pl.program_id(0)

    @pl.when(step == 0)
    def _():
        carry_sc[...] = jnp.zeros(carry_sc.shape, F32)

    x1 = x_ref[...] + _dot(od_ref[...], wout_ref[:DIFF_WIDTH, :]) + _dot(on_ref[...], wout_ref[DIFF_WIDTH:, :])

    hm = _rms(x1, gq_ref[...]).astype(BF16)
    q = (_dot(hm, wmq_ref[...]) * (1.0 / math.sqrt(MEM_HEAD_DIM))).astype(BF16)
    head_of_lane = lax.broadcasted_iota(jnp.int32, (tm, MEM_WIDTH), 1) // MEM_HEAD_DIM
    zero = jnp.zeros_like(q)
    qs = jnp.concatenate([jnp.where(head_of_lane == h, q, zero) for h in range(MEM_HEADS)], axis=0)
    kv = kv_ref[0]
    s = _dot_nt(qs, kv[:, :MEM_WIDTH])
    m = jnp.max(s, axis=1, keepdims=True)
    p = jnp.exp(s - m)
    l = jnp.sum(p, axis=1, keepdims=True)
    o4 = _dot(p.astype(BF16), kv[:, MEM_WIDTH:]) / l
    o = jnp.zeros((tm, MEM_WIDTH), F32)
    for h in range(MEM_HEADS):
        o = jnp.where(head_of_lane == h, o4[h * tm:(h + 1) * tm], o)
    x2 = x1 + _dot(o.astype(BF16), wmo_ref[...])
    x2_ref[...] = x2

    hf = _rms(x2, gf_ref[...])
    _rows_to_tiles(hf_ref, hf, tm)
    hi = hf.astype(BF16)
    lo = (hf - hi.astype(F32)).astype(BF16)
    logits = _dot(hi, wrh_ref[...]) + _dot(hi, wrl_ref[...]) + _dot(lo, wrh_ref[...]) + br_ref[...]

    e_iota = lax.broadcasted_iota(jnp.int32, (tm, N_EXPERTS), 1)
    vals, idxs, hots = [], [], []
    cur = logits
    for _ in range(TOP_K):
        mx = jnp.max(cur, axis=1, keepdims=True)
        idx = jnp.min(jnp.where(cur == mx, e_iota, N_EXPERTS), axis=1, keepdims=True)
        hot = e_iota == idx
        vals.append(mx)
        idxs.append(idx)
        hots.append(hot)
        cur = jnp.where(hot, -jnp.inf, cur)
    exps = [jnp.exp(v - vals[0]) for v in vals]
    den = exps[0] + exps[1] + exps[2] + exps[3]
    gates = [e / den for e in exps]

    chosen = jnp.zeros((tm, N_EXPERTS), F32)
    for hot in hots:
        chosen = chosen + hot.astype(F32)
    r_i = lax.broadcasted_iota(jnp.int32, (tm, tm), 0)
    c_i = lax.broadcasted_iota(jnp.int32, (tm, tm), 1)
    strict_lower = (c_i < r_i).astype(BF16)
    before = _dot(strict_lower, chosen.astype(BF16)) + carry_sc[...]
    ranks = [jnp.sum(jnp.where(hot, before, 0.0), axis=1, keepdims=True).astype(jnp.int32) for hot in hots]
    carry_sc[...] = carry_sc[...] + jnp.sum(chosen, axis=0, keepdims=True)
    cnt_ref[...] = carry_sc[...]

    ti_ref[...] = _lane_pack(idxs, tm)
    tg_ref[...] = _lane_pack(gates, tm)
    tr_ref[...] = _lane_pack(ranks, tm)


def _post_attn(x2d, o_diff, o_na, w_out, gq, w_mq, kv, w_mo, gf, wr_hi, wr_lo, b_router, seq):
    n, d = x2d.shape
    tm = POST_TM
    m_tok = kv.shape[1]
    const = lambda i: (0, 0)
    row = lambda i: (i, 0)
    kern = functools.partial(_post_kernel, tm=tm)
    return pl.pallas_call(
        kern,
        grid=(n // tm,),
        in_specs=[pl.BlockSpec((tm, d), row),
                  pl.BlockSpec((tm, DIFF_WIDTH), row),
                  pl.BlockSpec((tm, NA_WIDTH), row),
                  pl.BlockSpec(w_out.shape, const),
                  pl.BlockSpec((1, d), const),
                  pl.BlockSpec(w_mq.shape, const),
                  pl.BlockSpec((1, m_tok, 2 * MEM_WIDTH), lambda i: ((i * tm) // seq, 0, 0)),
                  pl.BlockSpec(w_mo.shape, const),
                  pl.BlockSpec((1, d), const),
                  pl.BlockSpec(wr_hi.shape, const),
                  pl.BlockSpec(wr_lo.shape, const),
                  pl.BlockSpec((1, N_EXPERTS), const)],
        out_specs=[pl.BlockSpec((tm, d), row),
                   pl.BlockSpec((tm * SUBLANES, LANES), row),
                   pl.BlockSpec((tm, LANES), row),
                   pl.BlockSpec((tm, LANES), row),
                   pl.BlockSpec((tm, LANES), row),
                   pl.BlockSpec((1, N_EXPERTS), const)],
        out_shape=[jax.ShapeDtypeStruct((n, d), F32),
                   jax.ShapeDtypeStruct((n * SUBLANES, LANES), F32),
                   jax.ShapeDtypeStruct((n, LANES), jnp.int32),
                   jax.ShapeDtypeStruct((n, LANES), F32),
                   jax.ShapeDtypeStruct((n, LANES), jnp.int32),
                   jax.ShapeDtypeStruct((1, N_EXPERTS), F32)],
        scratch_shapes=[pltpu.VMEM((1, N_EXPERTS), F32)],
        compiler_params=_params(("arbitrary",)),
        name="post_attn",
    )(x2d, o_diff, o_na, w_out, gq.reshape(1, d), w_mq, kv, w_mo, gf.reshape(1, d), wr_hi, wr_lo,
      b_router.reshape(1, N_EXPERTS))


def _row_copy(src, src_row, dst, dst_row, sem):
    return pltpu.make_async_copy(
        src.at[pl.ds(pl.multiple_of(src_row * SUBLANES, SUBLANES), SUBLANES), :],
        dst.at[pl.ds(pl.multiple_of(dst_row * SUBLANES, SUBLANES), SUBLANES), :],
        sem)


def _dispatch_kernel(dest_ref, hf_ref, xin_init_ref, xin_ref, sem, *, tm):
    del xin_init_ref

    def issue(t, carry):
        for k in range(TOP_K):
            _row_copy(hf_ref, t, xin_ref, dest_ref[t * TOP_K + k], sem).start()
        return carry

    lax.fori_loop(0, tm, issue, 0)

    def drain(t, carry):
        for k in range(TOP_K):
            _row_copy(hf_ref, t, xin_ref, dest_ref[t * TOP_K + k], sem).wait()
        return carry

    lax.fori_loop(0, tm, drain, 0)


def _dispatch(dest_flat, hf_tiles, xin_zero):
    n = hf_tiles.shape[0] // SUBLANES
    tm = DISP_TM
    kern = functools.partial(_dispatch_kernel, tm=tm)
    return pl.pallas_call(
        kern,
        grid=(n // tm,),
        in_specs=[pl.BlockSpec((tm * TOP_K,), lambda i: (i,), memory_space=pltpu.SMEM),
                  pl.BlockSpec((tm * SUBLANES, LANES), lambda i: (i, 0)),
                  pl.BlockSpec(memory_space=pl.ANY)],
        out_specs=pl.BlockSpec(memory_space=pl.ANY),
        out_shape=jax.ShapeDtypeStruct(xin_zero.shape, F32),
        scratch_shapes=[pltpu.SemaphoreType.DMA(())],
        input_output_aliases={2: 0},
        compiler_params=_params(("arbitrary",)),
        name="dispatch",
    )(dest_flat, hf_tiles, xin_zero)


def _ffn_kernel(blk_e_ref, nact_ref, xin_ref, w1_ref, b1_ref, w2_ref, b2_ref, y_ref, *, blk, d_ff):
    del blk_e_ref
    i = pl.program_id(0)

    @pl.when(i < nact_ref[0])
    def _():
        x = _tiles_to_rows(xin_ref, blk).astype(BF16)
        gu = _dot(x, w1_ref[0]) + b1_ref[0]
        gate = jnp.minimum(gu[:, :d_ff], SWIGLU_LIMIT)
        lin = jnp.clip(gu[:, d_ff:], -SWIGLU_LIMIT, SWIGLU_LIMIT)
        act = gate * jax.nn.sigmoid(SWIGLU_ALPHA * gate) * (lin + 1.0)
        y = _dot(act.astype(BF16), w2_ref[0]) + b2_ref[0]
        _rows_to_tiles(y_ref, y, blk)

    @pl.when(i >= nact_ref[0])
    def _():
        y_ref[...] = jnp.zeros(y_ref.shape, F32)


def _expert_ffn(blk_e, nact, xin, w1, b1, w2, b2):
    blk = FFN_BLK
    nblk = xin.shape[0] // (blk * SUBLANES)
    _, d, two_ff = w1.shape
    d_ff = two_ff // 2
    kern = functools.partial(_ffn_kernel, blk=blk, d_ff=d_ff)
    grid_spec = pltpu.PrefetchScalarGridSpec(
        num_scalar_prefetch=2,
        grid=(nblk,),
        in_specs=[pl.BlockSpec((blk * SUBLANES, LANES), lambda i, be, na: (i, 0)),
                  pl.BlockSpec((1, d, two_ff), lambda i, be, na: (be[i], 0, 0)),
                  pl.BlockSpec((1, 1, two_ff), lambda i, be, na: (be[i], 0, 0)),
                  pl.BlockSpec((1, d_ff, d), lambda i, be, na: (be[i], 0, 0)),
                  pl.BlockSpec((1, 1, d), lambda i, be, na: (be[i], 0, 0))],
        out_specs=pl.BlockSpec((blk * SUBLANES, LANES), lambda i, be, na: (i, 0)),
    )
    return pl.pallas_call(
        kern,
        grid_spec=grid_spec,
        out_shape=jax.ShapeDtypeStruct(xin.shape, F32),
        compiler_params=_params(("arbitrary",)),
        name="expert_ffn",
    )(blk_e, nact, xin, w1, b1.reshape(N_EXPERTS, 1, two_ff), w2, b2.reshape(N_EXPERTS, 1, d))


def _combine_kernel(dest_ref, x2_ref, tg_ref, gfin_ref, y_ref, o_ref, buf, sem, *, tm, final_norm):
    def issue(t, carry):
        for k in range(TOP_K):
            _row_copy(y_ref, dest_ref[t * TOP_K + k], buf.at[k], t, sem).start()
        return carry

    lax.fori_loop(0, tm, issue, 0)

    def drain(t, carry):
        for k in range(TOP_K):
            _row_copy(y_ref, dest_ref[t * TOP_K + k], buf.at[k], t, sem).wait()
        return carry

    lax.fori_loop(0, tm, drain, 0)

    gates = tg_ref[...]
    acc = x2_ref[...]
    for k in range(TOP_K):
        acc = acc + gates[:, k:k + 1] * _tiles_to_rows(buf, tm, lead=k)
    if final_norm:
        acc = _rms(acc, gfin_ref[...])
    o_ref[...] = acc


def _combine(dest_flat, x2, gates, g_final, y_pad, final_norm):
    n, d = x2.shape
    tm = COMB_TM
    kern = functools.partial(_combine_kernel, tm=tm, final_norm=final_norm)
    return pl.pallas_call(
        kern,
        grid=(n // tm,),
        in_specs=[pl.BlockSpec((tm * TOP_K,), lambda i: (i,), memory_space=pltpu.SMEM),
                  pl.BlockSpec((tm, d), lambda i: (i, 0)),
                  pl.BlockSpec((tm, LANES), lambda i: (i, 0)),
                  pl.BlockSpec((1, d), lambda i: (0, 0)),
                  pl.BlockSpec(memory_space=pl.ANY)],
        out_specs=pl.BlockSpec((tm, d), lambda i: (i, 0)),
        out_shape=jax.ShapeDtypeStruct((n, d), F32),
        scratch_shapes=[pltpu.VMEM((TOP_K, tm * SUBLANES, LANES), F32), pltpu.SemaphoreType.DMA(())],
        compiler_params=_params(("arbitrary",)),
        name="combine",
    )(dest_flat, x2, gates, g_final.reshape(1, d), y_pad)


def _routing_tables(counts_f32, top_i, rank, n_assign):
    blk = FFN_BLK
    counts = counts_f32.astype(jnp.int32)
    pcounts = ((counts + blk - 1) // blk) * blk
    pends = jnp.cumsum(pcounts)
    pstarts = pends - pcounts
    dest = pstarts[top_i] + rank
    p_rows = n_assign + N_EXPERTS * blk
    nblk = p_rows // blk
    blk_start = jnp.arange(nblk, dtype=jnp.int32) * blk
    blk_e = jnp.minimum(jnp.sum(blk_start[:, None] >= pends[None, :], axis=1), N_EXPERTS - 1).astype(jnp.int32)
    nact = (pends[-1] // blk).astype(jnp.int32).reshape(1)
    return dest.reshape(-1).astype(jnp.int32), blk_e, nact, p_rows


def _lambda_init(layer):
    return 0.8 - 0.6 * math.exp(-0.3 * layer)


def kernel(x, mem, norm_mix_g, w_in, lambda_q1, lambda_k1, lambda_q2, lambda_k2, subln_g, rpb, w_out,
           norm_mem_q_g, norm_mem_kv_g, w_mq, w_mkv, w_mo, norm_ffn_g, w_router, b_router, w1, b1, w2, b2,
           norm_final_g):
    b, s, d = x.shape
    n = b * s
    depth = w_in.shape[0]
    assert d == SUBLANES * LANES and s % GRID_W == 0
    slopes = jnp.asarray([2.0 ** (-8.0 * (i + 1) / DIFF_HEADS) for i in range(DIFF_HEADS)], F32)
    x2d = x.reshape(n, d)
    for l in range(depth):
        lam_init = _lambda_init(l)
        lam = (jnp.exp(jnp.sum(lambda_q1[l] * lambda_k1[l])) - jnp.exp(jnp.sum(lambda_q2[l] * lambda_k2[l]))
               + lam_init).astype(F32)
        qkv = _qkv_proj(x2d, norm_mix_g[l], w_in[l].astype(BF16))
        qkv3 = qkv.reshape(b, s, qkv.shape[1])
        o_diff = _diff_attn(qkv3, lam, slopes, subln_g[l], lam_init).reshape(n, DIFF_WIDTH)
        o_na = _na_attn(qkv3, _na_bias_table(rpb[l], s // GRID_W)).reshape(n, NA_WIDTH)
        kv = _mem_kv(mem, norm_mem_kv_g[l], w_mkv[l].astype(BF16))
        wr = w_router[l]
        wr_hi = wr.astype(BF16)
        wr_lo = (wr - wr_hi.astype(F32)).astype(BF16)
        x2, hf_tiles, ti, tg, tr, counts = _post_attn(
            x2d, o_diff, o_na, w_out[l].astype(BF16), norm_mem_q_g[l], w_mq[l].astype(BF16), kv,
            w_mo[l].astype(BF16), norm_ffn_g[l], wr_hi, wr_lo, b_router[l], s)
        dest, blk_e, nact, p_rows = _routing_tables(counts[0], ti[:, :TOP_K], tr[:, :TOP_K], n * TOP_K)
        xin = _dispatch(dest, hf_tiles, jnp.zeros((p_rows * SUBLANES, LANES), F32))
        y_pad = _expert_ffn(blk_e, nact, xin, w1[l].astype(BF16), b1[l], w2[l].astype(BF16), b2[l])
        x2d = _combine(dest, x2, tg, norm_final_g, y_pad, final_norm=(l == depth - 1))
    return x2d.reshape(b, s, d)
```

```python
import functools
import math

import jax
import jax.numpy as jnp
import numpy as np
from jax import lax
from jax.experimental import pallas as pl
from jax.experimental.pallas import tpu as pltpu

HEAD_DIM = 64
DIFF_HEADS = 4
DIFF_VDIM = 2 * HEAD_DIM
DIFF_QK_WIDTH = DIFF_HEADS * 2 * HEAD_DIM
DIFF_WIDTH = DIFF_HEADS * DIFF_VDIM
NA_HEADS = 8
NA_WIDTH = NA_HEADS * HEAD_DIM
GRID_W = 64
NA_ROWS = 8
NA_COLS = 16
MEM_HEADS = 4
MEM_HEAD_DIM = 64
MEM_WIDTH = MEM_HEADS * MEM_HEAD_DIM
N_EXPERTS = 32
TOP_K = 4
SWIGLU_LIMIT = 7.0
SWIGLU_ALPHA = 1.702
RMS_EPS = 1e-5

LANES = 128
SUBLANES = 8
NEG_BIG = -1e30

F32 = jnp.float32
BF16 = jnp.bfloat16

QKV_TM = 512
DIFF_TQ = 512
DIFF_TK = 512
NA_R = 8
POST_TM = 256
DISP_TM = 256
FFN_BLK = 256
COMB_TM = 128
VMEM_LIMIT = 56 * 1024 * 1024


def _params(sem):
    return pltpu.CompilerParams(dimension_semantics=sem, vmem_limit_bytes=VMEM_LIMIT)


def _rms(x, g):
    return x * lax.rsqrt(jnp.mean(x * x, axis=-1, keepdims=True) + RMS_EPS) * g


def _dot(a, b):
    return jnp.dot(a, b, preferred_element_type=F32)


def _dot_nt(a, b):
    return lax.dot_general(a, b, (((1,), (1,)), ((), ())), preferred_element_type=F32)


def _rows_to_tiles(ref, val, rows):
    for j in range(SUBLANES):
        ref[pl.ds(j, rows, stride=SUBLANES), :] = val[:, j * LANES:(j + 1) * LANES]


def _tiles_to_rows(ref, rows, lead=None):
    if lead is None:
        parts = [ref[pl.ds(j, rows, stride=SUBLANES), :] for j in range(SUBLANES)]
    else:
        parts = [ref[lead, pl.ds(j, rows, stride=SUBLANES), :] for j in range(SUBLANES)]
    return jnp.concatenate(parts, axis=1)


def _qkv_kernel(x_ref, g_ref, w_ref, o_ref):
    h = _rms(x_ref[...], g_ref[...]).astype(BF16)
    cols = o_ref.shape[1]
    step = 512
    for c in range(cols // step):
        o_ref[:, c * step:(c + 1) * step] = _dot(h, w_ref[:, c * step:(c + 1) * step]).astype(o_ref.dtype)


def _qkv_proj(x2d, g, w_bf16):
    n, d = x2d.shape
    cols = w_bf16.shape[1]
    return pl.pallas_call(
        _qkv_kernel,
        grid=(n // QKV_TM,),
        in_specs=[pl.BlockSpec((QKV_TM, d), lambda i: (i, 0)),
                  pl.BlockSpec((1, d), lambda i: (0, 0)),
                  pl.BlockSpec((d, cols), lambda i: (0, 0))],
        out_specs=pl.BlockSpec((QKV_TM, cols), lambda i: (i, 0)),
        out_shape=jax.ShapeDtypeStruct((n, cols), BF16),
        compiler_params=_params(("parallel",)),
        name="qkv_proj",
    )(x2d, g.reshape(1, d), w_bf16)


ALIBI_AUG_LEFT, ALIBI_AUG_RIGHT, ALIBI_AUG_NONE = 0, 1, 2
ONES_ROWS = 16
EXP_UNDERFLOW = 105.0
NORM_BOUND_SLACK = 1.02


def _bf16_exact_split(n):
    lo_bits = max(0, int(n.max()).bit_length() - 8)
    lo = n % (1 << lo_bits)
    return n - lo, lo


def _alibi_aug_tables(tq, tk):
    ii_hi, ii_lo = _bf16_exact_split(np.arange(tq))
    jj_hi, jj_lo = _bf16_exact_split(np.arange(tk))
    qa = np.zeros((DIFF_HEADS, tq, LANES), np.float32)
    ka = np.zeros((DIFF_HEADS, 3, tk, LANES), np.float32)
    for h in range(DIFF_HEADS):
        sl = 2.0 ** (-8.0 * (h + 1) / DIFF_HEADS)
        qa[h, :, 0], qa[h, :, 1], qa[h, :, 2], qa[h, :, 3] = -sl * ii_hi, -sl * ii_lo, 1.0, 1.0
        qa[h, :, 4], qa[h, :, 5], qa[h, :, 6], qa[h, :, 7] = sl * ii_hi, sl * ii_lo, 1.0, 1.0
        ka[h, ALIBI_AUG_LEFT, :, 0], ka[h, ALIBI_AUG_LEFT, :, 1] = 1.0, 1.0
        ka[h, ALIBI_AUG_LEFT, :, 2], ka[h, ALIBI_AUG_LEFT, :, 3] = sl * jj_hi, sl * jj_lo
        ka[h, ALIBI_AUG_RIGHT, :, 4], ka[h, ALIBI_AUG_RIGHT, :, 5] = 1.0, 1.0
        ka[h, ALIBI_AUG_RIGHT, :, 6], ka[h, ALIBI_AUG_RIGHT, :, 7] = -sl * jj_hi, -sl * jj_lo
    return jnp.asarray(qa, BF16), jnp.asarray(ka, BF16)


def _diff_kernel(lam_ref, slope_ref, q_ref, k_ref, vt_ref, qaug_ref, kaug_ref, g_ref, o_ref,
                 qa_sc, s0_sc, s1_sc, mb_sc, m_sc, acc_sc, ksq_sc, *, tq, tk, seq, out_scale):
    h = pl.program_id(1)
    i = pl.program_id(2)
    slope = slope_ref[h]
    lam = lam_ref[0]
    nk = seq // tk
    i0 = i * tq
    jd = i0 // tk

    sel_r = lax.broadcasted_iota(jnp.int32, (LANES, LANES), 0) // HEAD_DIM
    sel_c = lax.broadcasted_iota(jnp.int32, (LANES, LANES), 1)
    sel = (sel_r == sel_c).astype(BF16)

    def max_sq_norm(x):
        xf = x.astype(F32)
        return jnp.max(_dot((xf * xf).astype(BF16), sel), axis=0, keepdims=True)

    @pl.when(i == 0)
    def _():
        ksq_sc[...] = max_sq_norm(k_ref[0])

    q = q_ref[0] * jnp.asarray(1.0 / math.sqrt(HEAD_DIM), BF16)
    lane = lax.broadcasted_iota(jnp.int32, (tq, LANES), 1)
    zero = jnp.zeros_like(q)
    qa_sc[:tq, :LANES] = jnp.where(lane < HEAD_DIM, q, zero)
    qa_sc[tq:, :LANES] = jnp.where(lane >= HEAD_DIM, q, zero)
    qa_sc[:tq, LANES:] = qaug_ref[0]
    qa_sc[tq:, LANES:] = qaug_ref[0]
    m_sc[...] = jnp.full(m_sc.shape, NEG_BIG, F32)
    acc_sc[...] = jnp.zeros(acc_sc.shape, F32)

    def scores(j, side):
        start = pl.multiple_of(j * tk, tk)
        kba = jnp.concatenate([k_ref[0, pl.ds(start, tk), :], kaug_ref[0, side]], axis=1)
        return _dot_nt(kba, qa_sc[...])

    s_bufs = (s0_sc, s1_sc)

    def stash(slot, st):
        s_bufs[slot][...] = st
        mb_sc[slot] = jnp.max(st, axis=0, keepdims=True)

    def consume(slot, j, c):
        m_prev = m_sc[...]
        m_new = jnp.maximum(m_prev, mb_sc[slot] + c)
        alpha = jnp.exp(m_prev - m_new)
        p_t = jnp.exp(s_bufs[slot][...] - (m_new - c)).astype(BF16)
        start = pl.multiple_of(j * tk, tk)
        vta = jnp.concatenate([vt_ref[0, :, pl.ds(start, tk)], jnp.ones((ONES_ROWS, tk), BF16)], axis=0)
        acc_sc[...] = alpha * acc_sc[...] + _dot(vta, p_t)
        m_sc[...] = m_new

    def block_const(j):
        return -slope * jnp.abs(i0 - j * tk).astype(F32)

    rel_t = (lax.broadcasted_iota(jnp.int32, (tk, tq), 0) - lax.broadcasted_iota(jnp.int32, (tk, tq), 1)).astype(F32)
    bias_t = -slope * jnp.abs(rel_t + (jd * tk - i0).astype(F32))
    stash(0, scores(jd, ALIBI_AUG_NONE) + jnp.concatenate([bias_t, bias_t], axis=1))

    qk_bound = jnp.sqrt(jnp.max(max_sq_norm(q) * ksq_sc[...])) * NORM_BOUND_SLACK
    reach = (qk_bound - jnp.min(mb_sc[0]) + EXP_UNDERFLOW) / slope
    reach = jnp.minimum(reach, 2.0 * seq)
    j_lo = jnp.clip(jnp.floor((i0 - tk + 1 - reach) / tk).astype(jnp.int32) + 1, 0, jd)
    j_hi = jnp.clip(jnp.ceil((reach + i0 + tq - 1) / tk).astype(jnp.int32) - 1, jd, nk - 1)
    n_blocks = j_hi - j_lo + 1

    def block_of(t):
        return j_lo + jnp.where(t - 1 < jd - j_lo, t - 1, t)

    def produce(t, slot):
        j = block_of(t)
        stash(slot, scores(j, jnp.where(j < jd, ALIBI_AUG_LEFT, ALIBI_AUG_RIGHT)))

    def consume_step(t, slot):
        j = jnp.where(t == 0, jd, block_of(t))
        consume(slot, j, jnp.where(t == 0, 0.0, block_const(j)))

    def body(u, carry):
        t = 2 * u
        produce(t + 1, 1)
        consume_step(t, 0)
        produce(t + 2, 0)
        consume_step(t + 1, 1)
        return carry

    lax.fori_loop(0, (n_blocks - 1) // 2, body, 0)

    @pl.when(n_blocks % 2 == 0)
    def _():
        produce(n_blocks - 1, 1)
        consume_step(n_blocks - 2, 0)
        consume_step(n_blocks - 1, 1)

    @pl.when(n_blocks % 2 == 1)
    def _():
        consume_step(n_blocks - 1, 0)

    acc = acc_sc[...]
    o = acc[:DIFF_VDIM] * (1.0 / acc[DIFF_VDIM:DIFF_VDIM + 1])
    d = o[:, :tq] - lam * o[:, tq:]
    y = d * lax.rsqrt(jnp.mean(d * d, axis=0, keepdims=True) + RMS_EPS) * g_ref[...] * out_scale
    o_ref[0] = y.T.astype(o_ref.dtype)


def _diff_attn(qkv3, lam, slopes, subln_g, lam_init):
    b, s, _ = qkv3.shape
    tq, tk = DIFF_TQ, DIFF_TK
    assert tk % tq == 0 and s % (2 * tk) == 0
    kcol = DIFF_QK_WIDTH // LANES
    v_t = qkv3[:, :, 2 * DIFF_QK_WIDTH:2 * DIFF_QK_WIDTH + DIFF_WIDTH].transpose(0, 2, 1)
    qaug, kaug = _alibi_aug_tables(tq, tk)
    kern = functools.partial(_diff_kernel, tq=tq, tk=tk, seq=s, out_scale=1.0 - lam_init)
    return pl.pallas_call(
        kern,
        grid=(b, DIFF_HEADS, s // tq),
        in_specs=[pl.BlockSpec(memory_space=pltpu.SMEM),
                  pl.BlockSpec(memory_space=pltpu.SMEM),
                  pl.BlockSpec((1, tq, LANES), lambda bi, h, i: (bi, i, h)),
                  pl.BlockSpec((1, s, LANES), lambda bi, h, i: (bi, 0, kcol + h)),
                  pl.BlockSpec((1, DIFF_VDIM, s), lambda bi, h, i: (bi, h, 0)),
                  pl.BlockSpec((1, tq, LANES), lambda bi, h, i: (h, 0, 0)),
                  pl.BlockSpec((1, 3, tk, LANES), lambda bi, h, i: (h, 0, 0, 0)),
                  pl.BlockSpec((DIFF_VDIM, 1), lambda bi, h, i: (0, 0))],
        out_specs=pl.BlockSpec((1, tq, LANES), lambda bi, h, i: (bi, i, h)),
        out_shape=jax.ShapeDtypeStruct((b, s, DIFF_WIDTH), BF16),
        scratch_shapes=[pltpu.VMEM((2 * tq, 2 * LANES), BF16),
                        pltpu.VMEM((tk, 2 * tq), F32),
                        pltpu.VMEM((tk, 2 * tq), F32),
                        pltpu.VMEM((2, 1, 2 * tq), F32),
                        pltpu.VMEM((1, 2 * tq), F32),
                        pltpu.VMEM((DIFF_VDIM + ONES_ROWS, 2 * tq), F32),
                        pltpu.VMEM((1, LANES), F32)],
        compiler_params=_params(("parallel", "parallel", "arbitrary")),
        name="diff_attn",
    )(lam.reshape(1), slopes, qkv3, qkv3, v_t, qaug, kaug, subln_g.reshape(DIFF_VDIM, 1))


def _na_bias_table(rpb, rows):
    kr = min(NA_ROWS, rows)
    q = np.arange(GRID_W)[:, None]
    j = np.arange(GRID_W)[None, :]
    cs = np.clip(q - NA_COLS // 2, 0, GRID_W - NA_COLS)
    valid = (j >= cs) & (j < cs + NA_COLS)
    dc = j - q + (NA_COLS - 1)
    sel = ((dc[:, :, None] == np.arange(2 * NA_COLS - 1)[None, None, :]) & valid[:, :, None]).astype(np.float32)
    col = jnp.einsum('hrd,qjd->hrqj', rpb.astype(F32), jnp.asarray(sel), precision=lax.Precision.HIGHEST)
    col = jnp.where(valid[None, None], col, NEG_BIG)
    cases = [col[:, NA_ROWS - 1 - c:NA_ROWS - 1 - c + kr] for c in range(kr)]
    tbl = jnp.stack(cases, axis=0).transpose(0, 1, 3, 2, 4)
    return tbl.reshape(kr, NA_HEADS // 2, 2 * GRID_W, kr * GRID_W)


def _na_kernel(q_ref, k_ref, v_ref, bias_ref, o_ref, *, rows_per_step, rows, kr):
    i = pl.program_id(2)
    lane = lax.broadcasted_iota(jnp.int32, (GRID_W, LANES), 1)
    win = kr * GRID_W

    starts, scores, probs = [], [], []
    for rr in range(rows_per_step):
        r = i * rows_per_step + rr
        rs = jnp.clip(r - kr // 2, 0, rows - kr)
        q = q_ref[0, rr * GRID_W:(rr + 1) * GRID_W, :]
        q = q * jnp.asarray(1.0 / math.sqrt(HEAD_DIM), BF16)
        zero = jnp.zeros_like(q)
        qs = jnp.concatenate([jnp.where(lane < HEAD_DIM, q, zero), jnp.where(lane >= HEAD_DIM, q, zero)], axis=0)
        start = pl.multiple_of(rs * GRID_W, GRID_W)
        starts.append(start)
        scores.append(_dot_nt(qs, k_ref[0, pl.ds(start, win), :]) + bias_ref[r - rs, 0])
    for s in scores:
        p = jnp.exp(s - jnp.max(s, axis=1, keepdims=True))
        probs.append((p.astype(BF16), jnp.sum(p, axis=1, keepdims=True)))
    for rr, (p, l) in enumerate(probs):
        o = _dot(p, v_ref[0, pl.ds(starts[rr], win), :]) / l
        out = jnp.where(lane < HEAD_DIM, o[:GRID_W], o[GRID_W:])
        o_ref[0, rr * GRID_W:(rr + 1) * GRID_W, :] = out.astype(o_ref.dtype)


def _na_attn(qkv3, bias_tbl):
    b, s, _ = qkv3.shape
    rows = s // GRID_W
    kr = min(NA_ROWS, rows)
    base = (2 * DIFF_QK_WIDTH + DIFF_WIDTH) // LANES
    pairs = NA_HEADS // 2
    rstep = NA_R
    kern = functools.partial(_na_kernel, rows_per_step=rstep, rows=rows, kr=kr)
    return pl.pallas_call(
        kern,
        grid=(b, pairs, rows // rstep),
        in_specs=[pl.BlockSpec((1, rstep * GRID_W, LANES), lambda bi, p, i: (bi, i, base + p)),
                  pl.BlockSpec((1, s, LANES), lambda bi, p, i: (bi, 0, base + pairs + p)),
                  pl.BlockSpec((1, s, LANES), lambda bi, p, i: (bi, 0, base + 2 * pairs + p)),
                  pl.BlockSpec((kr, 1, 2 * GRID_W, kr * GRID_W), lambda bi, p, i: (0, p, 0, 0))],
        out_specs=pl.BlockSpec((1, rstep * GRID_W, LANES), lambda bi, p, i: (bi, i, p)),
        out_shape=jax.ShapeDtypeStruct((b, s, NA_WIDTH), BF16),
        compiler_params=_params(("parallel", "parallel", "parallel")),
        name="na_attn",
    )(qkv3, qkv3, qkv3, bias_tbl)


def _memkv_kernel(mem_ref, g_ref, w_ref, o_ref):
    h = _rms(mem_ref[0], g_ref[...]).astype(BF16)
    o_ref[0] = _dot(h, w_ref[...]).astype(o_ref.dtype)


def _mem_kv(mem, g, w_bf16):
    b, m, d = mem.shape
    cols = w_bf16.shape[1]
    return pl.pallas_call(
        _memkv_kernel,
        grid=(b,),
        in_specs=[pl.BlockSpec((1, m, d), lambda i: (i, 0, 0)),
                  pl.BlockSpec((1, d), lambda i: (0, 0)),
                  pl.BlockSpec((d, cols), lambda i: (0, 0))],
        out_specs=pl.BlockSpec((1, m, cols), lambda i: (i, 0, 0)),
        out_shape=jax.ShapeDtypeStruct((b, m, cols), BF16),
        compiler_params=_params(("parallel",)),
        name="mem_kv",
    )(mem, g.reshape(1, d), w_bf16)


def _sublane_pack(rows, tm):
    sub = lax.broadcasted_iota(jnp.int32, (SUBLANES, tm), 0)
    out = jnp.zeros((SUBLANES, tm), rows[0].dtype)
    for k, r in enumerate(rows):
        out = jnp.where(sub == k, r, out)
    return out


def _post_kernel(x_ref, od_ref, on_ref, wout_ref, gq_ref, wmq_ref, kv_ref, wmo_ref, gf_ref,
                 wrh_ref, wrl_ref, br_ref,
                 x2_ref, hf_ref, ti_ref, tg_ref, tr_ref, cnt_ref, carry_sc, *, tm):
    step = pl.program_id(0)

    @pl.when(step == 0)
    def _():
        carry_sc[...] = jnp.zeros(carry_sc.shape, F32)

    x1 = x_ref[...] + _dot(od_ref[...], wout_ref[:DIFF_WIDTH, :]) + _dot(on_ref[...], wout_ref[DIFF_WIDTH:, :])

    hm = _rms(x1, gq_ref[...]).astype(BF16)
    q = (_dot(hm, wmq_ref[...]) * (1.0 / math.sqrt(MEM_HEAD_DIM))).astype(BF16)
    head_of_lane = lax.broadcasted_iota(jnp.int32, (tm, MEM_WIDTH), 1) // MEM_HEAD_DIM
    zero = jnp.zeros_like(q)
    qs = jnp.concatenate([jnp.where(head_of_lane == h, q, zero) for h in range(MEM_HEADS)], axis=0)
    kv = kv_ref[0]
    s = _dot_nt(qs, kv[:, :MEM_WIDTH])
    m = jnp.max(s, axis=1, keepdims=True)
    p = jnp.exp(s - m)
    l = jnp.sum(p, axis=1, keepdims=True)
    o4 = _dot(p.astype(BF16), kv[:, MEM_WIDTH:]) / l
    o = jnp.zeros((tm, MEM_WIDTH), F32)
    for h in range(MEM_HEADS):
        o = jnp.where(head_of_lane == h, o4[h * tm:(h + 1) * tm], o)
    x2 = x1 + _dot(o.astype(BF16), wmo_ref[...])
    x2_ref[...] = x2

    hf = _rms(x2, gf_ref[...])
    _rows_to_tiles(hf_ref, hf, tm)
    hi = hf.astype(BF16)
    lo = (hf - hi.astype(F32)).astype(BF16)
    logits = _dot_nt(wrh_ref[...], hi) + _dot_nt(wrl_ref[...], hi) + _dot_nt(wrh_ref[...], lo) + br_ref[...]

    e_iota = lax.broadcasted_iota(jnp.int32, (N_EXPERTS, tm), 0)
    vals, idxs, hots = [], [], []
    cur = logits
    for _ in range(TOP_K):
        mx = jnp.max(cur, axis=0, keepdims=True)
        idx = jnp.min(jnp.where(cur == mx, e_iota, N_EXPERTS), axis=0, keepdims=True)
        hot = e_iota == idx
        vals.append(mx)
        idxs.append(idx)
        hots.append(hot)
        cur = jnp.where(hot, -jnp.inf, cur)
    exps = [jnp.exp(v - vals[0]) for v in vals]
    den = exps[0] + exps[1] + exps[2] + exps[3]
    gates = [e / den for e in exps]

    chosen = jnp.zeros((N_EXPERTS, tm), F32)
    for hot in hots:
        chosen = chosen + hot.astype(F32)
    r_i = lax.broadcasted_iota(jnp.int32, (tm, tm), 0)
    c_i = lax.broadcasted_iota(jnp.int32, (tm, tm), 1)
    earlier = (r_i < c_i).astype(BF16)
    before = _dot(chosen.astype(BF16), earlier) + carry_sc[...]
    ranks = [jnp.sum(jnp.where(hot, before, 0.0), axis=0, keepdims=True).astype(jnp.int32) for hot in hots]
    carry_sc[...] = carry_sc[...] + jnp.sum(chosen, axis=1, keepdims=True)
    cnt_ref[...] = carry_sc[...]

    ti_ref[...] = _sublane_pack(idxs, tm)
    tg_ref[...] = _sublane_pack(gates, tm)
    tr_ref[...] = _sublane_pack(ranks, tm)


def _post_attn(x2d, o_diff, o_na, w_out, gq, w_mq, kv, w_mo, gf, wr_hi_t, wr_lo_t, b_router, seq):
    n, d = x2d.shape
    tm = POST_TM
    m_tok = kv.shape[1]
    const = lambda i: (0, 0)
    row = lambda i: (i, 0)
    col = lambda i: (0, i)
    kern = functools.partial(_post_kernel, tm=tm)
    return pl.pallas_call(
        kern,
        grid=(n // tm,),
        in_specs=[pl.BlockSpec((tm, d), row),
                  pl.BlockSpec((tm, DIFF_WIDTH), row),
                  pl.BlockSpec((tm, NA_WIDTH), row),
                  pl.BlockSpec(w_out.shape, const),
                  pl.BlockSpec((1, d), const),
                  pl.BlockSpec(w_mq.shape, const),
                  pl.BlockSpec((1, m_tok, 2 * MEM_WIDTH), lambda i: ((i * tm) // seq, 0, 0)),
                  pl.BlockSpec(w_mo.shape, const),
                  pl.BlockSpec((1, d), const),
                  pl.BlockSpec(wr_hi_t.shape, const),
                  pl.BlockSpec(wr_lo_t.shape, const),
                  pl.BlockSpec((N_EXPERTS, 1), const)],
        out_specs=[pl.BlockSpec((tm, d), row),
                   pl.BlockSpec((tm * SUBLANES, LANES), row),
                   pl.BlockSpec((SUBLANES, tm), col),
                   pl.BlockSpec((SUBLANES, tm), col),
                   pl.BlockSpec((SUBLANES, tm), col),
                   pl.BlockSpec((N_EXPERTS, 1), const)],
        out_shape=[jax.ShapeDtypeStruct((n, d), F32),
                   jax.ShapeDtypeStruct((n * SUBLANES, LANES), F32),
                   jax.ShapeDtypeStruct((SUBLANES, n), jnp.int32),
                   jax.ShapeDtypeStruct((SUBLANES, n), F32),
                   jax.ShapeDtypeStruct((SUBLANES, n), jnp.int32),
                   jax.ShapeDtypeStruct((N_EXPERTS, 1), F32)],
        scratch_shapes=[pltpu.VMEM((N_EXPERTS, 1), F32)],
        compiler_params=_params(("arbitrary",)),
        name="post_attn",
    )(x2d, o_diff, o_na, w_out, gq.reshape(1, d), w_mq, kv, w_mo, gf.reshape(1, d), wr_hi_t, wr_lo_t,
      b_router.reshape(N_EXPERTS, 1))


def _row_copy(src, src_row, dst, dst_row, sem):
    return pltpu.make_async_copy(
        src.at[pl.ds(pl.multiple_of(src_row * SUBLANES, SUBLANES), SUBLANES), :],
        dst.at[pl.ds(pl.multiple_of(dst_row * SUBLANES, SUBLANES), SUBLANES), :],
        sem)


def _dispatch_kernel(dest_ref, hf_ref, xin_init_ref, xin_ref, sem, *, tm):
    del xin_init_ref

    def issue(t, carry):
        for k in range(TOP_K):
            _row_copy(hf_ref, t, xin_ref, dest_ref[t * TOP_K + k], sem).start(priority=k % 2)
        return carry

    lax.fori_loop(0, tm, issue, 0)

    def drain(t, carry):
        for k in range(TOP_K):
            _row_copy(hf_ref, t, xin_ref, dest_ref[t * TOP_K + k], sem).wait()
        return carry

    lax.fori_loop(0, tm, drain, 0)


def _dispatch(dest_flat, hf_tiles, xin_zero):
    n = hf_tiles.shape[0] // SUBLANES
    tm = DISP_TM
    kern = functools.partial(_dispatch_kernel, tm=tm)
    return pl.pallas_call(
        kern,
        grid=(n // tm,),
        in_specs=[pl.BlockSpec((tm * TOP_K,), lambda i: (i,), memory_space=pltpu.SMEM),
                  pl.BlockSpec((tm * SUBLANES, LANES), lambda i: (i, 0)),
                  pl.BlockSpec(memory_space=pl.ANY)],
        out_specs=pl.BlockSpec(memory_space=pl.ANY),
        out_shape=jax.ShapeDtypeStruct(xin_zero.shape, F32),
        scratch_shapes=[pltpu.SemaphoreType.DMA(())],
        input_output_aliases={2: 0},
        compiler_params=_params(("arbitrary",)),
        name="dispatch",
    )(dest_flat, hf_tiles, xin_zero)


def _ffn_kernel(blk_e_ref, nact_ref, xin_ref, w1_ref, b1_ref, w2_ref, b2_ref, y_ref, w1b_sc, w2b_sc, *,
                blk, d_ff):
    i = pl.program_id(0)

    @pl.when((i == 0) | (blk_e_ref[i] != blk_e_ref[jnp.maximum(i - 1, 0)]))
    def _():
        rows = 256
        for c in range(w1b_sc.shape[0] // rows):
            w1b_sc[c * rows:(c + 1) * rows, :] = w1_ref[0, c * rows:(c + 1) * rows, :].astype(BF16)
        for c in range(w2b_sc.shape[0] // rows):
            w2b_sc[c * rows:(c + 1) * rows, :] = w2_ref[0, c * rows:(c + 1) * rows, :].astype(BF16)

    @pl.when(i < nact_ref[0])
    def _():
        x = _tiles_to_rows(xin_ref, blk).astype(BF16)
        gu = _dot(x, w1b_sc[...]) + b1_ref[0]
        gate = jnp.minimum(gu[:, :d_ff], SWIGLU_LIMIT)
        lin = jnp.clip(gu[:, d_ff:], -SWIGLU_LIMIT, SWIGLU_LIMIT)
        act = gate * jax.nn.sigmoid(SWIGLU_ALPHA * gate) * (lin + 1.0)
        y = _dot(act.astype(BF16), w2b_sc[...]) + b2_ref[0]
        _rows_to_tiles(y_ref, y, blk)

    @pl.when(i >= nact_ref[0])
    def _():
        y_ref[...] = jnp.zeros(y_ref.shape, F32)


def _expert_ffn(blk_e, nact, xin, w1, b1, w2, b2):
    blk = FFN_BLK
    nblk = xin.shape[0] // (blk * SUBLANES)
    _, d, two_ff = w1.shape
    d_ff = two_ff // 2
    kern = functools.partial(_ffn_kernel, blk=blk, d_ff=d_ff)
    grid_spec = pltpu.PrefetchScalarGridSpec(
        num_scalar_prefetch=2,
        grid=(nblk,),
        in_specs=[pl.BlockSpec((blk * SUBLANES, LANES), lambda i, be, na: (jnp.minimum(i, na[0] - 1), 0)),
                  pl.BlockSpec((1, d, two_ff), lambda i, be, na: (be[i], 0, 0)),
                  pl.BlockSpec((1, 1, two_ff), lambda i, be, na: (be[i], 0, 0)),
                  pl.BlockSpec((1, d_ff, d), lambda i, be, na: (be[i], 0, 0)),
                  pl.BlockSpec((1, 1, d), lambda i, be, na: (be[i], 0, 0))],
        out_specs=pl.BlockSpec((blk * SUBLANES, LANES), lambda i, be, na: (i, 0)),
        scratch_shapes=[pltpu.VMEM((d, two_ff), BF16), pltpu.VMEM((d_ff, d), BF16)],
    )
    return pl.pallas_call(
        kern,
        grid_spec=grid_spec,
        out_shape=jax.ShapeDtypeStruct(xin.shape, F32),
        compiler_params=_params(("arbitrary",)),
        name="expert_ffn",
    )(blk_e, nact, xin, w1, b1.reshape(N_EXPERTS, 1, two_ff), w2, b2.reshape(N_EXPERTS, 1, d))


def _combine_kernel(dest_ref, x2_ref, tg_ref, gfin_ref, y_ref, o_ref, buf, sem, *, tm, final_norm):
    def issue(t, carry):
        for k in range(TOP_K):
            _row_copy(y_ref, dest_ref[t * TOP_K + k], buf.at[k], t, sem).start(priority=k % 2)
        return carry

    lax.fori_loop(0, tm, issue, 0)

    def drain(t, carry):
        for k in range(TOP_K):
            _row_copy(y_ref, dest_ref[t * TOP_K + k], buf.at[k], t, sem).wait()
        return carry

    lax.fori_loop(0, tm, drain, 0)

    gates = tg_ref[...]
    acc = x2_ref[...]
    for k in range(TOP_K):
        acc = acc + gates[:, k:k + 1] * _tiles_to_rows(buf, tm, lead=k)
    if final_norm:
        acc = _rms(acc, gfin_ref[...])
    o_ref[...] = acc


def _combine(dest_flat, x2, gates, g_final, y_pad, final_norm):
    n, d = x2.shape
    tm = COMB_TM
    kern = functools.partial(_combine_kernel, tm=tm, final_norm=final_norm)
    return pl.pallas_call(
        kern,
        grid=(n // tm,),
        in_specs=[pl.BlockSpec((tm * TOP_K,), lambda i: (i,), memory_space=pltpu.SMEM),
                  pl.BlockSpec((tm, d), lambda i: (i, 0)),
                  pl.BlockSpec((tm, SUBLANES), lambda i: (i, 0)),
                  pl.BlockSpec((1, d), lambda i: (0, 0)),
                  pl.BlockSpec(memory_space=pl.ANY)],
        out_specs=pl.BlockSpec((tm, d), lambda i: (i, 0)),
        out_shape=jax.ShapeDtypeStruct((n, d), F32),
        scratch_shapes=[pltpu.VMEM((TOP_K, tm * SUBLANES, LANES), F32), pltpu.SemaphoreType.DMA(())],
        compiler_params=_params(("arbitrary",)),
        name="combine",
    )(dest_flat, x2, gates, g_final.reshape(1, d), y_pad)


def _routing_tables(counts_f32, top_i, rank, n_assign):
    blk = FFN_BLK
    counts = counts_f32.astype(jnp.int32)
    pcounts = ((counts + blk - 1) // blk) * blk
    pends = jnp.cumsum(pcounts)
    pstarts = pends - pcounts
    dest = pstarts[top_i] + rank
    p_rows = n_assign + N_EXPERTS * blk
    nblk = p_rows // blk
    blk_start = jnp.arange(nblk, dtype=jnp.int32) * blk
    blk_e = jnp.minimum(jnp.sum(blk_start[:, None] >= pends[None, :], axis=1), N_EXPERTS - 1).astype(jnp.int32)
    nact = (pends[-1] // blk).astype(jnp.int32).reshape(1)
    return dest.reshape(-1).astype(jnp.int32), blk_e, nact, p_rows


def _lambda_init(layer):
    return 0.8 - 0.6 * math.exp(-0.3 * layer)


def kernel(x, mem, norm_mix_g, w_in, lambda_q1, lambda_k1, lambda_q2, lambda_k2, subln_g, rpb, w_out,
           norm_mem_q_g, norm_mem_kv_g, w_mq, w_mkv, w_mo, norm_ffn_g, w_router, b_router, w1, b1, w2, b2,
           norm_final_g):
    b, s, d = x.shape
    n = b * s
    depth = w_in.shape[0]
    assert d == SUBLANES * LANES and s % GRID_W == 0
    slopes = jnp.asarray([2.0 ** (-8.0 * (i + 1) / DIFF_HEADS) for i in range(DIFF_HEADS)], F32)
    x2d = x.reshape(n, d)
    for l in range(depth):
        lam_init = _lambda_init(l)
        lam = (jnp.exp(jnp.sum(lambda_q1[l] * lambda_k1[l])) - jnp.exp(jnp.sum(lambda_q2[l] * lambda_k2[l]))
               + lam_init).astype(F32)
        qkv = _qkv_proj(x2d, norm_mix_g[l], w_in[l].astype(BF16))
        qkv3 = qkv.reshape(b, s, qkv.shape[1])
        o_diff = _diff_attn(qkv3, lam, slopes, subln_g[l], lam_init).reshape(n, DIFF_WIDTH)
        o_na = _na_attn(qkv3, _na_bias_table(rpb[l], s // GRID_W)).reshape(n, NA_WIDTH)
        kv = _mem_kv(mem, norm_mem_kv_g[l], w_mkv[l].astype(BF16))
        wr_t = w_router[l].T
        wr_hi_t = wr_t.astype(BF16)
        wr_lo_t = (wr_t - wr_hi_t.astype(F32)).astype(BF16)
        x2, hf_tiles, ti, tg, tr, counts = _post_attn(
            x2d, o_diff, o_na, w_out[l].astype(BF16), norm_mem_q_g[l], w_mq[l].astype(BF16), kv,
            w_mo[l].astype(BF16), norm_ffn_g[l], wr_hi_t, wr_lo_t, b_router[l], s)
        dest, blk_e, nact, p_rows = _routing_tables(counts[:, 0], ti[:TOP_K].T, tr[:TOP_K].T, n * TOP_K)
        xin = _dispatch(dest, hf_tiles, jnp.zeros((p_rows * SUBLANES, LANES), F32))
        y_pad = _expert_ffn(blk_e, nact, xin, w1[l], b1[l], w2[l], b2[l])
        x2d = _combine(dest, x2, tg.T, norm_final_g, y_pad, final_norm=(l == depth - 1))
    return x2d.reshape(b, s, d)
```

```python
import functools
import math

import jax
import jax.numpy as jnp
import numpy as np
from jax import lax
from jax.experimental import pallas as pl
from jax.experimental.pallas import tpu as pltpu

HEAD_DIM = 64
DIFF_HEADS = 4
DIFF_VDIM = 2 * HEAD_DIM
DIFF_QK_WIDTH = DIFF_HEADS * 2 * HEAD_DIM
DIFF_WIDTH = DIFF_HEADS * DIFF_VDIM
NA_HEADS = 8
NA_WIDTH = NA_HEADS * HEAD_DIM
GRID_W = 64
NA_ROWS = 8
NA_COLS = 16
MEM_HEADS = 4
MEM_HEAD_DIM = 64
MEM_WIDTH = MEM_HEADS * MEM_HEAD_DIM
N_EXPERTS = 32
TOP_K = 4
SWIGLU_LIMIT = 7.0
SWIGLU_ALPHA = 1.702
RMS_EPS = 1e-5

LANES = 128
SUBLANES = 8
NEG_BIG = -1e30

F32 = jnp.float32
BF16 = jnp.bfloat16

QKV_TM = 512
DIFF_TQ = 512
DIFF_TK = 512
NA_R = 8
POST_TM = 256
FFN_BLK = 256
COMB_TM = 256
WAITS_PER_TRIP = 16
VMEM_LIMIT = 56 * 1024 * 1024


def _params(sem):
    return pltpu.CompilerParams(dimension_semantics=sem, vmem_limit_bytes=VMEM_LIMIT)


def _rms(x, g):
    return x * lax.rsqrt(jnp.mean(x * x, axis=-1, keepdims=True) + RMS_EPS) * g


def _dot(a, b):
    return jnp.dot(a, b, preferred_element_type=F32)


def _dot_nt(a, b):
    return lax.dot_general(a, b, (((1,), (1,)), ((), ())), preferred_element_type=F32)


def _rows_to_tiles(ref, val, rows):
    for j in range(SUBLANES):
        ref[pl.ds(j, rows, stride=SUBLANES), :] = val[:, j * LANES:(j + 1) * LANES]


def _tiles_to_rows(ref, rows, lead=None):
    if lead is None:
        parts = [ref[pl.ds(j, rows, stride=SUBLANES), :] for j in range(SUBLANES)]
    else:
        parts = [ref[lead, pl.ds(j, rows, stride=SUBLANES), :] for j in range(SUBLANES)]
    return jnp.concatenate(parts, axis=1)


def _qkv_kernel(x_ref, g_ref, w_ref, o_ref):
    h = _rms(x_ref[...], g_ref[...]).astype(BF16)
    cols = o_ref.shape[1]
    step = 512
    for c in range(cols // step):
        o_ref[:, c * step:(c + 1) * step] = _dot(h, w_ref[:, c * step:(c + 1) * step]).astype(o_ref.dtype)


def _qkv_proj(x2d, g, w_bf16):
    n, d = x2d.shape
    cols = w_bf16.shape[1]
    return pl.pallas_call(
        _qkv_kernel,
        grid=(n // QKV_TM,),
        in_specs=[pl.BlockSpec((QKV_TM, d), lambda i: (i, 0)),
                  pl.BlockSpec((1, d), lambda i: (0, 0)),
                  pl.BlockSpec((d, cols), lambda i: (0, 0))],
        out_specs=pl.BlockSpec((QKV_TM, cols), lambda i: (i, 0)),
        out_shape=jax.ShapeDtypeStruct((n, cols), BF16),
        compiler_params=_params(("parallel",)),
        name="qkv_proj",
    )(x2d, g.reshape(1, d), w_bf16)


ALIBI_AUG_LEFT, ALIBI_AUG_RIGHT, ALIBI_AUG_NONE = 0, 1, 2
ONES_ROWS = 16
EXP_UNDERFLOW = 105.0
NORM_BOUND_SLACK = 1.02


def _bf16_exact_split(n):
    lo_bits = max(0, int(n.max()).bit_length() - 8)
    lo = n % (1 << lo_bits)
    return n - lo, lo


def _alibi_aug_tables(tq, tk):
    ii_hi, ii_lo = _bf16_exact_split(np.arange(tq))
    jj_hi, jj_lo = _bf16_exact_split(np.arange(tk))
    qa = np.zeros((DIFF_HEADS, tq, LANES), np.float32)
    ka = np.zeros((DIFF_HEADS, 3, tk, LANES), np.float32)
    for h in range(DIFF_HEADS):
        sl = 2.0 ** (-8.0 * (h + 1) / DIFF_HEADS)
        qa[h, :, 0], qa[h, :, 1], qa[h, :, 2], qa[h, :, 3] = -sl * ii_hi, -sl * ii_lo, 1.0, 1.0
        qa[h, :, 4], qa[h, :, 5], qa[h, :, 6], qa[h, :, 7] = sl * ii_hi, sl * ii_lo, 1.0, 1.0
        ka[h, ALIBI_AUG_LEFT, :, 0], ka[h, ALIBI_AUG_LEFT, :, 1] = 1.0, 1.0
        ka[h, ALIBI_AUG_LEFT, :, 2], ka[h, ALIBI_AUG_LEFT, :, 3] = sl * jj_hi, sl * jj_lo
        ka[h, ALIBI_AUG_RIGHT, :, 4], ka[h, ALIBI_AUG_RIGHT, :, 5] = 1.0, 1.0
        ka[h, ALIBI_AUG_RIGHT, :, 6], ka[h, ALIBI_AUG_RIGHT, :, 7] = -sl * jj_hi, -sl * jj_lo
    return jnp.asarray(qa, BF16), jnp.asarray(ka, BF16)


def _diff_kernel(lam_ref, slope_ref, q_ref, k_ref, vt_ref, qaug_ref, kaug_ref, g_ref, o_ref,
                 qa_sc, s0_sc, s1_sc, mb_sc, m_sc, acc_sc, ksq_sc, *, tq, tk, seq, out_scale):
    h = pl.program_id(1)
    i = pl.program_id(2)
    slope = slope_ref[h]
    lam = lam_ref[0]
    nk = seq // tk
    i0 = i * tq
    jd = i0 // tk

    sel_r = lax.broadcasted_iota(jnp.int32, (LANES, LANES), 0) // HEAD_DIM
    sel_c = lax.broadcasted_iota(jnp.int32, (LANES, LANES), 1)
    sel = (sel_r == sel_c).astype(BF16)

    def max_sq_norm(x):
        xf = x.astype(F32)
        return jnp.max(_dot((xf * xf).astype(BF16), sel), axis=0, keepdims=True)

    @pl.when(i == 0)
    def _():
        ksq_sc[...] = max_sq_norm(k_ref[0])

    q = q_ref[0] * jnp.asarray(1.0 / math.sqrt(HEAD_DIM), BF16)
    lane = lax.broadcasted_iota(jnp.int32, (tq, LANES), 1)
    zero = jnp.zeros_like(q)
    qa_sc[:tq, :LANES] = jnp.where(lane < HEAD_DIM, q, zero)
    qa_sc[tq:, :LANES] = jnp.where(lane >= HEAD_DIM, q, zero)
    qa_sc[:tq, LANES:] = qaug_ref[0]
    qa_sc[tq:, LANES:] = qaug_ref[0]
    m_sc[...] = jnp.full(m_sc.shape, NEG_BIG, F32)
    acc_sc[...] = jnp.zeros(acc_sc.shape, F32)

    def scores(j, side):
        start = pl.multiple_of(j * tk, tk)
        kba = jnp.concatenate([k_ref[0, pl.ds(start, tk), :], kaug_ref[0, side]], axis=1)
        return _dot_nt(kba, qa_sc[...])

    s_bufs = (s0_sc, s1_sc)

    def stash(slot, st):
        s_bufs[slot][...] = st
        mb_sc[slot] = jnp.max(st, axis=0, keepdims=True)

    def consume(slot, j, c):
        m_prev = m_sc[...]
        m_new = jnp.maximum(m_prev, mb_sc[slot] + c)
        alpha = jnp.exp(m_prev - m_new)
        p_t = jnp.exp(s_bufs[slot][...] - (m_new - c)).astype(BF16)
        start = pl.multiple_of(j * tk, tk)
        vta = jnp.concatenate([vt_ref[0, :, pl.ds(start, tk)], jnp.ones((ONES_ROWS, tk), BF16)], axis=0)
        acc_sc[...] = alpha * acc_sc[...] + _dot(vta, p_t)
        m_sc[...] = m_new

    def block_const(j):
        return -slope * jnp.abs(i0 - j * tk).astype(F32)

    rel_t = (lax.broadcasted_iota(jnp.int32, (tk, tq), 0) - lax.broadcasted_iota(jnp.int32, (tk, tq), 1)).astype(F32)
    bias_t = -slope * jnp.abs(rel_t + (jd * tk - i0).astype(F32))
    stash(0, scores(jd, ALIBI_AUG_NONE) + jnp.concatenate([bias_t, bias_t], axis=1))

    qk_bound = jnp.sqrt(jnp.max(max_sq_norm(q) * ksq_sc[...])) * NORM_BOUND_SLACK
    reach = (qk_bound - jnp.min(mb_sc[0]) + EXP_UNDERFLOW) / slope
    reach = jnp.minimum(reach, 2.0 * seq)
    j_lo = jnp.clip(jnp.floor((i0 - tk + 1 - reach) / tk).astype(jnp.int32) + 1, 0, jd)
    j_hi = jnp.clip(jnp.ceil((reach + i0 + tq - 1) / tk).astype(jnp.int32) - 1, jd, nk - 1)
    n_blocks = j_hi - j_lo + 1

    def block_of(t):
        return j_lo + jnp.where(t - 1 < jd - j_lo, t - 1, t)

    def produce(t, slot):
        j = block_of(t)
        stash(slot, scores(j, jnp.where(j < jd, ALIBI_AUG_LEFT, ALIBI_AUG_RIGHT)))

    def consume_step(t, slot):
        j = jnp.where(t == 0, jd, block_of(t))
        consume(slot, j, jnp.where(t == 0, 0.0, block_const(j)))

    def body(u, carry):
        t = 2 * u
        produce(t + 1, 1)
        consume_step(t, 0)
        produce(t + 2, 0)
        consume_step(t + 1, 1)
        return carry

    lax.fori_loop(0, (n_blocks - 1) // 2, body, 0)

    @pl.when(n_blocks % 2 == 0)
    def _():
        produce(n_blocks - 1, 1)
        consume_step(n_blocks - 2, 0)
        consume_step(n_blocks - 1, 1)

    @pl.when(n_blocks % 2 == 1)
    def _():
        consume_step(n_blocks - 1, 0)

    acc = acc_sc[...]
    o = acc[:DIFF_VDIM] * (1.0 / acc[DIFF_VDIM:DIFF_VDIM + 1])
    d = o[:, :tq] - lam * o[:, tq:]
    y = d * lax.rsqrt(jnp.mean(d * d, axis=0, keepdims=True) + RMS_EPS) * g_ref[...] * out_scale
    o_ref[0] = y.T.astype(o_ref.dtype)


def _diff_attn(qkv3, lam, slopes, subln_g, lam_init):
    b, s, _ = qkv3.shape
    tq, tk = DIFF_TQ, DIFF_TK
    assert tk % tq == 0 and s % (2 * tk) == 0
    kcol = DIFF_QK_WIDTH // LANES
    v_t = qkv3[:, :, 2 * DIFF_QK_WIDTH:2 * DIFF_QK_WIDTH + DIFF_WIDTH].transpose(0, 2, 1)
    qaug, kaug = _alibi_aug_tables(tq, tk)
    kern = functools.partial(_diff_kernel, tq=tq, tk=tk, seq=s, out_scale=1.0 - lam_init)
    return pl.pallas_call(
        kern,
        grid=(b, DIFF_HEADS, s // tq),
        in_specs=[pl.BlockSpec(memory_space=pltpu.SMEM),
                  pl.BlockSpec(memory_space=pltpu.SMEM),
                  pl.BlockSpec((1, tq, LANES), lambda bi, h, i: (bi, i, h)),
                  pl.BlockSpec((1, s, LANES), lambda bi, h, i: (bi, 0, kcol + h)),
                  pl.BlockSpec((1, DIFF_VDIM, s), lambda bi, h, i: (bi, h, 0)),
                  pl.BlockSpec((1, tq, LANES), lambda bi, h, i: (h, 0, 0)),
                  pl.BlockSpec((1, 3, tk, LANES), lambda bi, h, i: (h, 0, 0, 0)),
                  pl.BlockSpec((DIFF_VDIM, 1), lambda bi, h, i: (0, 0))],
        out_specs=pl.BlockSpec((1, tq, LANES), lambda bi, h, i: (bi, i, h)),
        out_shape=jax.ShapeDtypeStruct((b, s, DIFF_WIDTH), BF16),
        scratch_shapes=[pltpu.VMEM((2 * tq, 2 * LANES), BF16),
                        pltpu.VMEM((tk, 2 * tq), F32),
                        pltpu.VMEM((tk, 2 * tq), F32),
                        pltpu.VMEM((2, 1, 2 * tq), F32),
                        pltpu.VMEM((1, 2 * tq), F32),
                        pltpu.VMEM((DIFF_VDIM + ONES_ROWS, 2 * tq), F32),
                        pltpu.VMEM((1, LANES), F32)],
        compiler_params=_params(("parallel", "parallel", "arbitrary")),
        name="diff_attn",
    )(lam.reshape(1), slopes, qkv3, qkv3, v_t, qaug, kaug, subln_g.reshape(DIFF_VDIM, 1))


def _na_bias_table(rpb, rows):
    kr = min(NA_ROWS, rows)
    q = np.arange(GRID_W)[:, None]
    j = np.arange(GRID_W)[None, :]
    cs = np.clip(q - NA_COLS // 2, 0, GRID_W - NA_COLS)
    valid = (j >= cs) & (j < cs + NA_COLS)
    dc = j - q + (NA_COLS - 1)
    sel = ((dc[:, :, None] == np.arange(2 * NA_COLS - 1)[None, None, :]) & valid[:, :, None]).astype(np.float32)
    col = jnp.einsum('hrd,qjd->hrqj', rpb.astype(F32), jnp.asarray(sel), precision=lax.Precision.HIGHEST)
    col = jnp.where(valid[None, None], col, NEG_BIG)
    cases = [col[:, NA_ROWS - 1 - c:NA_ROWS - 1 - c + kr] for c in range(kr)]
    tbl = jnp.stack(cases, axis=0).transpose(0, 1, 3, 2, 4)
    return tbl.reshape(kr, NA_HEADS // 2, 2 * GRID_W, kr * GRID_W)


def _na_kernel(q_ref, k_ref, v_ref, bias_ref, o_ref, *, rows_per_step, rows, kr):
    i = pl.program_id(2)
    lane = lax.broadcasted_iota(jnp.int32, (GRID_W, LANES), 1)
    win = kr * GRID_W

    starts, scores, probs = [], [], []
    for rr in range(rows_per_step):
        r = i * rows_per_step + rr
        rs = jnp.clip(r - kr // 2, 0, rows - kr)
        q = q_ref[0, rr * GRID_W:(rr + 1) * GRID_W, :]
        q = q * jnp.asarray(1.0 / math.sqrt(HEAD_DIM), BF16)
        zero = jnp.zeros_like(q)
        qs = jnp.concatenate([jnp.where(lane < HEAD_DIM, q, zero), jnp.where(lane >= HEAD_DIM, q, zero)], axis=0)
        start = pl.multiple_of(rs * GRID_W, GRID_W)
        starts.append(start)
        scores.append(_dot_nt(qs, k_ref[0, pl.ds(start, win), :]) + bias_ref[r - rs, 0])
    for s in scores:
        p = jnp.exp(s - jnp.max(s, axis=1, keepdims=True))
        probs.append((p.astype(BF16), jnp.sum(p, axis=1, keepdims=True)))
    for rr, (p, l) in enumerate(probs):
        o = _dot(p, v_ref[0, pl.ds(starts[rr], win), :]) / l
        out = jnp.where(lane < HEAD_DIM, o[:GRID_W], o[GRID_W:])
        o_ref[0, rr * GRID_W:(rr + 1) * GRID_W, :] = out.astype(o_ref.dtype)


def _na_attn(qkv3, bias_tbl):
    b, s, _ = qkv3.shape
    rows = s // GRID_W
    kr = min(NA_ROWS, rows)
    base = (2 * DIFF_QK_WIDTH + DIFF_WIDTH) // LANES
    pairs = NA_HEADS // 2
    rstep = NA_R
    kern = functools.partial(_na_kernel, rows_per_step=rstep, rows=rows, kr=kr)
    return pl.pallas_call(
        kern,
        grid=(b, pairs, rows // rstep),
        in_specs=[pl.BlockSpec((1, rstep * GRID_W, LANES), lambda bi, p, i: (bi, i, base + p)),
                  pl.BlockSpec((1, s, LANES), lambda bi, p, i: (bi, 0, base + pairs + p)),
                  pl.BlockSpec((1, s, LANES), lambda bi, p, i: (bi, 0, base + 2 * pairs + p)),
                  pl.BlockSpec((kr, 1, 2 * GRID_W, kr * GRID_W), lambda bi, p, i: (0, p, 0, 0))],
        out_specs=pl.BlockSpec((1, rstep * GRID_W, LANES), lambda bi, p, i: (bi, i, p)),
        out_shape=jax.ShapeDtypeStruct((b, s, NA_WIDTH), BF16),
        compiler_params=_params(("parallel", "parallel", "parallel")),
        name="na_attn",
    )(qkv3, qkv3, qkv3, bias_tbl)


def _memkv_kernel(mem_ref, g_ref, w_ref, o_ref):
    h = _rms(mem_ref[0], g_ref[...]).astype(BF16)
    o_ref[0] = _dot(h, w_ref[...]).astype(o_ref.dtype)


def _mem_kv(mem, g, w_bf16):
    b, m, d = mem.shape
    cols = w_bf16.shape[1]
    return pl.pallas_call(
        _memkv_kernel,
        grid=(b,),
        in_specs=[pl.BlockSpec((1, m, d), lambda i: (i, 0, 0)),
                  pl.BlockSpec((1, d), lambda i: (0, 0)),
                  pl.BlockSpec((d, cols), lambda i: (0, 0))],
        out_specs=pl.BlockSpec((1, m, cols), lambda i: (i, 0, 0)),
        out_shape=jax.ShapeDtypeStruct((b, m, cols), BF16),
        compiler_params=_params(("parallel",)),
        name="mem_kv",
    )(mem, g.reshape(1, d), w_bf16)


def _sublane_pack(rows, tm):
    sub = lax.broadcasted_iota(jnp.int32, (SUBLANES, tm), 0)
    out = jnp.zeros((SUBLANES, tm), rows[0].dtype)
    for k, r in enumerate(rows):
        out = jnp.where(sub == k, r, out)
    return out


def _post_kernel(x_ref, od_ref, on_ref, wout_ref, gq_ref, wmq_ref, kv_ref, wmo_ref, gf_ref,
                 wrh_ref, wrl_ref, br_ref,
                 x2_ref, hf_ref, ti_ref, tg_ref, tr_ref, cnt_ref, carry_sc, *, tm):
    step = pl.program_id(0)

    @pl.when(step == 0)
    def _():
        carry_sc[...] = jnp.zeros(carry_sc.shape, F32)

    x1 = x_ref[...] + _dot(od_ref[...], wout_ref[:DIFF_WIDTH, :]) + _dot(on_ref[...], wout_ref[DIFF_WIDTH:, :])

    hm = _rms(x1, gq_ref[...]).astype(BF16)
    q = (_dot(hm, wmq_ref[...]) * (1.0 / math.sqrt(MEM_HEAD_DIM))).astype(BF16)
    head_of_lane = lax.broadcasted_iota(jnp.int32, (tm, MEM_WIDTH), 1) // MEM_HEAD_DIM
    zero = jnp.zeros_like(q)
    qs = jnp.concatenate([jnp.where(head_of_lane == h, q, zero) for h in range(MEM_HEADS)], axis=0)
    kv = kv_ref[0]
    s = _dot_nt(qs, kv[:, :MEM_WIDTH])
    m = jnp.max(s, axis=1, keepdims=True)
    p = jnp.exp(s - m)
    l = jnp.sum(p, axis=1, keepdims=True)
    o4 = _dot(p.astype(BF16), kv[:, MEM_WIDTH:]) / l
    o = jnp.zeros((tm, MEM_WIDTH), F32)
    for h in range(MEM_HEADS):
        o = jnp.where(head_of_lane == h, o4[h * tm:(h + 1) * tm], o)
    x2 = x1 + _dot(o.astype(BF16), wmo_ref[...])
    x2_ref[...] = x2

    hf = _rms(x2, gf_ref[...])
    _rows_to_tiles(hf_ref, hf, tm)
    hi = hf.astype(BF16)
    lo = (hf - hi.astype(F32)).astype(BF16)
    logits = _dot_nt(wrh_ref[...], hi) + _dot_nt(wrl_ref[...], hi) + _dot_nt(wrh_ref[...], lo) + br_ref[...]

    e_iota = lax.broadcasted_iota(jnp.int32, (N_EXPERTS, tm), 0)
    vals, idxs, hots = [], [], []
    cur = logits
    for _ in range(TOP_K):
        mx = jnp.max(cur, axis=0, keepdims=True)
        idx = jnp.min(jnp.where(cur == mx, e_iota, N_EXPERTS), axis=0, keepdims=True)
        hot = e_iota == idx
        vals.append(mx)
        idxs.append(idx)
        hots.append(hot)
        cur = jnp.where(hot, -jnp.inf, cur)
    exps = [jnp.exp(v - vals[0]) for v in vals]
    den = exps[0] + exps[1] + exps[2] + exps[3]
    gates = [e / den for e in exps]

    chosen = jnp.zeros((N_EXPERTS, tm), F32)
    for hot in hots:
        chosen = chosen + hot.astype(F32)
    r_i = lax.broadcasted_iota(jnp.int32, (tm, tm), 0)
    c_i = lax.broadcasted_iota(jnp.int32, (tm, tm), 1)
    earlier = (r_i < c_i).astype(BF16)
    before = _dot(chosen.astype(BF16), earlier) + carry_sc[...]
    ranks = [jnp.sum(jnp.where(hot, before, 0.0), axis=0, keepdims=True).astype(jnp.int32) for hot in hots]
    carry_sc[...] = carry_sc[...] + jnp.sum(chosen, axis=1, keepdims=True)
    cnt_ref[...] = carry_sc[...]

    ti_ref[...] = _sublane_pack(idxs, tm)
    tg_ref[...] = _sublane_pack(gates, tm)
    tr_ref[...] = _sublane_pack(ranks, tm)


def _post_attn(x2d, o_diff, o_na, w_out, gq, w_mq, kv, w_mo, gf, wr_hi_t, wr_lo_t, b_router, seq):
    n, d = x2d.shape
    tm = POST_TM
    m_tok = kv.shape[1]
    const = lambda i: (0, 0)
    row = lambda i: (i, 0)
    col = lambda i: (0, i)
    kern = functools.partial(_post_kernel, tm=tm)
    return pl.pallas_call(
        kern,
        grid=(n // tm,),
        in_specs=[pl.BlockSpec((tm, d), row),
                  pl.BlockSpec((tm, DIFF_WIDTH), row),
                  pl.BlockSpec((tm, NA_WIDTH), row),
                  pl.BlockSpec(w_out.shape, const),
                  pl.BlockSpec((1, d), const),
                  pl.BlockSpec(w_mq.shape, const),
                  pl.BlockSpec((1, m_tok, 2 * MEM_WIDTH), lambda i: ((i * tm) // seq, 0, 0)),
                  pl.BlockSpec(w_mo.shape, const),
                  pl.BlockSpec((1, d), const),
                  pl.BlockSpec(wr_hi_t.shape, const),
                  pl.BlockSpec(wr_lo_t.shape, const),
                  pl.BlockSpec((N_EXPERTS, 1), const)],
        out_specs=[pl.BlockSpec((tm, d), row),
                   pl.BlockSpec((tm * SUBLANES, LANES), row),
                   pl.BlockSpec((SUBLANES, tm), col),
                   pl.BlockSpec((SUBLANES, tm), col),
                   pl.BlockSpec((SUBLANES, tm), col),
                   pl.BlockSpec((N_EXPERTS, 1), const)],
        out_shape=[jax.ShapeDtypeStruct((n, d), F32),
                   jax.ShapeDtypeStruct((n * SUBLANES, LANES), F32),
                   jax.ShapeDtypeStruct((SUBLANES, n), jnp.int32),
                   jax.ShapeDtypeStruct((SUBLANES, n), F32),
                   jax.ShapeDtypeStruct((SUBLANES, n), jnp.int32),
                   jax.ShapeDtypeStruct((N_EXPERTS, 1), F32)],
        scratch_shapes=[pltpu.VMEM((N_EXPERTS, 1), F32)],
        compiler_params=_params(("arbitrary",)),
        name="post_attn",
    )(x2d, o_diff, o_na, w_out, gq.reshape(1, d), w_mq, kv, w_mo, gf.reshape(1, d), wr_hi_t, wr_lo_t,
      b_router.reshape(N_EXPERTS, 1))


def _row_copy(src, src_row, dst, dst_row, sem):
    return pltpu.make_async_copy(
        src.at[pl.ds(pl.multiple_of(src_row * SUBLANES, SUBLANES), SUBLANES), :],
        dst.at[pl.ds(pl.multiple_of(dst_row * SUBLANES, SUBLANES), SUBLANES), :],
        sem)


def _ffn_kernel(blk_e_ref, nact_ref, src_cur_ref, src_next_ref, dst_prev_ref, hf_ref,
                w1_ref, b1_ref, w2_ref, b2_ref, yc_ref,
                xbuf, ybuf, w1b_sc, w2b_sc, gsem, ssem, *, blk, d_ff, trash_row):
    i = pl.program_id(0)
    nact = nact_ref[0]
    slot = i % 2

    def gather(src_ref, t, to_slot):
        return _row_copy(hf_ref, src_ref[t], xbuf.at[to_slot], t, gsem.at[to_slot])

    def scatter(t, from_slot):
        dst = jnp.where(i == 0, trash_row + t, dst_prev_ref[t])
        return _row_copy(ybuf.at[from_slot], t, yc_ref, dst, ssem)

    @pl.when(i == 0)
    def _():
        ybuf[...] = jnp.zeros(ybuf.shape, F32)
        for t in range(blk):
            gather(src_cur_ref, t, 0).start(priority=t % 2)

    @pl.when((i < nact) & ((i == 0) | (blk_e_ref[i] != blk_e_ref[jnp.maximum(i - 1, 0)])))
    def _():
        rows = 256
        for c in range(w1b_sc.shape[0] // rows):
            w1b_sc[c * rows:(c + 1) * rows, :] = w1_ref[0, c * rows:(c + 1) * rows, :].astype(BF16)
        for c in range(w2b_sc.shape[0] // rows):
            w2b_sc[c * rows:(c + 1) * rows, :] = w2_ref[0, c * rows:(c + 1) * rows, :].astype(BF16)

    @pl.when(i <= nact)
    def _():
        def drain(t8, carry):
            for u in range(WAITS_PER_TRIP):
                gather(src_cur_ref, t8 * WAITS_PER_TRIP + u, slot).wait()
            return carry
        lax.fori_loop(0, blk // WAITS_PER_TRIP, drain, 0)

    @pl.when(i < nact)
    def _():
        for t in range(blk):
            gather(src_next_ref, t, 1 - slot).start(priority=t % 2)
            scatter(t, 1 - slot).start(priority=(t + 1) % 2)
        x = _tiles_to_rows(xbuf, blk, lead=slot).astype(BF16)
        gu = _dot(x, w1b_sc[...]) + b1_ref[0]
        gate = jnp.minimum(gu[:, :d_ff], SWIGLU_LIMIT)
        lin = jnp.clip(gu[:, d_ff:], -SWIGLU_LIMIT, SWIGLU_LIMIT)
        act = gate * jax.nn.sigmoid(SWIGLU_ALPHA * gate) * (lin + 1.0)
        y = _dot(act.astype(BF16), w2b_sc[...]) + b2_ref[0]
        _rows_to_tiles(ybuf.at[slot], y, blk)

        def drain(t8, carry):
            for u in range(WAITS_PER_TRIP):
                scatter(t8 * WAITS_PER_TRIP + u, 1 - slot).wait()
            return carry
        lax.fori_loop(0, blk // WAITS_PER_TRIP, drain, 0)

    @pl.when(i == nact)
    def _():
        def issue(t, carry):
            scatter(t, 1 - slot).start()
            return carry
        lax.fori_loop(0, blk, issue, 0)

        def drain(t8, carry):
            for u in range(WAITS_PER_TRIP):
                scatter(t8 * WAITS_PER_TRIP + u, 1 - slot).wait()
            return carry
        lax.fori_loop(0, blk // WAITS_PER_TRIP, drain, 0)


def _expert_ffn(blk_e, nact, src_tok, dst_slot, hf_tiles, w1, b1, w2, b2, n_slots):
    blk = FFN_BLK
    nblk = src_tok.shape[0] // blk
    _, d, two_ff = w1.shape
    d_ff = two_ff // 2
    kern = functools.partial(_ffn_kernel, blk=blk, d_ff=d_ff, trash_row=n_slots)
    smem_blk = lambda f: pl.BlockSpec((blk,), f, memory_space=pltpu.SMEM)
    grid_spec = pltpu.PrefetchScalarGridSpec(
        num_scalar_prefetch=2,
        grid=(nblk,),
        in_specs=[smem_blk(lambda i, be, na: (i,)),
                  smem_blk(lambda i, be, na: (jnp.minimum(i + 1, nblk - 1),)),
                  smem_blk(lambda i, be, na: (jnp.maximum(i - 1, 0),)),
                  pl.BlockSpec(memory_space=pl.ANY),
                  pl.BlockSpec((1, d, two_ff), lambda i, be, na: (be[i], 0, 0)),
                  pl.BlockSpec((1, 1, two_ff), lambda i, be, na: (be[i], 0, 0)),
                  pl.BlockSpec((1, d_ff, d), lambda i, be, na: (be[i], 0, 0)),
                  pl.BlockSpec((1, 1, d), lambda i, be, na: (be[i], 0, 0))],
        out_specs=pl.BlockSpec(memory_space=pl.ANY),
        scratch_shapes=[pltpu.VMEM((2, blk * SUBLANES, LANES), F32),
                        pltpu.VMEM((2, blk * SUBLANES, LANES), F32),
                        pltpu.VMEM((d, two_ff), BF16),
                        pltpu.VMEM((d_ff, d), BF16),
                        pltpu.SemaphoreType.DMA((2,)),
                        pltpu.SemaphoreType.DMA(())],
    )
    return pl.pallas_call(
        kern,
        grid_spec=grid_spec,
        out_shape=jax.ShapeDtypeStruct(((n_slots + blk) * SUBLANES, LANES), F32),
        compiler_params=_params(("arbitrary",)),
        name="expert_ffn",
    )(blk_e, nact, src_tok, src_tok, dst_slot, hf_tiles, w1, b1.reshape(N_EXPERTS, 1, two_ff), w2,
      b2.reshape(N_EXPERTS, 1, d))


def _combine_kernel(x2_ref, tg_ref, gfin_ref, yc_ref, o_ref, *, tm, final_norm):
    gates = tg_ref[...]
    pieces = []
    for j in range(SUBLANES):
        acc = x2_ref[:, j * LANES:(j + 1) * LANES]
        for k in range(TOP_K):
            acc = acc + gates[:, k:k + 1] * yc_ref[pl.ds(k * SUBLANES + j, tm, stride=TOP_K * SUBLANES), :]
        pieces.append(acc)
    out = jnp.concatenate(pieces, axis=1)
    if final_norm:
        out = _rms(out, gfin_ref[...])
    o_ref[...] = out


def _combine(x2, gates, g_final, y_slots, final_norm):
    n, d = x2.shape
    tm = COMB_TM
    kern = functools.partial(_combine_kernel, tm=tm, final_norm=final_norm)
    return pl.pallas_call(
        kern,
        grid=(n // tm,),
        in_specs=[pl.BlockSpec((tm, d), lambda i: (i, 0)),
                  pl.BlockSpec((tm, SUBLANES), lambda i: (i, 0)),
                  pl.BlockSpec((1, d), lambda i: (0, 0)),
                  pl.BlockSpec((tm * TOP_K * SUBLANES, LANES), lambda i: (i, 0))],
        out_specs=pl.BlockSpec((tm, d), lambda i: (i, 0)),
        out_shape=jax.ShapeDtypeStruct((n, d), F32),
        compiler_params=_params(("parallel",)),
        name="combine",
    )(x2, gates, g_final.reshape(1, d), y_slots)


def _routing_tables(counts_f32, top_i, rank, n_tok):
    blk = FFN_BLK
    n_assign = n_tok * TOP_K
    counts = counts_f32.astype(jnp.int32)
    pcounts = ((counts + blk - 1) // blk) * blk
    pends = jnp.cumsum(pcounts)
    pstarts = pends - pcounts
    dest = (pstarts[top_i] + rank).reshape(-1)
    p_rows = n_assign + N_EXPERTS * blk
    nblk = p_rows // blk
    row = jnp.arange(p_rows, dtype=jnp.int32)
    slot = jnp.arange(n_assign, dtype=jnp.int32)
    src_tok = (row % n_tok).at[dest].set(slot // TOP_K)
    dst_slot = (n_assign + row % blk).at[dest].set(slot)
    blk_start = jnp.arange(nblk, dtype=jnp.int32) * blk
    blk_e = jnp.minimum(jnp.sum(blk_start[:, None] >= pends[None, :], axis=1), N_EXPERTS - 1).astype(jnp.int32)
    nact = (pends[-1] // blk).astype(jnp.int32).reshape(1)
    return src_tok.astype(jnp.int32), dst_slot.astype(jnp.int32), blk_e, nact


def _lambda_init(layer):
    return 0.8 - 0.6 * math.exp(-0.3 * layer)


def kernel(x, mem, norm_mix_g, w_in, lambda_q1, lambda_k1, lambda_q2, lambda_k2, subln_g, rpb, w_out,
           norm_mem_q_g, norm_mem_kv_g, w_mq, w_mkv, w_mo, norm_ffn_g, w_router, b_router, w1, b1, w2, b2,
           norm_final_g):
    b, s, d = x.shape
    n = b * s
    depth = w_in.shape[0]
    assert d == SUBLANES * LANES and s % GRID_W == 0
    slopes = jnp.asarray([2.0 ** (-8.0 * (i + 1) / DIFF_HEADS) for i in range(DIFF_HEADS)], F32)
    x2d = x.reshape(n, d)
    for l in range(depth):
        lam_init = _lambda_init(l)
        lam = (jnp.exp(jnp.sum(lambda_q1[l] * lambda_k1[l])) - jnp.exp(jnp.sum(lambda_q2[l] * lambda_k2[l]))
               + lam_init).astype(F32)
        qkv = _qkv_proj(x2d, norm_mix_g[l], w_in[l].astype(BF16))
        qkv3 = qkv.reshape(b, s, qkv.shape[1])
        o_diff = _diff_attn(qkv3, lam, slopes, subln_g[l], lam_init).reshape(n, DIFF_WIDTH)
        o_na = _na_attn(qkv3, _na_bias_table(rpb[l], s // GRID_W)).reshape(n, NA_WIDTH)
        kv = _mem_kv(mem, norm_mem_kv_g[l], w_mkv[l].astype(BF16))
        wr_t = w_router[l].T
        wr_hi_t = wr_t.astype(BF16)
        wr_lo_t = (wr_t - wr_hi_t.astype(F32)).astype(BF16)
        x2, hf_tiles, ti, tg, tr, counts = _post_attn(
            x2d, o_diff, o_na, w_out[l].astype(BF16), norm_mem_q_g[l], w_mq[l].astype(BF16), kv,
            w_mo[l].astype(BF16), norm_ffn_g[l], wr_hi_t, wr_lo_t, b_router[l], s)
        src_tok, dst_slot, blk_e, nact = _routing_tables(counts[:, 0], ti[:TOP_K].T, tr[:TOP_K].T, n)
        y_slots = _expert_ffn(blk_e, nact, src_tok, dst_slot, hf_tiles, w1[l], b1[l], w2[l], b2[l], n * TOP_K)
        x2d = _combine(x2, tg.T, norm_final_g, y_slots, final_norm=(l == depth - 1))
    return x2d.reshape(b, s, d)
```

```python
import functools
import math

import jax
import jax.numpy as jnp
import numpy as np
from jax import lax
from jax.experimental import pallas as pl
from jax.experimental.pallas import tpu as pltpu

HEAD_DIM = 64
DIFF_HEADS = 4
DIFF_VDIM = 2 * HEAD_DIM
DIFF_QK_WIDTH = DIFF_HEADS * 2 * HEAD_DIM
DIFF_WIDTH = DIFF_HEADS * DIFF_VDIM
NA_HEADS = 8
NA_WIDTH = NA_HEADS * HEAD_DIM
GRID_W = 64
NA_ROWS = 8
NA_COLS = 16
MEM_HEADS = 4
MEM_HEAD_DIM = 64
MEM_WIDTH = MEM_HEADS * MEM_HEAD_DIM
N_EXPERTS = 32
TOP_K = 4
SWIGLU_LIMIT = 7.0
SWIGLU_ALPHA = 1.702
RMS_EPS = 1e-5

LANES = 128
SUBLANES = 8
NEG_BIG = -1e30

F32 = jnp.float32
BF16 = jnp.bfloat16

QKV_TM = 512
DIFF_TQ = 512
DIFF_TK = 512
NA_R = 8
POST_TM = 256
FFN_BLK = 256
COMB_TM = 256
WAITS_PER_TRIP = 16
INVERT_CHUNK = 8192
INVERT_UNROLL = 16
Y_SLOTS = 3
VMEM_LIMIT = 56 * 1024 * 1024


def _params(sem):
    return pltpu.CompilerParams(dimension_semantics=sem, vmem_limit_bytes=VMEM_LIMIT)


def _rms(x, g):
    return x * lax.rsqrt(jnp.mean(x * x, axis=-1, keepdims=True) + RMS_EPS) * g


def _dot(a, b):
    return jnp.dot(a, b, preferred_element_type=F32)


def _dot_nt(a, b):
    return lax.dot_general(a, b, (((1,), (1,)), ((), ())), preferred_element_type=F32)


def _rows_to_tiles(ref, val, rows):
    for j in range(SUBLANES):
        ref[pl.ds(j, rows, stride=SUBLANES), :] = val[:, j * LANES:(j + 1) * LANES]


def _tiles_to_rows(ref, rows, lead=None):
    if lead is None:
        parts = [ref[pl.ds(j, rows, stride=SUBLANES), :] for j in range(SUBLANES)]
    else:
        parts = [ref[lead, pl.ds(j, rows, stride=SUBLANES), :] for j in range(SUBLANES)]
    return jnp.concatenate(parts, axis=1)


def _qkv_kernel(x_ref, g_ref, w_ref, o_ref):
    h = _rms(x_ref[...], g_ref[...]).astype(BF16)
    cols = o_ref.shape[1]
    step = 512
    for c in range(cols // step):
        o_ref[:, c * step:(c + 1) * step] = _dot(h, w_ref[:, c * step:(c + 1) * step]).astype(o_ref.dtype)


def _qkv_proj(x2d, g, w_bf16):
    n, d = x2d.shape
    cols = w_bf16.shape[1]
    return pl.pallas_call(
        _qkv_kernel,
        grid=(n // QKV_TM,),
        in_specs=[pl.BlockSpec((QKV_TM, d), lambda i: (i, 0)),
                  pl.BlockSpec((1, d), lambda i: (0, 0)),
                  pl.BlockSpec((d, cols), lambda i: (0, 0))],
        out_specs=pl.BlockSpec((QKV_TM, cols), lambda i: (i, 0)),
        out_shape=jax.ShapeDtypeStruct((n, cols), BF16),
        compiler_params=_params(("parallel",)),
        name="qkv_proj",
    )(x2d, g.reshape(1, d), w_bf16)


ALIBI_AUG_LEFT, ALIBI_AUG_RIGHT, ALIBI_AUG_NONE = 0, 1, 2
ONES_ROWS = 16
EXP_UNDERFLOW = 105.0
NORM_BOUND_SLACK = 1.02


def _bf16_exact_split(n):
    lo_bits = max(0, int(n.max()).bit_length() - 8)
    lo = n % (1 << lo_bits)
    return n - lo, lo


def _alibi_aug_tables(tq, tk):
    ii_hi, ii_lo = _bf16_exact_split(np.arange(tq))
    jj_hi, jj_lo = _bf16_exact_split(np.arange(tk))
    qa = np.zeros((DIFF_HEADS, tq, LANES), np.float32)
    ka = np.zeros((DIFF_HEADS, 3, tk, LANES), np.float32)
    for h in range(DIFF_HEADS):
        sl = 2.0 ** (-8.0 * (h + 1) / DIFF_HEADS)
        qa[h, :, 0], qa[h, :, 1], qa[h, :, 2], qa[h, :, 3] = -sl * ii_hi, -sl * ii_lo, 1.0, 1.0
        qa[h, :, 4], qa[h, :, 5], qa[h, :, 6], qa[h, :, 7] = sl * ii_hi, sl * ii_lo, 1.0, 1.0
        ka[h, ALIBI_AUG_LEFT, :, 0], ka[h, ALIBI_AUG_LEFT, :, 1] = 1.0, 1.0
        ka[h, ALIBI_AUG_LEFT, :, 2], ka[h, ALIBI_AUG_LEFT, :, 3] = sl * jj_hi, sl * jj_lo
        ka[h, ALIBI_AUG_RIGHT, :, 4], ka[h, ALIBI_AUG_RIGHT, :, 5] = 1.0, 1.0
        ka[h, ALIBI_AUG_RIGHT, :, 6], ka[h, ALIBI_AUG_RIGHT, :, 7] = -sl * jj_hi, -sl * jj_lo
    return jnp.asarray(qa, BF16), jnp.asarray(ka, BF16)


def _diff_kernel(lam_ref, slope_ref, q_ref, k_ref, vt_ref, qaug_ref, kaug_ref, g_ref, o_ref,
                 qa_sc, s0_sc, s1_sc, mb_sc, m_sc, acc_sc, ksq_sc, *, tq, tk, seq, out_scale):
    h = pl.program_id(1)
    i = pl.program_id(2)
    slope = slope_ref[h]
    lam = lam_ref[0]
    nk = seq // tk
    i0 = i * tq
    jd = i0 // tk

    sel_r = lax.broadcasted_iota(jnp.int32, (LANES, LANES), 0) // HEAD_DIM
    sel_c = lax.broadcasted_iota(jnp.int32, (LANES, LANES), 1)
    sel = (sel_r == sel_c).astype(BF16)

    def max_sq_norm(x):
        xf = x.astype(F32)
        return jnp.max(_dot((xf * xf).astype(BF16), sel), axis=0, keepdims=True)

    @pl.when(i == 0)
    def _():
        ksq_sc[...] = max_sq_norm(k_ref[0])

    q = q_ref[0] * jnp.asarray(1.0 / math.sqrt(HEAD_DIM), BF16)
    lane = lax.broadcasted_iota(jnp.int32, (tq, LANES), 1)
    zero = jnp.zeros_like(q)
    qa_sc[:tq, :LANES] = jnp.where(lane < HEAD_DIM, q, zero)
    qa_sc[tq:, :LANES] = jnp.where(lane >= HEAD_DIM, q, zero)
    qa_sc[:tq, LANES:] = qaug_ref[0]
    qa_sc[tq:, LANES:] = qaug_ref[0]
    m_sc[...] = jnp.full(m_sc.shape, NEG_BIG, F32)
    acc_sc[...] = jnp.zeros(acc_sc.shape, F32)

    def scores(j, side):
        start = pl.multiple_of(j * tk, tk)
        kba = jnp.concatenate([k_ref[0, pl.ds(start, tk), :], kaug_ref[0, side]], axis=1)
        return _dot_nt(kba, qa_sc[...])

    s_bufs = (s0_sc, s1_sc)

    def stash(slot, st):
        s_bufs[slot][...] = st
        mb_sc[slot] = jnp.max(st, axis=0, keepdims=True)

    def consume(slot, j, c):
        m_prev = m_sc[...]
        m_new = jnp.maximum(m_prev, mb_sc[slot] + c)
        alpha = jnp.exp(m_prev - m_new)
        p_t = jnp.exp(s_bufs[slot][...] - (m_new - c)).astype(BF16)
        start = pl.multiple_of(j * tk, tk)
        vta = jnp.concatenate([vt_ref[0, :, pl.ds(start, tk)], jnp.ones((ONES_ROWS, tk), BF16)], axis=0)
        acc_sc[...] = alpha * acc_sc[...] + _dot(vta, p_t)
        m_sc[...] = m_new

    def block_const(j):
        return -slope * jnp.abs(i0 - j * tk).astype(F32)

    rel_t = (lax.broadcasted_iota(jnp.int32, (tk, tq), 0) - lax.broadcasted_iota(jnp.int32, (tk, tq), 1)).astype(F32)
    bias_t = -slope * jnp.abs(rel_t + (jd * tk - i0).astype(F32))
    stash(0, scores(jd, ALIBI_AUG_NONE) + jnp.concatenate([bias_t, bias_t], axis=1))

    qk_bound = jnp.sqrt(jnp.max(max_sq_norm(q) * ksq_sc[...])) * NORM_BOUND_SLACK
    reach = (qk_bound - jnp.min(mb_sc[0]) + EXP_UNDERFLOW) / slope
    reach = jnp.minimum(reach, 2.0 * seq)
    j_lo = jnp.clip(jnp.floor((i0 - tk + 1 - reach) / tk).astype(jnp.int32) + 1, 0, jd)
    j_hi = jnp.clip(jnp.ceil((reach + i0 + tq - 1) / tk).astype(jnp.int32) - 1, jd, nk - 1)
    n_blocks = j_hi - j_lo + 1

    def block_of(t):
        return j_lo + jnp.where(t - 1 < jd - j_lo, t - 1, t)

    def produce(t, slot):
        j = block_of(t)
        stash(slot, scores(j, jnp.where(j < jd, ALIBI_AUG_LEFT, ALIBI_AUG_RIGHT)))

    def consume_step(t, slot):
        j = jnp.where(t == 0, jd, block_of(t))
        consume(slot, j, jnp.where(t == 0, 0.0, block_const(j)))

    def body(u, carry):
        t = 2 * u
        produce(t + 1, 1)
        consume_step(t, 0)
        produce(t + 2, 0)
        consume_step(t + 1, 1)
        return carry

    lax.fori_loop(0, (n_blocks - 1) // 2, body, 0)

    @pl.when(n_blocks % 2 == 0)
    def _():
        produce(n_blocks - 1, 1)
        consume_step(n_blocks - 2, 0)
        consume_step(n_blocks - 1, 1)

    @pl.when(n_blocks % 2 == 1)
    def _():
        consume_step(n_blocks - 1, 0)

    acc = acc_sc[...]
    o = acc[:DIFF_VDIM] * (1.0 / acc[DIFF_VDIM:DIFF_VDIM + 1])
    d = o[:, :tq] - lam * o[:, tq:]
    y = d * lax.rsqrt(jnp.mean(d * d, axis=0, keepdims=True) + RMS_EPS) * g_ref[...] * out_scale
    o_ref[0] = y.T.astype(o_ref.dtype)


def _diff_attn(qkv3, lam, slopes, subln_g, lam_init):
    b, s, _ = qkv3.shape
    tq, tk = DIFF_TQ, DIFF_TK
    assert tk % tq == 0 and s % (2 * tk) == 0
    kcol = DIFF_QK_WIDTH // LANES
    v_t = qkv3[:, :, 2 * DIFF_QK_WIDTH:2 * DIFF_QK_WIDTH + DIFF_WIDTH].transpose(0, 2, 1)
    qaug, kaug = _alibi_aug_tables(tq, tk)
    kern = functools.partial(_diff_kernel, tq=tq, tk=tk, seq=s, out_scale=1.0 - lam_init)
    return pl.pallas_call(
        kern,
        grid=(b, DIFF_HEADS, s // tq),
        in_specs=[pl.BlockSpec(memory_space=pltpu.SMEM),
                  pl.BlockSpec(memory_space=pltpu.SMEM),
                  pl.BlockSpec((1, tq, LANES), lambda bi, h, i: (bi, i, h)),
                  pl.BlockSpec((1, s, LANES), lambda bi, h, i: (bi, 0, kcol + h)),
                  pl.BlockSpec((1, DIFF_VDIM, s), lambda bi, h, i: (bi, h, 0)),
                  pl.BlockSpec((1, tq, LANES), lambda bi, h, i: (h, 0, 0)),
                  pl.BlockSpec((1, 3, tk, LANES), lambda bi, h, i: (h, 0, 0, 0)),
                  pl.BlockSpec((DIFF_VDIM, 1), lambda bi, h, i: (0, 0))],
        out_specs=pl.BlockSpec((1, tq, LANES), lambda bi, h, i: (bi, i, h)),
        out_shape=jax.ShapeDtypeStruct((b, s, DIFF_WIDTH), BF16),
        scratch_shapes=[pltpu.VMEM((2 * tq, 2 * LANES), BF16),
                        pltpu.VMEM((tk, 2 * tq), F32),
                        pltpu.VMEM((tk, 2 * tq), F32),
                        pltpu.VMEM((2, 1, 2 * tq), F32),
                        pltpu.VMEM((1, 2 * tq), F32),
                        pltpu.VMEM((DIFF_VDIM + ONES_ROWS, 2 * tq), F32),
                        pltpu.VMEM((1, LANES), F32)],
        compiler_params=_params(("parallel", "parallel", "arbitrary")),
        name="diff_attn",
    )(lam.reshape(1), slopes, qkv3, qkv3, v_t, qaug, kaug, subln_g.reshape(DIFF_VDIM, 1))


def _na_bias_table(rpb, rows):
    kr = min(NA_ROWS, rows)
    q = np.arange(GRID_W)[:, None]
    j = np.arange(GRID_W)[None, :]
    cs = np.clip(q - NA_COLS // 2, 0, GRID_W - NA_COLS)
    valid = (j >= cs) & (j < cs + NA_COLS)
    dc = j - q + (NA_COLS - 1)
    sel = ((dc[:, :, None] == np.arange(2 * NA_COLS - 1)[None, None, :]) & valid[:, :, None]).astype(np.float32)
    col = jnp.einsum('hrd,qjd->hrqj', rpb.astype(F32), jnp.asarray(sel), precision=lax.Precision.HIGHEST)
    col = jnp.where(valid[None, None], col, NEG_BIG)
    cases = [col[:, NA_ROWS - 1 - c:NA_ROWS - 1 - c + kr] for c in range(kr)]
    tbl = jnp.stack(cases, axis=0).transpose(0, 1, 3, 2, 4)
    return tbl.reshape(kr, NA_HEADS // 2, 2 * GRID_W, kr * GRID_W)


def _na_kernel(q_ref, k_ref, v_ref, bias_ref, o_ref, *, rows_per_step, rows, kr):
    i = pl.program_id(2)
    lane = lax.broadcasted_iota(jnp.int32, (GRID_W, LANES), 1)
    win = kr * GRID_W

    starts, scores, probs = [], [], []
    for rr in range(rows_per_step):
        r = i * rows_per_step + rr
        rs = jnp.clip(r - kr // 2, 0, rows - kr)
        q = q_ref[0, rr * GRID_W:(rr + 1) * GRID_W, :]
        q = q * jnp.asarray(1.0 / math.sqrt(HEAD_DIM), BF16)
        zero = jnp.zeros_like(q)
        qs = jnp.concatenate([jnp.where(lane < HEAD_DIM, q, zero), jnp.where(lane >= HEAD_DIM, q, zero)], axis=0)
        start = pl.multiple_of(rs * GRID_W, GRID_W)
        starts.append(start)
        scores.append(_dot_nt(qs, k_ref[0, pl.ds(start, win), :]) + bias_ref[r - rs, 0])
    for s in scores:
        p = jnp.exp(s - jnp.max(s, axis=1, keepdims=True))
        probs.append((p.astype(BF16), jnp.sum(p, axis=1, keepdims=True)))
    for rr, (p, l) in enumerate(probs):
        o = _dot(p, v_ref[0, pl.ds(starts[rr], win), :]) / l
        out = jnp.where(lane < HEAD_DIM, o[:GRID_W], o[GRID_W:])
        o_ref[0, rr * GRID_W:(rr + 1) * GRID_W, :] = out.astype(o_ref.dtype)


def _na_attn(qkv3, bias_tbl):
    b, s, _ = qkv3.shape
    rows = s // GRID_W
    kr = min(NA_ROWS, rows)
    base = (2 * DIFF_QK_WIDTH + DIFF_WIDTH) // LANES
    pairs = NA_HEADS // 2
    rstep = NA_R
    kern = functools.partial(_na_kernel, rows_per_step=rstep, rows=rows, kr=kr)
    return pl.pallas_call(
        kern,
        grid=(b, pairs, rows // rstep),
        in_specs=[pl.BlockSpec((1, rstep * GRID_W, LANES), lambda bi, p, i: (bi, i, base + p)),
                  pl.BlockSpec((1, s, LANES), lambda bi, p, i: (bi, 0, base + pairs + p)),
                  pl.BlockSpec((1, s, LANES), lambda bi, p, i: (bi, 0, base + 2 * pairs + p)),
                  pl.BlockSpec((kr, 1, 2 * GRID_W, kr * GRID_W), lambda bi, p, i: (0, p, 0, 0))],
        out_specs=pl.BlockSpec((1, rstep * GRID_W, LANES), lambda bi, p, i: (bi, i, p)),
        out_shape=jax.ShapeDtypeStruct((b, s, NA_WIDTH), BF16),
        compiler_params=_params(("parallel", "parallel", "parallel")),
        name="na_attn",
    )(qkv3, qkv3, qkv3, bias_tbl)


def _memkv_kernel(mem_ref, g_ref, w_ref, o_ref):
    h = _rms(mem_ref[0], g_ref[...]).astype(BF16)
    o_ref[0] = _dot(h, w_ref[...]).astype(o_ref.dtype)


def _mem_kv(mem, g, w_bf16):
    b, m, d = mem.shape
    cols = w_bf16.shape[1]
    return pl.pallas_call(
        _memkv_kernel,
        grid=(b,),
        in_specs=[pl.BlockSpec((1, m, d), lambda i: (i, 0, 0)),
                  pl.BlockSpec((1, d), lambda i: (0, 0)),
                  pl.BlockSpec((d, cols), lambda i: (0, 0))],
        out_specs=pl.BlockSpec((1, m, cols), lambda i: (i, 0, 0)),
        out_shape=jax.ShapeDtypeStruct((b, m, cols), BF16),
        compiler_params=_params(("parallel",)),
        name="mem_kv",
    )(mem, g.reshape(1, d), w_bf16)


def _sublane_pack(rows, tm):
    sub = lax.broadcasted_iota(jnp.int32, (SUBLANES, tm), 0)
    out = jnp.zeros((SUBLANES, tm), rows[0].dtype)
    for k, r in enumerate(rows):
        out = jnp.where(sub == k, r, out)
    return out


def _post_kernel(x_ref, od_ref, on_ref, wout_ref, gq_ref, wmq_ref, kv_ref, wmo_ref, gf_ref,
                 wrh_ref, wrl_ref, br_ref,
                 x2_ref, hf_ref, ti_ref, tg_ref, tr_ref, cnt_ref, carry_sc, *, tm):
    step = pl.program_id(0)

    @pl.when(step == 0)
    def _():
        carry_sc[...] = jnp.zeros(carry_sc.shape, F32)

    x1 = x_ref[...] + _dot(od_ref[...], wout_ref[:DIFF_WIDTH, :]) + _dot(on_ref[...], wout_ref[DIFF_WIDTH:, :])

    hm = _rms(x1, gq_ref[...]).astype(BF16)
    q = (_dot(hm, wmq_ref[...]) * (1.0 / math.sqrt(MEM_HEAD_DIM))).astype(BF16)
    head_of_lane = lax.broadcasted_iota(jnp.int32, (tm, MEM_WIDTH), 1) // MEM_HEAD_DIM
    zero = jnp.zeros_like(q)
    qs = jnp.concatenate([jnp.where(head_of_lane == h, q, zero) for h in range(MEM_HEADS)], axis=0)
    kv = kv_ref[0]
    s = _dot_nt(qs, kv[:, :MEM_WIDTH])
    m = jnp.max(s, axis=1, keepdims=True)
    p = jnp.exp(s - m)
    l = jnp.sum(p, axis=1, keepdims=True)
    o4 = _dot(p.astype(BF16), kv[:, MEM_WIDTH:]) / l
    o = jnp.zeros((tm, MEM_WIDTH), F32)
    for h in range(MEM_HEADS):
        o = jnp.where(head_of_lane == h, o4[h * tm:(h + 1) * tm], o)
    x2 = x1 + _dot(o.astype(BF16), wmo_ref[...])
    x2_ref[...] = x2

    hf = _rms(x2, gf_ref[...])
    _rows_to_tiles(hf_ref, hf, tm)
    hi = hf.astype(BF16)
    lo = (hf - hi.astype(F32)).astype(BF16)
    logits = _dot_nt(wrh_ref[...], hi) + _dot_nt(wrl_ref[...], hi) + _dot_nt(wrh_ref[...], lo) + br_ref[...]

    e_iota = lax.broadcasted_iota(jnp.int32, (N_EXPERTS, tm), 0)
    vals, idxs, hots = [], [], []
    cur = logits
    for _ in range(TOP_K):
        mx = jnp.max(cur, axis=0, keepdims=True)
        idx = jnp.min(jnp.where(cur == mx, e_iota, N_EXPERTS), axis=0, keepdims=True)
        hot = e_iota == idx
        vals.append(mx)
        idxs.append(idx)
        hots.append(hot)
        cur = jnp.where(hot, -jnp.inf, cur)
    exps = [jnp.exp(v - vals[0]) for v in vals]
    den = exps[0] + exps[1] + exps[2] + exps[3]
    gates = [e / den for e in exps]

    chosen = jnp.zeros((N_EXPERTS, tm), F32)
    for hot in hots:
        chosen = chosen + hot.astype(F32)
    r_i = lax.broadcasted_iota(jnp.int32, (tm, tm), 0)
    c_i = lax.broadcasted_iota(jnp.int32, (tm, tm), 1)
    earlier = (r_i < c_i).astype(BF16)
    before = _dot(chosen.astype(BF16), earlier) + carry_sc[...]
    ranks = [jnp.sum(jnp.where(hot, before, 0.0), axis=0, keepdims=True).astype(jnp.int32) for hot in hots]
    carry_sc[...] = carry_sc[...] + jnp.sum(chosen, axis=1, keepdims=True)
    cnt_ref[...] = carry_sc[...]

    ti_ref[...] = _sublane_pack(idxs, tm)
    tg_ref[...] = _sublane_pack(gates, tm)
    tr_ref[...] = _sublane_pack(ranks, tm)


def _post_attn(x2d, o_diff, o_na, w_out, gq, w_mq, kv, w_mo, gf, wr_hi_t, wr_lo_t, b_router, seq):
    n, d = x2d.shape
    tm = POST_TM
    m_tok = kv.shape[1]
    const = lambda i: (0, 0)
    row = lambda i: (i, 0)
    col = lambda i: (0, i)
    kern = functools.partial(_post_kernel, tm=tm)
    return pl.pallas_call(
        kern,
        grid=(n // tm,),
        in_specs=[pl.BlockSpec((tm, d), row),
                  pl.BlockSpec((tm, DIFF_WIDTH), row),
                  pl.BlockSpec((tm, NA_WIDTH), row),
                  pl.BlockSpec(w_out.shape, const),
                  pl.BlockSpec((1, d), const),
                  pl.BlockSpec(w_mq.shape, const),
                  pl.BlockSpec((1, m_tok, 2 * MEM_WIDTH), lambda i: ((i * tm) // seq, 0, 0)),
                  pl.BlockSpec(w_mo.shape, const),
                  pl.BlockSpec((1, d), const),
                  pl.BlockSpec(wr_hi_t.shape, const),
                  pl.BlockSpec(wr_lo_t.shape, const),
                  pl.BlockSpec((N_EXPERTS, 1), const)],
        out_specs=[pl.BlockSpec((tm, d), row),
                   pl.BlockSpec((tm * SUBLANES, LANES), row),
                   pl.BlockSpec((SUBLANES, tm), col),
                   pl.BlockSpec((SUBLANES, tm), col),
                   pl.BlockSpec((SUBLANES, tm), col),
                   pl.BlockSpec((N_EXPERTS, 1), const)],
        out_shape=[jax.ShapeDtypeStruct((n, d), F32),
                   jax.ShapeDtypeStruct((n * SUBLANES, LANES), F32),
                   jax.ShapeDtypeStruct((SUBLANES, n), jnp.int32),
                   jax.ShapeDtypeStruct((SUBLANES, n), F32),
                   jax.ShapeDtypeStruct((SUBLANES, n), jnp.int32),
                   jax.ShapeDtypeStruct((N_EXPERTS, 1), F32)],
        scratch_shapes=[pltpu.VMEM((N_EXPERTS, 1), F32)],
        compiler_params=_params(("arbitrary",)),
        name="post_attn",
    )(x2d, o_diff, o_na, w_out, gq.reshape(1, d), w_mq, kv, w_mo, gf.reshape(1, d), wr_hi_t, wr_lo_t,
      b_router.reshape(N_EXPERTS, 1))


def _row_copy(src, src_row, dst, dst_row, sem):
    return pltpu.make_async_copy(
        src.at[pl.ds(pl.multiple_of(src_row * SUBLANES, SUBLANES), SUBLANES), :],
        dst.at[pl.ds(pl.multiple_of(dst_row * SUBLANES, SUBLANES), SUBLANES), :],
        sem)


def _ffn_kernel(blk_e_ref, nact_ref, src_cur_ref, src_next_ref, dst_prev_ref, hf_ref,
                w1_ref, b1_ref, w2_ref, b2_ref, yc_ref,
                xbuf, ybuf, w1b_sc, w2b_sc, gsem, ssem, *, blk, d_ff, trash_row):
    i = pl.program_id(0)
    nact = nact_ref[0]
    slot = i % 2
    y_slot = i % Y_SLOTS
    y_prev = (i + Y_SLOTS - 1) % Y_SLOTS

    def gather(src_ref, t, to_slot):
        return _row_copy(hf_ref, src_ref[t], xbuf.at[to_slot], t, gsem.at[to_slot])

    def scatter(t):
        dst = jnp.where(i == 0, trash_row + t, dst_prev_ref[t])
        return _row_copy(ybuf.at[y_prev], t, yc_ref, dst, ssem)

    def wait_all(copy_of_row):
        def drain(g, carry):
            for u in range(WAITS_PER_TRIP):
                copy_of_row(g * WAITS_PER_TRIP + u).wait()
            return carry
        lax.fori_loop(0, blk // WAITS_PER_TRIP, drain, 0)

    @pl.when(i == 0)
    def _():
        ybuf[...] = jnp.zeros(ybuf.shape, F32)
        for t in range(blk):
            gather(src_cur_ref, t, 0).start(priority=t % 2)

    @pl.when((i < nact) & ((i == 0) | (blk_e_ref[i] != blk_e_ref[jnp.maximum(i - 1, 0)])))
    def _():
        rows = 256
        for c in range(w1b_sc.shape[0] // rows):
            w1b_sc[c * rows:(c + 1) * rows, :] = w1_ref[0, c * rows:(c + 1) * rows, :].astype(BF16)
        for c in range(w2b_sc.shape[0] // rows):
            w2b_sc[c * rows:(c + 1) * rows, :] = w2_ref[0, c * rows:(c + 1) * rows, :].astype(BF16)

    @pl.when(i <= nact)
    def _():
        wait_all(lambda t: gather(src_cur_ref, t, slot))

    @pl.when((i >= 1) & (i <= nact))
    def _():
        wait_all(scatter)

    @pl.when(i < nact)
    def _():
        for t in range(blk):
            gather(src_next_ref, t, 1 - slot).start(priority=t % 2)
            scatter(t).start(priority=(t + 1) % 2)
        x = _tiles_to_rows(xbuf, blk, lead=slot).astype(BF16)
        gu = _dot(x, w1b_sc[...]) + b1_ref[0]
        gate = jnp.minimum(gu[:, :d_ff], SWIGLU_LIMIT)
        lin = jnp.clip(gu[:, d_ff:], -SWIGLU_LIMIT, SWIGLU_LIMIT)
        act = gate * jax.nn.sigmoid(SWIGLU_ALPHA * gate) * (lin + 1.0)
        y = _dot(act.astype(BF16), w2b_sc[...]) + b2_ref[0]
        _rows_to_tiles(ybuf.at[y_slot], y, blk)

    @pl.when(i == nact)
    def _():
        def issue(t, carry):
            scatter(t).start()
            return carry
        lax.fori_loop(0, blk, issue, 0)
        wait_all(scatter)


def _expert_ffn(blk_e, nact, src_tok, dst_slot, hf_tiles, w1, b1, w2, b2, n_slots):
    blk = FFN_BLK
    nblk = src_tok.shape[0] // blk
    _, d, two_ff = w1.shape
    d_ff = two_ff // 2
    kern = functools.partial(_ffn_kernel, blk=blk, d_ff=d_ff, trash_row=n_slots)
    smem_blk = lambda f: pl.BlockSpec((blk,), f, memory_space=pltpu.SMEM)
    grid_spec = pltpu.PrefetchScalarGridSpec(
        num_scalar_prefetch=2,
        grid=(nblk,),
        in_specs=[smem_blk(lambda i, be, na: (i,)),
                  smem_blk(lambda i, be, na: (jnp.minimum(i + 1, nblk - 1),)),
                  smem_blk(lambda i, be, na: (jnp.maximum(i - 1, 0),)),
                  pl.BlockSpec(memory_space=pl.ANY),
                  pl.BlockSpec((1, d, two_ff), lambda i, be, na: (be[i], 0, 0)),
                  pl.BlockSpec((1, 1, two_ff), lambda i, be, na: (be[i], 0, 0)),
                  pl.BlockSpec((1, d_ff, d), lambda i, be, na: (be[i], 0, 0)),
                  pl.BlockSpec((1, 1, d), lambda i, be, na: (be[i], 0, 0))],
        out_specs=pl.BlockSpec(memory_space=pl.ANY),
        scratch_shapes=[pltpu.VMEM((2, blk * SUBLANES, LANES), F32),
                        pltpu.VMEM((Y_SLOTS, blk * SUBLANES, LANES), F32),
                        pltpu.VMEM((d, two_ff), BF16),
                        pltpu.VMEM((d_ff, d), BF16),
                        pltpu.SemaphoreType.DMA((2,)),
                        pltpu.SemaphoreType.DMA(())],
    )
    return pl.pallas_call(
        kern,
        grid_spec=grid_spec,
        out_shape=jax.ShapeDtypeStruct(((n_slots + blk) * SUBLANES, LANES), F32),
        compiler_params=_params(("arbitrary",)),
        name="expert_ffn",
    )(blk_e, nact, src_tok, src_tok, dst_slot, hf_tiles, w1, b1.reshape(N_EXPERTS, 1, two_ff), w2,
      b2.reshape(N_EXPERTS, 1, d))


def _combine_kernel(x2_ref, tg_ref, gfin_ref, yc_ref, o_ref, *, tm, final_norm):
    gates = tg_ref[...]
    pieces = []
    for j in range(SUBLANES):
        acc = x2_ref[:, j * LANES:(j + 1) * LANES]
        for k in range(TOP_K):
            acc = acc + gates[:, k:k + 1] * yc_ref[pl.ds(k * SUBLANES + j, tm, stride=TOP_K * SUBLANES), :]
        pieces.append(acc)
    out = jnp.concatenate(pieces, axis=1)
    if final_norm:
        out = _rms(out, gfin_ref[...])
    o_ref[...] = out


def _combine(x2, gates, g_final, y_slots, final_norm):
    n, d = x2.shape
    tm = COMB_TM
    kern = functools.partial(_combine_kernel, tm=tm, final_norm=final_norm)
    return pl.pallas_call(
        kern,
        grid=(n // tm,),
        in_specs=[pl.BlockSpec((tm, d), lambda i: (i, 0)),
                  pl.BlockSpec((tm, SUBLANES), lambda i: (i, 0)),
                  pl.BlockSpec((1, d), lambda i: (0, 0)),
                  pl.BlockSpec((tm * TOP_K * SUBLANES, LANES), lambda i: (i, 0))],
        out_specs=pl.BlockSpec((tm, d), lambda i: (i, 0)),
        out_shape=jax.ShapeDtypeStruct((n, d), F32),
        compiler_params=_params(("parallel",)),
        name="combine",
    )(x2, gates, g_final.reshape(1, d), y_slots)


def _invert_kernel(pad_lo_ref, pad_hi_ref, dest_ref, src_ref, dst_ref, *, chunk, n_assign, blk):
    c = pl.program_id(0)

    @pl.when(c == 0)
    def _():
        def expert(e, carry):
            def pad_row(r, carry2):
                off = r & (blk - 1)
                src_ref[r] = off
                dst_ref[r] = n_assign + off
                return carry2
            lax.fori_loop(pad_lo_ref[e], pad_hi_ref[e], pad_row, 0)
            return carry
        lax.fori_loop(0, N_EXPERTS, expert, 0)

    def body(g, carry):
        a0 = g * INVERT_UNROLL
        slot0 = c * chunk + a0
        tok0 = lax.shift_right_logical(slot0, TOP_K.bit_length() - 1)
        for u in range(INVERT_UNROLL):
            row = dest_ref[a0 + u]
            src_ref[row] = tok0 + u // TOP_K
            dst_ref[row] = slot0 + u
        return carry
    lax.fori_loop(0, chunk // INVERT_UNROLL, body, 0)


def _invert_routing(dest, pad_lo, pad_hi, p_rows):
    n_assign = dest.shape[0]
    chunk = INVERT_CHUNK
    blk = FFN_BLK
    assert n_assign % chunk == 0 and chunk % INVERT_UNROLL == 0 and INVERT_UNROLL % TOP_K == 0
    assert TOP_K & (TOP_K - 1) == 0 and blk & (blk - 1) == 0
    kern = functools.partial(_invert_kernel, chunk=chunk, n_assign=n_assign, blk=blk)
    whole_smem = pl.BlockSpec(memory_space=pltpu.SMEM)
    grid_spec = pltpu.PrefetchScalarGridSpec(
        num_scalar_prefetch=2,
        grid=(n_assign // chunk,),
        in_specs=[pl.BlockSpec((chunk,), lambda c, lo, hi: (c,), memory_space=pltpu.SMEM)],
        out_specs=[whole_smem, whole_smem],
    )
    return pl.pallas_call(
        kern,
        grid_spec=grid_spec,
        out_shape=[jax.ShapeDtypeStruct((p_rows,), jnp.int32), jax.ShapeDtypeStruct((p_rows,), jnp.int32)],
        compiler_params=_params(("arbitrary",)),
        name="invert_routing",
    )(pad_lo, pad_hi, dest)


def _routing_tables(counts_f32, top_i, rank, n_tok):
    blk = FFN_BLK
    n_assign = n_tok * TOP_K
    counts = counts_f32.astype(jnp.int32)
    pcounts = ((counts + blk - 1) // blk) * blk
    pends = jnp.cumsum(pcounts)
    pstarts = pends - pcounts
    dest = (pstarts[top_i] + rank).reshape(-1)
    p_rows = n_assign + N_EXPERTS * blk
    nblk = p_rows // blk
    pad_lo = (pstarts + counts).astype(jnp.int32)
    pad_hi = pends.at[N_EXPERTS - 1].set(p_rows).astype(jnp.int32)
    src_tok, dst_slot = _invert_routing(dest.astype(jnp.int32), pad_lo, pad_hi, p_rows)
    blk_start = jnp.arange(nblk, dtype=jnp.int32) * blk
    blk_e = jnp.minimum(jnp.sum(blk_start[:, None] >= pends[None, :], axis=1), N_EXPERTS - 1).astype(jnp.int32)
    nact = (pends[-1] // blk).astype(jnp.int32).reshape(1)
    return src_tok, dst_slot, blk_e, nact


def _lambda_init(layer):
    return 0.8 - 0.6 * math.exp(-0.3 * layer)


def kernel(x, mem, norm_mix_g, w_in, lambda_q1, lambda_k1, lambda_q2, lambda_k2, subln_g, rpb, w_out,
           norm_mem_q_g, norm_mem_kv_g, w_mq, w_mkv, w_mo, norm_ffn_g, w_router, b_router, w1, b1, w2, b2,
           norm_final_g):
    b, s, d = x.shape
    n = b * s
    depth = w_in.shape[0]
    assert d == SUBLANES * LANES and s % GRID_W == 0
    slopes = jnp.asarray([2.0 ** (-8.0 * (i + 1) / DIFF_HEADS) for i in range(DIFF_HEADS)], F32)
    x2d = x.reshape(n, d)
    for l in range(depth):
        lam_init = _lambda_init(l)
        lam = (jnp.exp(jnp.sum(lambda_q1[l] * lambda_k1[l])) - jnp.exp(jnp.sum(lambda_q2[l] * lambda_k2[l]))
               + lam_init).astype(F32)
        qkv = _qkv_proj(x2d, norm_mix_g[l], w_in[l].astype(BF16))
        qkv3 = qkv.reshape(b, s, qkv.shape[1])
        o_diff = _diff_attn(qkv3, lam, slopes, subln_g[l], lam_init).reshape(n, DIFF_WIDTH)
        o_na = _na_attn(qkv3, _na_bias_table(rpb[l], s // GRID_W)).reshape(n, NA_WIDTH)
        kv = _mem_kv(mem, norm_mem_kv_g[l], w_mkv[l].astype(BF16))
        wr_t = w_router[l].T
        wr_hi_t = wr_t.astype(BF16)
        wr_lo_t = (wr_t - wr_hi_t.astype(F32)).astype(BF16)
        x2, hf_tiles, ti, tg, tr, counts = _post_attn(
            x2d, o_diff, o_na, w_out[l].astype(BF16), norm_mem_q_g[l], w_mq[l].astype(BF16), kv,
            w_mo[l].astype(BF16), norm_ffn_g[l], wr_hi_t, wr_lo_t, b_router[l], s)
        src_tok, dst_slot, blk_e, nact = _routing_tables(counts[:, 0], ti[:TOP_K].T, tr[:TOP_K].T, n)
        y_slots = _expert_ffn(blk_e, nact, src_tok, dst_slot, hf_tiles, w1[l], b1[l], w2[l], b2[l], n * TOP_K)
        x2d = _combine(x2, tg.T, norm_final_g, y_slots, final_norm=(l == depth - 1))
    return x2d.reshape(b, s, d)
```

```python
import functools
import math

import jax
import jax.numpy as jnp
import numpy as np
from jax import lax
from jax.experimental import pallas as pl
from jax.experimental.pallas import tpu as pltpu

HEAD_DIM = 64
DIFF_HEADS = 4
DIFF_VDIM = 2 * HEAD_DIM
DIFF_QK_WIDTH = DIFF_HEADS * 2 * HEAD_DIM
DIFF_WIDTH = DIFF_HEADS * DIFF_VDIM
NA_HEADS = 8
NA_WIDTH = NA_HEADS * HEAD_DIM
GRID_W = 64
NA_ROWS = 8
NA_COLS = 16
MEM_HEADS = 4
MEM_HEAD_DIM = 64
MEM_WIDTH = MEM_HEADS * MEM_HEAD_DIM
N_EXPERTS = 32
TOP_K = 4
SWIGLU_LIMIT = 7.0
SWIGLU_ALPHA = 1.702
RMS_EPS = 1e-5

LANES = 128
SUBLANES = 8
NEG_BIG = -1e30

F32 = jnp.float32
BF16 = jnp.bfloat16

QKV_TM = 512
DIFF_TQ = 512
DIFF_TK = 512
NA_R = 8
POST_TM = 512
POST_PARTS = 2
FFN_BLK = 256
COMB_TM = 256
WAITS_PER_TRIP = 16
INVERT_CHUNK = 8192
INVERT_UNROLL = 16
Y_SLOTS = 3
VMEM_LIMIT = 56 * 1024 * 1024


def _params(sem):
    return pltpu.CompilerParams(dimension_semantics=sem, vmem_limit_bytes=VMEM_LIMIT)


def _rms(x, g):
    return x * lax.rsqrt(jnp.mean(x * x, axis=-1, keepdims=True) + RMS_EPS) * g


def _dot(a, b):
    return jnp.dot(a, b, preferred_element_type=F32)


def _dot_nt(a, b):
    return lax.dot_general(a, b, (((1,), (1,)), ((), ())), preferred_element_type=F32)


def _rows_to_tiles(ref, val, rows):
    for j in range(SUBLANES):
        ref[pl.ds(j, rows, stride=SUBLANES), :] = val[:, j * LANES:(j + 1) * LANES]


def _tiles_to_rows(ref, rows, lead=None):
    if lead is None:
        parts = [ref[pl.ds(j, rows, stride=SUBLANES), :] for j in range(SUBLANES)]
    else:
        parts = [ref[lead, pl.ds(j, rows, stride=SUBLANES), :] for j in range(SUBLANES)]
    return jnp.concatenate(parts, axis=1)


def _qkv_kernel(x_ref, g_ref, w_ref, o_ref):
    h = _rms(x_ref[...], g_ref[...]).astype(BF16)
    cols = o_ref.shape[1]
    step = 512
    for c in range(cols // step):
        o_ref[:, c * step:(c + 1) * step] = _dot(h, w_ref[:, c * step:(c + 1) * step]).astype(o_ref.dtype)


def _qkv_proj(x2d, g, w_bf16):
    n, d = x2d.shape
    cols = w_bf16.shape[1]
    return pl.pallas_call(
        _qkv_kernel,
        grid=(n // QKV_TM,),
        in_specs=[pl.BlockSpec((QKV_TM, d), lambda i: (i, 0)),
                  pl.BlockSpec((1, d), lambda i: (0, 0)),
                  pl.BlockSpec((d, cols), lambda i: (0, 0))],
        out_specs=pl.BlockSpec((QKV_TM, cols), lambda i: (i, 0)),
        out_shape=jax.ShapeDtypeStruct((n, cols), BF16),
        compiler_params=_params(("parallel",)),
        name="qkv_proj",
    )(x2d, g.reshape(1, d), w_bf16)


ALIBI_AUG_LEFT, ALIBI_AUG_RIGHT, ALIBI_AUG_NONE = 0, 1, 2
ONES_ROWS = 16
EXP_UNDERFLOW = 105.0
NORM_BOUND_SLACK = 1.02


def _bf16_exact_split(n):
    lo_bits = max(0, int(n.max()).bit_length() - 8)
    lo = n % (1 << lo_bits)
    return n - lo, lo


def _alibi_aug_tables(tq, tk):
    ii_hi, ii_lo = _bf16_exact_split(np.arange(tq))
    jj_hi, jj_lo = _bf16_exact_split(np.arange(tk))
    qa = np.zeros((DIFF_HEADS, tq, LANES), np.float32)
    ka = np.zeros((DIFF_HEADS, 3, tk, LANES), np.float32)
    for h in range(DIFF_HEADS):
        sl = 2.0 ** (-8.0 * (h + 1) / DIFF_HEADS)
        qa[h, :, 0], qa[h, :, 1], qa[h, :, 2], qa[h, :, 3] = -sl * ii_hi, -sl * ii_lo, 1.0, 1.0
        qa[h, :, 4], qa[h, :, 5], qa[h, :, 6], qa[h, :, 7] = sl * ii_hi, sl * ii_lo, 1.0, 1.0
        ka[h, ALIBI_AUG_LEFT, :, 0], ka[h, ALIBI_AUG_LEFT, :, 1] = 1.0, 1.0
        ka[h, ALIBI_AUG_LEFT, :, 2], ka[h, ALIBI_AUG_LEFT, :, 3] = sl * jj_hi, sl * jj_lo
        ka[h, ALIBI_AUG_RIGHT, :, 4], ka[h, ALIBI_AUG_RIGHT, :, 5] = 1.0, 1.0
        ka[h, ALIBI_AUG_RIGHT, :, 6], ka[h, ALIBI_AUG_RIGHT, :, 7] = -sl * jj_hi, -sl * jj_lo
    return jnp.asarray(qa, BF16), jnp.asarray(ka, BF16)


def _diff_kernel(lam_ref, slope_ref, q_ref, k_ref, vt_ref, qaug_ref, kaug_ref, g_ref, o_ref,
                 qa_sc, s0_sc, s1_sc, mb_sc, m_sc, acc_sc, ksq_sc, *, tq, tk, seq, out_scale):
    h = pl.program_id(1)
    i = pl.program_id(2)
    slope = slope_ref[h]
    lam = lam_ref[0]
    nk = seq // tk
    i0 = i * tq
    jd = i0 // tk

    sel_r = lax.broadcasted_iota(jnp.int32, (LANES, LANES), 0) // HEAD_DIM
    sel_c = lax.broadcasted_iota(jnp.int32, (LANES, LANES), 1)
    sel = (sel_r == sel_c).astype(BF16)

    def max_sq_norm(x):
        xf = x.astype(F32)
        return jnp.max(_dot((xf * xf).astype(BF16), sel), axis=0, keepdims=True)

    @pl.when(i == 0)
    def _():
        ksq_sc[...] = max_sq_norm(k_ref[0])

    q = q_ref[0] * jnp.asarray(1.0 / math.sqrt(HEAD_DIM), BF16)
    lane = lax.broadcasted_iota(jnp.int32, (tq, LANES), 1)
    zero = jnp.zeros_like(q)
    qa_sc[:tq, :LANES] = jnp.where(lane < HEAD_DIM, q, zero)
    qa_sc[tq:, :LANES] = jnp.where(lane >= HEAD_DIM, q, zero)
    qa_sc[:tq, LANES:] = qaug_ref[0]
    qa_sc[tq:, LANES:] = qaug_ref[0]
    m_sc[...] = jnp.full(m_sc.shape, NEG_BIG, F32)
    acc_sc[...] = jnp.zeros(acc_sc.shape, F32)

    def scores(j, side):
        start = pl.multiple_of(j * tk, tk)
        kba = jnp.concatenate([k_ref[0, pl.ds(start, tk), :], kaug_ref[0, side]], axis=1)
        return _dot_nt(kba, qa_sc[...])

    s_bufs = (s0_sc, s1_sc)

    def stash(slot, st):
        s_bufs[slot][...] = st
        mb_sc[slot] = jnp.max(st, axis=0, keepdims=True)

    def consume(slot, j, c):
        m_prev = m_sc[...]
        m_new = jnp.maximum(m_prev, mb_sc[slot] + c)
        alpha = jnp.exp(m_prev - m_new)
        p_t = jnp.exp(s_bufs[slot][...] - (m_new - c)).astype(BF16)
        start = pl.multiple_of(j * tk, tk)
        vta = jnp.concatenate([vt_ref[0, :, pl.ds(start, tk)], jnp.ones((ONES_ROWS, tk), BF16)], axis=0)
        acc_sc[...] = alpha * acc_sc[...] + _dot(vta, p_t)
        m_sc[...] = m_new

    def block_const(j):
        return -slope * jnp.abs(i0 - j * tk).astype(F32)

    rel_t = (lax.broadcasted_iota(jnp.int32, (tk, tq), 0) - lax.broadcasted_iota(jnp.int32, (tk, tq), 1)).astype(F32)
    bias_t = -slope * jnp.abs(rel_t + (jd * tk - i0).astype(F32))
    stash(0, scores(jd, ALIBI_AUG_NONE) + jnp.concatenate([bias_t, bias_t], axis=1))

    qk_bound = jnp.sqrt(jnp.max(max_sq_norm(q) * ksq_sc[...])) * NORM_BOUND_SLACK
    reach = (qk_bound - jnp.min(mb_sc[0]) + EXP_UNDERFLOW) / slope
    reach = jnp.minimum(reach, 2.0 * seq)
    j_lo = jnp.clip(jnp.floor((i0 - tk + 1 - reach) / tk).astype(jnp.int32) + 1, 0, jd)
    j_hi = jnp.clip(jnp.ceil((reach + i0 + tq - 1) / tk).astype(jnp.int32) - 1, jd, nk - 1)
    n_blocks = j_hi - j_lo + 1

    def block_of(t):
        return j_lo + jnp.where(t - 1 < jd - j_lo, t - 1, t)

    def produce(t, slot):
        j = block_of(t)
        stash(slot, scores(j, jnp.where(j < jd, ALIBI_AUG_LEFT, ALIBI_AUG_RIGHT)))

    def consume_step(t, slot):
        j = jnp.where(t == 0, jd, block_of(t))
        consume(slot, j, jnp.where(t == 0, 0.0, block_const(j)))

    def body(u, carry):
        t = 2 * u
        produce(t + 1, 1)
        consume_step(t, 0)
        produce(t + 2, 0)
        consume_step(t + 1, 1)
        return carry

    def body4(u, carry):
        body(2 * u, carry)
        body(2 * u + 1, carry)
        return carry

    quads = (n_blocks - 1) // 4
    lax.fori_loop(0, quads, body4, 0)
    lax.fori_loop(2 * quads, (n_blocks - 1) // 2, body, 0)

    @pl.when(n_blocks % 2 == 0)
    def _():
        produce(n_blocks - 1, 1)
        consume_step(n_blocks - 2, 0)
        consume_step(n_blocks - 1, 1)

    @pl.when(n_blocks % 2 == 1)
    def _():
        consume_step(n_blocks - 1, 0)

    acc = acc_sc[...]
    o = acc[:DIFF_VDIM] * (1.0 / acc[DIFF_VDIM:DIFF_VDIM + 1])
    d = o[:, :tq] - lam * o[:, tq:]
    y = d * lax.rsqrt(jnp.mean(d * d, axis=0, keepdims=True) + RMS_EPS) * g_ref[...] * out_scale
    o_ref[0] = y.T.astype(o_ref.dtype)


def _diff_attn(qkv3, lam, slopes, subln_g, lam_init):
    b, s, _ = qkv3.shape
    tq, tk = DIFF_TQ, DIFF_TK
    assert tk % tq == 0 and s % (2 * tk) == 0
    kcol = DIFF_QK_WIDTH // LANES
    v_t = qkv3[:, :, 2 * DIFF_QK_WIDTH:2 * DIFF_QK_WIDTH + DIFF_WIDTH].transpose(0, 2, 1)
    qaug, kaug = _alibi_aug_tables(tq, tk)
    kern = functools.partial(_diff_kernel, tq=tq, tk=tk, seq=s, out_scale=1.0 - lam_init)
    return pl.pallas_call(
        kern,
        grid=(b, DIFF_HEADS, s // tq),
        in_specs=[pl.BlockSpec(memory_space=pltpu.SMEM),
                  pl.BlockSpec(memory_space=pltpu.SMEM),
                  pl.BlockSpec((1, tq, LANES), lambda bi, h, i: (bi, i, h)),
                  pl.BlockSpec((1, s, LANES), lambda bi, h, i: (bi, 0, kcol + h)),
                  pl.BlockSpec((1, DIFF_VDIM, s), lambda bi, h, i: (bi, h, 0)),
                  pl.BlockSpec((1, tq, LANES), lambda bi, h, i: (h, 0, 0)),
                  pl.BlockSpec((1, 3, tk, LANES), lambda bi, h, i: (h, 0, 0, 0)),
                  pl.BlockSpec((DIFF_VDIM, 1), lambda bi, h, i: (0, 0))],
        out_specs=pl.BlockSpec((1, tq, LANES), lambda bi, h, i: (bi, i, h)),
        out_shape=jax.ShapeDtypeStruct((b, s, DIFF_WIDTH), BF16),
        scratch_shapes=[pltpu.VMEM((2 * tq, 2 * LANES), BF16),
                        pltpu.VMEM((tk, 2 * tq), F32),
                        pltpu.VMEM((tk, 2 * tq), F32),
                        pltpu.VMEM((2, 1, 2 * tq), F32),
                        pltpu.VMEM((1, 2 * tq), F32),
                        pltpu.VMEM((DIFF_VDIM + ONES_ROWS, 2 * tq), F32),
                        pltpu.VMEM((1, LANES), F32)],
        compiler_params=_params(("parallel", "parallel", "arbitrary")),
        name="diff_attn",
    )(lam.reshape(1), slopes, qkv3, qkv3, v_t, qaug, kaug, subln_g.reshape(DIFF_VDIM, 1))


def _na_bias_table(rpb, rows):
    kr = min(NA_ROWS, rows)
    q = np.arange(GRID_W)[:, None]
    j = np.arange(GRID_W)[None, :]
    cs = np.clip(q - NA_COLS // 2, 0, GRID_W - NA_COLS)
    valid = (j >= cs) & (j < cs + NA_COLS)
    dc = j - q + (NA_COLS - 1)
    sel = ((dc[:, :, None] == np.arange(2 * NA_COLS - 1)[None, None, :]) & valid[:, :, None]).astype(np.float32)
    col = jnp.einsum('hrd,qjd->hrqj', rpb.astype(F32), jnp.asarray(sel), precision=lax.Precision.HIGHEST)
    col = jnp.where(valid[None, None], col, NEG_BIG)
    cases = [col[:, NA_ROWS - 1 - c:NA_ROWS - 1 - c + kr] for c in range(kr)]
    tbl = jnp.stack(cases, axis=0).transpose(0, 1, 3, 2, 4)
    return tbl.reshape(kr, NA_HEADS // 2, 2 * GRID_W, kr * GRID_W)


def _na_kernel(q_ref, k_ref, v_ref, bias_ref, o_ref, *, rows_per_step, rows, kr):
    i = pl.program_id(2)
    lane = lax.broadcasted_iota(jnp.int32, (GRID_W, LANES), 1)
    win = kr * GRID_W

    starts, scores, probs = [], [], []
    for rr in range(rows_per_step):
        r = i * rows_per_step + rr
        rs = jnp.clip(r - kr // 2, 0, rows - kr)
        q = q_ref[0, rr * GRID_W:(rr + 1) * GRID_W, :]
        q = q * jnp.asarray(1.0 / math.sqrt(HEAD_DIM), BF16)
        zero = jnp.zeros_like(q)
        qs = jnp.concatenate([jnp.where(lane < HEAD_DIM, q, zero), jnp.where(lane >= HEAD_DIM, q, zero)], axis=0)
        start = pl.multiple_of(rs * GRID_W, GRID_W)
        starts.append(start)
        scores.append(_dot_nt(qs, k_ref[0, pl.ds(start, win), :]) + bias_ref[r - rs, 0])
    for s in scores:
        p = jnp.exp(s - jnp.max(s, axis=1, keepdims=True))
        probs.append((p.astype(BF16), jnp.sum(p, axis=1, keepdims=True)))
    for rr, (p, l) in enumerate(probs):
        o = _dot(p, v_ref[0, pl.ds(starts[rr], win), :]) / l
        out = jnp.where(lane < HEAD_DIM, o[:GRID_W], o[GRID_W:])
        o_ref[0, rr * GRID_W:(rr + 1) * GRID_W, :] = out.astype(o_ref.dtype)


def _na_attn(qkv3, bias_tbl):
    b, s, _ = qkv3.shape
    rows = s // GRID_W
    kr = min(NA_ROWS, rows)
    base = (2 * DIFF_QK_WIDTH + DIFF_WIDTH) // LANES
    pairs = NA_HEADS // 2
    rstep = NA_R
    kern = functools.partial(_na_kernel, rows_per_step=rstep, rows=rows, kr=kr)
    return pl.pallas_call(
        kern,
        grid=(b, pairs, rows // rstep),
        in_specs=[pl.BlockSpec((1, rstep * GRID_W, LANES), lambda bi, p, i: (bi, i, base + p)),
                  pl.BlockSpec((1, s, LANES), lambda bi, p, i: (bi, 0, base + pairs + p)),
                  pl.BlockSpec((1, s, LANES), lambda bi, p, i: (bi, 0, base + 2 * pairs + p)),
                  pl.BlockSpec((kr, 1, 2 * GRID_W, kr * GRID_W), lambda bi, p, i: (0, p, 0, 0))],
        out_specs=pl.BlockSpec((1, rstep * GRID_W, LANES), lambda bi, p, i: (bi, i, p)),
        out_shape=jax.ShapeDtypeStruct((b, s, NA_WIDTH), BF16),
        compiler_params=_params(("parallel", "parallel", "parallel")),
        name="na_attn",
    )(qkv3, qkv3, qkv3, bias_tbl)


def _memkv_kernel(mem_ref, g_ref, w_ref, o_ref):
    h = _rms(mem_ref[0], g_ref[...]).astype(BF16)
    o_ref[0] = _dot(h, w_ref[...]).astype(o_ref.dtype)


def _mem_kv(mem, g, w_bf16):
    b, m, d = mem.shape
    cols = w_bf16.shape[1]
    return pl.pallas_call(
        _memkv_kernel,
        grid=(b,),
        in_specs=[pl.BlockSpec((1, m, d), lambda i: (i, 0, 0)),
                  pl.BlockSpec((1, d), lambda i: (0, 0)),
                  pl.BlockSpec((d, cols), lambda i: (0, 0))],
        out_specs=pl.BlockSpec((1, m, cols), lambda i: (i, 0, 0)),
        out_shape=jax.ShapeDtypeStruct((b, m, cols), BF16),
        compiler_params=_params(("parallel",)),
        name="mem_kv",
    )(mem, g.reshape(1, d), w_bf16)


def _sublane_pack(rows, tm):
    sub = lax.broadcasted_iota(jnp.int32, (SUBLANES, tm), 0)
    out = jnp.zeros((SUBLANES, tm), rows[0].dtype)
    for k, r in enumerate(rows):
        out = jnp.where(sub == k, r, out)
    return out


def _post_kernel(x_ref, od_ref, on_ref, wout_ref, gq_ref, wmq_ref, kv_ref, wmo_ref, gf_ref,
                 wrh_ref, wrl_ref, br_ref,
                 x2_ref, hf_ref, ti_ref, tg_ref, tr_ref, cnt_ref, carry_sc, *, tm, parts):
    step = pl.program_id(0)
    rows = tm // parts

    @pl.when(step == 0)
    def _():
        carry_sc[...] = jnp.zeros(carry_sc.shape, F32)

    head_of_lane = lax.broadcasted_iota(jnp.int32, (rows, MEM_WIDTH), 1) // MEM_HEAD_DIM
    kv = kv_ref[0]

    x1s, qss = [], []
    for g in range(parts):
        rs = slice(g * rows, (g + 1) * rows)
        x1 = (x_ref[rs, :] + _dot(od_ref[rs, :], wout_ref[:DIFF_WIDTH, :])
              + _dot(on_ref[rs, :], wout_ref[DIFF_WIDTH:, :]))
        hm = _rms(x1, gq_ref[...]).astype(BF16)
        q = (_dot(hm, wmq_ref[...]) * (1.0 / math.sqrt(MEM_HEAD_DIM))).astype(BF16)
        zero = jnp.zeros_like(q)
        qss.append(jnp.concatenate([jnp.where(head_of_lane == h, q, zero) for h in range(MEM_HEADS)], axis=0))
        x1s.append(x1)

    probs = []
    for g in range(parts):
        s = _dot_nt(qss[g], kv[:, :MEM_WIDTH])
        p = jnp.exp(s - jnp.max(s, axis=1, keepdims=True))
        probs.append((p.astype(BF16), jnp.sum(p, axis=1, keepdims=True)))

    hfs = []
    for g in range(parts):
        rs = slice(g * rows, (g + 1) * rows)
        p, l = probs[g]
        o4 = _dot(p, kv[:, MEM_WIDTH:]) / l
        o = jnp.zeros((rows, MEM_WIDTH), F32)
        for h in range(MEM_HEADS):
            o = jnp.where(head_of_lane == h, o4[h * rows:(h + 1) * rows], o)
        x2 = x1s[g] + _dot(o.astype(BF16), wmo_ref[...])
        x2_ref[rs, :] = x2
        hf = _rms(x2, gf_ref[...])
        _rows_to_tiles(hf_ref.at[pl.ds(g * rows * SUBLANES, rows * SUBLANES), :], hf, rows)
        hfs.append(hf)

    e_iota = lax.broadcasted_iota(jnp.int32, (N_EXPERTS, rows), 0)
    r_i = lax.broadcasted_iota(jnp.int32, (rows, rows), 0)
    c_i = lax.broadcasted_iota(jnp.int32, (rows, rows), 1)
    earlier = (r_i < c_i).astype(BF16)
    carry = carry_sc[...]
    idx_rows, gate_rows, rank_rows = [], [], []
    for g in range(parts):
        hi = hfs[g].astype(BF16)
        lo = (hfs[g] - hi.astype(F32)).astype(BF16)
        logits = _dot_nt(wrh_ref[...], hi) + _dot_nt(wrl_ref[...], hi) + _dot_nt(wrh_ref[...], lo) + br_ref[...]
        vals, idxs, hots = [], [], []
        cur = logits
        for _ in range(TOP_K):
            mx = jnp.max(cur, axis=0, keepdims=True)
            idx = jnp.min(jnp.where(cur == mx, e_iota, N_EXPERTS), axis=0, keepdims=True)
            hot = e_iota == idx
            vals.append(mx)
            idxs.append(idx)
            hots.append(hot)
            cur = jnp.where(hot, -jnp.inf, cur)
        exps = [jnp.exp(v - vals[0]) for v in vals]
        den = exps[0] + exps[1] + exps[2] + exps[3]
        chosen = jnp.zeros((N_EXPERTS, rows), F32)
        for hot in hots:
            chosen = chosen + hot.astype(F32)
        before = _dot(chosen.astype(BF16), earlier) + carry
        carry = carry + jnp.sum(chosen, axis=1, keepdims=True)
        idx_rows.append(idxs)
        gate_rows.append([e / den for e in exps])
        rank_rows.append([jnp.sum(jnp.where(hot, before, 0.0), axis=0, keepdims=True).astype(jnp.int32)
                          for hot in hots])
    carry_sc[...] = carry
    cnt_ref[...] = carry

    def tile_rows(per_part):
        return [jnp.concatenate([per_part[g][k] for g in range(parts)], axis=1) for k in range(TOP_K)]

    ti_ref[...] = _sublane_pack(tile_rows(idx_rows), tm)
    tg_ref[...] = _sublane_pack(tile_rows(gate_rows), tm)
    tr_ref[...] = _sublane_pack(tile_rows(rank_rows), tm)


def _post_attn(x2d, o_diff, o_na, w_out, gq, w_mq, kv, w_mo, gf, wr_hi_t, wr_lo_t, b_router, seq):
    n, d = x2d.shape
    tm = POST_TM
    m_tok = kv.shape[1]
    const = lambda i: (0, 0)
    row = lambda i: (i, 0)
    col = lambda i: (0, i)
    kern = functools.partial(_post_kernel, tm=tm, parts=POST_PARTS)
    return pl.pallas_call(
        kern,
        grid=(n // tm,),
        in_specs=[pl.BlockSpec((tm, d), row),
                  pl.BlockSpec((tm, DIFF_WIDTH), row),
                  pl.BlockSpec((tm, NA_WIDTH), row),
                  pl.BlockSpec(w_out.shape, const),
                  pl.BlockSpec((1, d), const),
                  pl.BlockSpec(w_mq.shape, const),
                  pl.BlockSpec((1, m_tok, 2 * MEM_WIDTH), lambda i: ((i * tm) // seq, 0, 0)),
                  pl.BlockSpec(w_mo.shape, const),
                  pl.BlockSpec((1, d), const),
                  pl.BlockSpec(wr_hi_t.shape, const),
                  pl.BlockSpec(wr_lo_t.shape, const),
                  pl.BlockSpec((N_EXPERTS, 1), const)],
        out_specs=[pl.BlockSpec((tm, d), row),
                   pl.BlockSpec((tm * SUBLANES, LANES), row),
                   pl.BlockSpec((SUBLANES, tm), col),
                   pl.BlockSpec((SUBLANES, tm), col),
                   pl.BlockSpec((SUBLANES, tm), col),
                   pl.BlockSpec((N_EXPERTS, 1), const)],
        out_shape=[jax.ShapeDtypeStruct((n, d), F32),
                   jax.ShapeDtypeStruct((n * SUBLANES, LANES), F32),
                   jax.ShapeDtypeStruct((SUBLANES, n), jnp.int32),
                   jax.ShapeDtypeStruct((SUBLANES, n), F32),
                   jax.ShapeDtypeStruct((SUBLANES, n), jnp.int32),
                   jax.ShapeDtypeStruct((N_EXPERTS, 1), F32)],
        scratch_shapes=[pltpu.VMEM((N_EXPERTS, 1), F32)],
        compiler_params=_params(("arbitrary",)),
        name="post_attn",
    )(x2d, o_diff, o_na, w_out, gq.reshape(1, d), w_mq, kv, w_mo, gf.reshape(1, d), wr_hi_t, wr_lo_t,
      b_router.reshape(N_EXPERTS, 1))


def _row_copy(src, src_row, dst, dst_row, sem):
    return pltpu.make_async_copy(
        src.at[pl.ds(pl.multiple_of(src_row * SUBLANES, SUBLANES), SUBLANES), :],
        dst.at[pl.ds(pl.multiple_of(dst_row * SUBLANES, SUBLANES), SUBLANES), :],
        sem)


def _ffn_kernel(blk_e_ref, nact_ref, slot_cur_ref, slot_next_ref, slot_prev_ref, hf_ref,
                w1_ref, b1_ref, w2_ref, b2_ref, yc_ref,
                xbuf, ybuf, w1b_sc, w2b_sc, gsem, ssem, *, blk, d_ff, n_tok):
    i = pl.program_id(0)
    nact = nact_ref[0]
    slot = i % 2
    y_slot = i % Y_SLOTS
    y_prev = (i + Y_SLOTS - 1) % Y_SLOTS

    trash_row = n_tok * TOP_K

    def gather(slot_ref, t, to_slot):
        tok = jnp.minimum(lax.shift_right_logical(slot_ref[t], TOP_K.bit_length() - 1), n_tok - 1)
        return _row_copy(hf_ref, tok, xbuf.at[to_slot], t, gsem.at[to_slot])

    def scatter(t):
        dst = jnp.where(i == 0, trash_row + t, slot_prev_ref[t])
        return _row_copy(ybuf.at[y_prev], t, yc_ref, dst, ssem)

    def wait_all(copy_of_row):
        def drain(g, carry):
            for u in range(WAITS_PER_TRIP):
                copy_of_row(g * WAITS_PER_TRIP + u).wait()
            return carry
        lax.fori_loop(0, blk // WAITS_PER_TRIP, drain, 0)

    @pl.when(i == 0)
    def _():
        ybuf[...] = jnp.zeros(ybuf.shape, F32)
        for t in range(blk):
            gather(slot_cur_ref, t, 0).start(priority=t % 2)

    @pl.when((i < nact) & ((i == 0) | (blk_e_ref[i] != blk_e_ref[jnp.maximum(i - 1, 0)])))
    def _():
        rows = 256
        for c in range(w1b_sc.shape[0] // rows):
            w1b_sc[c * rows:(c + 1) * rows, :] = w1_ref[0, c * rows:(c + 1) * rows, :].astype(BF16)
        for c in range(w2b_sc.shape[0] // rows):
            w2b_sc[c * rows:(c + 1) * rows, :] = w2_ref[0, c * rows:(c + 1) * rows, :].astype(BF16)

    @pl.when(i <= nact)
    def _():
        wait_all(lambda t: gather(slot_cur_ref, t, slot))

    @pl.when((i >= 1) & (i <= nact))
    def _():
        wait_all(scatter)

    @pl.when(i < nact)
    def _():
        for t in range(blk):
            gather(slot_next_ref, t, 1 - slot).start(priority=t % 2)
            scatter(t).start(priority=(t + 1) % 2)
        x = _tiles_to_rows(xbuf, blk, lead=slot).astype(BF16)
        gu = _dot(x, w1b_sc[...]) + b1_ref[0]
        gate = jnp.minimum(gu[:, :d_ff], SWIGLU_LIMIT)
        lin = jnp.clip(gu[:, d_ff:], -SWIGLU_LIMIT, SWIGLU_LIMIT)
        act = gate * jax.nn.sigmoid(SWIGLU_ALPHA * gate) * (lin + 1.0)
        y = _dot(act.astype(BF16), w2b_sc[...]) + b2_ref[0]
        _rows_to_tiles(ybuf.at[y_slot], y, blk)

    @pl.when(i == nact)
    def _():
        def issue(t, carry):
            scatter(t).start()
            return carry
        lax.fori_loop(0, blk, issue, 0)
        wait_all(scatter)


def _expert_ffn(blk_e, nact, row_slot, hf_tiles, w1, b1, w2, b2, n_tok):
    blk = FFN_BLK
    nblk = row_slot.shape[0] // blk
    n_slots = n_tok * TOP_K
    assert TOP_K & (TOP_K - 1) == 0
    _, d, two_ff = w1.shape
    d_ff = two_ff // 2
    kern = functools.partial(_ffn_kernel, blk=blk, d_ff=d_ff, n_tok=n_tok)
    smem_blk = lambda f: pl.BlockSpec((blk,), f, memory_space=pltpu.SMEM)
    grid_spec = pltpu.PrefetchScalarGridSpec(
        num_scalar_prefetch=2,
        grid=(nblk,),
        in_specs=[smem_blk(lambda i, be, na: (i,)),
                  smem_blk(lambda i, be, na: (jnp.minimum(i + 1, nblk - 1),)),
                  smem_blk(lambda i, be, na: (jnp.maximum(i - 1, 0),)),
                  pl.BlockSpec(memory_space=pl.ANY),
                  pl.BlockSpec((1, d, two_ff), lambda i, be, na: (be[i], 0, 0)),
                  pl.BlockSpec((1, 1, two_ff), lambda i, be, na: (be[i], 0, 0)),
                  pl.BlockSpec((1, d_ff, d), lambda i, be, na: (be[i], 0, 0)),
                  pl.BlockSpec((1, 1, d), lambda i, be, na: (be[i], 0, 0))],
        out_specs=pl.BlockSpec(memory_space=pl.ANY),
        scratch_shapes=[pltpu.VMEM((2, blk * SUBLANES, LANES), F32),
                        pltpu.VMEM((Y_SLOTS, blk * SUBLANES, LANES), F32),
                        pltpu.VMEM((d, two_ff), BF16),
                        pltpu.VMEM((d_ff, d), BF16),
                        pltpu.SemaphoreType.DMA((2,)),
                        pltpu.SemaphoreType.DMA(())],
    )
    return pl.pallas_call(
        kern,
        grid_spec=grid_spec,
        out_shape=jax.ShapeDtypeStruct(((n_slots + blk) * SUBLANES, LANES), F32),
        compiler_params=_params(("arbitrary",)),
        name="expert_ffn",
    )(blk_e, nact, row_slot, row_slot, row_slot, hf_tiles, w1, b1.reshape(N_EXPERTS, 1, two_ff), w2,
      b2.reshape(N_EXPERTS, 1, d))


def _combine_kernel(x2_ref, tg_ref, gfin_ref, yc_ref, o_ref, *, tm, final_norm):
    gates = tg_ref[...]
    pieces = []
    for j in range(SUBLANES):
        acc = x2_ref[:, j * LANES:(j + 1) * LANES]
        for k in range(TOP_K):
            acc = acc + gates[:, k:k + 1] * yc_ref[pl.ds(k * SUBLANES + j, tm, stride=TOP_K * SUBLANES), :]
        pieces.append(acc)
    out = jnp.concatenate(pieces, axis=1)
    if final_norm:
        out = _rms(out, gfin_ref[...])
    o_ref[...] = out


def _combine(x2, gates, g_final, y_slots, final_norm):
    n, d = x2.shape
    tm = COMB_TM
    kern = functools.partial(_combine_kernel, tm=tm, final_norm=final_norm)
    return pl.pallas_call(
        kern,
        grid=(n // tm,),
        in_specs=[pl.BlockSpec((tm, d), lambda i: (i, 0)),
                  pl.BlockSpec((tm, SUBLANES), lambda i: (i, 0)),
                  pl.BlockSpec((1, d), lambda i: (0, 0)),
                  pl.BlockSpec((tm * TOP_K * SUBLANES, LANES), lambda i: (i, 0))],
        out_specs=pl.BlockSpec((tm, d), lambda i: (i, 0)),
        out_shape=jax.ShapeDtypeStruct((n, d), F32),
        compiler_params=_params(("parallel",)),
        name="combine",
    )(x2, gates, g_final.reshape(1, d), y_slots)


def _invert_kernel(pad_lo_ref, pad_hi_ref, dest_ref, slot_ref, *, chunk, n_assign, blk):
    c = pl.program_id(0)

    @pl.when(c == 0)
    def _():
        def expert(e, carry):
            def pad_row(r, carry2):
                slot_ref[r] = n_assign + (r & (blk - 1))
                return carry2
            lax.fori_loop(pad_lo_ref[e], pad_hi_ref[e], pad_row, 0)
            return carry
        lax.fori_loop(0, N_EXPERTS, expert, 0)

    def body(g, carry):
        a0 = g * INVERT_UNROLL
        slot0 = c * chunk + a0
        for u in range(INVERT_UNROLL):
            slot_ref[dest_ref[a0 + u]] = slot0 + u
        return carry
    lax.fori_loop(0, chunk // INVERT_UNROLL, body, 0)


def _invert_routing(dest, pad_lo, pad_hi, p_rows):
    n_assign = dest.shape[0]
    chunk = INVERT_CHUNK
    blk = FFN_BLK
    assert n_assign % chunk == 0 and chunk % INVERT_UNROLL == 0 and blk & (blk - 1) == 0
    kern = functools.partial(_invert_kernel, chunk=chunk, n_assign=n_assign, blk=blk)
    whole_smem = pl.BlockSpec(memory_space=pltpu.SMEM)
    grid_spec = pltpu.PrefetchScalarGridSpec(
        num_scalar_prefetch=2,
        grid=(n_assign // chunk,),
        in_specs=[pl.BlockSpec((chunk,), lambda c, lo, hi: (c,), memory_space=pltpu.SMEM)],
        out_specs=whole_smem,
    )
    return pl.pallas_call(
        kern,
        grid_spec=grid_spec,
        out_shape=jax.ShapeDtypeStruct((p_rows,), jnp.int32),
        compiler_params=_params(("arbitrary",)),
        name="invert_routing",
    )(pad_lo, pad_hi, dest)


def _routing_tables(counts_f32, top_i, rank, n_tok):
    blk = FFN_BLK
    n_assign = n_tok * TOP_K
    counts = counts_f32.astype(jnp.int32)
    pcounts = ((counts + blk - 1) // blk) * blk
    pends = jnp.cumsum(pcounts)
    pstarts = pends - pcounts
    dest = (pstarts[top_i] + rank).reshape(-1)
    p_rows = n_assign + N_EXPERTS * blk
    nblk = p_rows // blk
    pad_lo = (pstarts + counts).astype(jnp.int32)
    pad_hi = pends.at[N_EXPERTS - 1].set(p_rows).astype(jnp.int32)
    row_slot = _invert_routing(dest.astype(jnp.int32), pad_lo, pad_hi, p_rows)
    blk_start = jnp.arange(nblk, dtype=jnp.int32) * blk
    blk_e = jnp.minimum(jnp.sum(blk_start[:, None] >= pends[None, :], axis=1), N_EXPERTS - 1).astype(jnp.int32)
    nact = (pends[-1] // blk).astype(jnp.int32).reshape(1)
    return row_slot, blk_e, nact


def _lambda_init(layer):
    return 0.8 - 0.6 * math.exp(-0.3 * layer)


def kernel(x, mem, norm_mix_g, w_in, lambda_q1, lambda_k1, lambda_q2, lambda_k2, subln_g, rpb, w_out,
           norm_mem_q_g, norm_mem_kv_g, w_mq, w_mkv, w_mo, norm_ffn_g, w_router, b_router, w1, b1, w2, b2,
           norm_final_g):
    b, s, d = x.shape
    n = b * s
    depth = w_in.shape[0]
    assert d == SUBLANES * LANES and s % GRID_W == 0
    slopes = jnp.asarray([2.0 ** (-8.0 * (i + 1) / DIFF_HEADS) for i in range(DIFF_HEADS)], F32)
    x2d = x.reshape(n, d)
    for l in range(depth):
        lam_init = _lambda_init(l)
        lam = (jnp.exp(jnp.sum(lambda_q1[l] * lambda_k1[l])) - jnp.exp(jnp.sum(lambda_q2[l] * lambda_k2[l]))
               + lam_init).astype(F32)
        qkv = _qkv_proj(x2d, norm_mix_g[l], w_in[l].astype(BF16))
        qkv3 = qkv.reshape(b, s, qkv.shape[1])
        o_diff = _diff_attn(qkv3, lam, slopes, subln_g[l], lam_init).reshape(n, DIFF_WIDTH)
        o_na = _na_attn(qkv3, _na_bias_table(rpb[l], s // GRID_W)).reshape(n, NA_WIDTH)
        kv = _mem_kv(mem, norm_mem_kv_g[l], w_mkv[l].astype(BF16))
        wr_t = w_router[l].T
        wr_hi_t = wr_t.astype(BF16)
        wr_lo_t = (wr_t - wr_hi_t.astype(F32)).astype(BF16)
        x2, hf_tiles, ti, tg, tr, counts = _post_attn(
            x2d, o_diff, o_na, w_out[l].astype(BF16), norm_mem_q_g[l], w_mq[l].astype(BF16), kv,
            w_mo[l].astype(BF16), norm_ffn_g[l], wr_hi_t, wr_lo_t, b_router[l], s)
        row_slot, blk_e, nact = _routing_tables(counts[:, 0], ti[:TOP_K].T, tr[:TOP_K].T, n)
        y_slots = _expert_ffn(blk_e, nact, row_slot, hf_tiles, w1[l], b1[l], w2[l], b2[l], n)
        x2d = _combine(x2, tg.T, norm_final_g, y_slots, final_norm=(l == depth - 1))
    return x2d.reshape(b, s, d)
```

```python
import functools
import math

import jax
import jax.numpy as jnp
import numpy as np
from jax import lax
from jax.experimental import pallas as pl
from jax.experimental.pallas import tpu as pltpu

HEAD_DIM = 64
DIFF_HEADS = 4
DIFF_VDIM = 2 * HEAD_DIM
DIFF_QK_WIDTH = DIFF_HEADS * 2 * HEAD_DIM
DIFF_WIDTH = DIFF_HEADS * DIFF_VDIM
NA_HEADS = 8
NA_WIDTH = NA_HEADS * HEAD_DIM
GRID_W = 64
NA_ROWS = 8
NA_COLS = 16
MEM_HEADS = 4
MEM_HEAD_DIM = 64
MEM_WIDTH = MEM_HEADS * MEM_HEAD_DIM
N_EXPERTS = 32
TOP_K = 4
SWIGLU_LIMIT = 7.0
SWIGLU_ALPHA = 1.702
RMS_EPS = 1e-5

LANES = 128
SUBLANES = 8
NEG_BIG = -1e30

F32 = jnp.float32
BF16 = jnp.bfloat16

QKV_TM = 512
DIFF_TQ = 512
DIFF_TK = 512
NA_R = 8
POST_TM = 512
POST_PARTS = 2
FFN_BLK = 256
COMB_TM = 256
WAITS_PER_TRIP = 16
INVERT_CHUNK = 8192
INVERT_UNROLL = 16
Y_SLOTS = 3
VMEM_LIMIT = 56 * 1024 * 1024


def _params(sem):
    return pltpu.CompilerParams(dimension_semantics=sem, vmem_limit_bytes=VMEM_LIMIT)


def _rms(x, g):
    return x * lax.rsqrt(jnp.mean(x * x, axis=-1, keepdims=True) + RMS_EPS) * g


def _dot(a, b):
    return jnp.dot(a, b, preferred_element_type=F32)


def _dot_nt(a, b):
    return lax.dot_general(a, b, (((1,), (1,)), ((), ())), preferred_element_type=F32)


def _rows_to_tiles(ref, val, rows):
    for j in range(SUBLANES):
        ref[pl.ds(j, rows, stride=SUBLANES), :] = val[:, j * LANES:(j + 1) * LANES]


def _tiles_to_rows(ref, rows, lead=None):
    if lead is None:
        parts = [ref[pl.ds(j, rows, stride=SUBLANES), :] for j in range(SUBLANES)]
    else:
        parts = [ref[lead, pl.ds(j, rows, stride=SUBLANES), :] for j in range(SUBLANES)]
    return jnp.concatenate(parts, axis=1)


def _qkv_kernel(x_ref, g_ref, w_ref, o_ref):
    h = _rms(x_ref[...], g_ref[...]).astype(BF16)
    cols = o_ref.shape[1]
    step = 512
    for c in range(cols // step):
        o_ref[:, c * step:(c + 1) * step] = _dot(h, w_ref[:, c * step:(c + 1) * step]).astype(o_ref.dtype)


def _qkv_proj(x2d, g, w_bf16):
    n, d = x2d.shape
    cols = w_bf16.shape[1]
    return pl.pallas_call(
        _qkv_kernel,
        grid=(n // QKV_TM,),
        in_specs=[pl.BlockSpec((QKV_TM, d), lambda i: (i, 0)),
                  pl.BlockSpec((1, d), lambda i: (0, 0)),
                  pl.BlockSpec((d, cols), lambda i: (0, 0))],
        out_specs=pl.BlockSpec((QKV_TM, cols), lambda i: (i, 0)),
        out_shape=jax.ShapeDtypeStruct((n, cols), BF16),
        compiler_params=_params(("parallel",)),
        name="qkv_proj",
    )(x2d, g.reshape(1, d), w_bf16)


ALIBI_AUG_LEFT, ALIBI_AUG_RIGHT, ALIBI_AUG_NONE = 0, 1, 2
ONES_ROWS = 16
EXP_UNDERFLOW = 105.0
NORM_BOUND_SLACK = 1.02


def _bf16_exact_split(n):
    lo_bits = max(0, int(n.max()).bit_length() - 8)
    lo = n % (1 << lo_bits)
    return n - lo, lo


def _alibi_aug_tables(tq, tk):
    ii_hi, ii_lo = _bf16_exact_split(np.arange(tq))
    jj_hi, jj_lo = _bf16_exact_split(np.arange(tk))
    qa = np.zeros((DIFF_HEADS, tq, LANES), np.float32)
    ka = np.zeros((DIFF_HEADS, 3, tk, LANES), np.float32)
    for h in range(DIFF_HEADS):
        sl = 2.0 ** (-8.0 * (h + 1) / DIFF_HEADS)
        qa[h, :, 0], qa[h, :, 1], qa[h, :, 2], qa[h, :, 3] = -sl * ii_hi, -sl * ii_lo, 1.0, 1.0
        qa[h, :, 4], qa[h, :, 5], qa[h, :, 6], qa[h, :, 7] = sl * ii_hi, sl * ii_lo, 1.0, 1.0
        ka[h, ALIBI_AUG_LEFT, :, 0], ka[h, ALIBI_AUG_LEFT, :, 1] = 1.0, 1.0
        ka[h, ALIBI_AUG_LEFT, :, 2], ka[h, ALIBI_AUG_LEFT, :, 3] = sl * jj_hi, sl * jj_lo
        ka[h, ALIBI_AUG_RIGHT, :, 4], ka[h, ALIBI_AUG_RIGHT, :, 5] = 1.0, 1.0
        ka[h, ALIBI_AUG_RIGHT, :, 6], ka[h, ALIBI_AUG_RIGHT, :, 7] = -sl * jj_hi, -sl * jj_lo
    return jnp.asarray(qa, BF16), jnp.asarray(ka, BF16)


def _diff_kernel(lam_ref, slope_ref, q_ref, k_ref, vt_ref, qaug_ref, kaug_ref, g_ref, o_ref,
                 qa_sc, s0_sc, s1_sc, mb_sc, m_sc, acc_sc, ksq_sc, *, tq, tk, seq, out_scale):
    h = pl.program_id(1)
    i = pl.program_id(2)
    slope = slope_ref[h]
    lam = lam_ref[0]
    nk = seq // tk
    i0 = i * tq
    jd = i0 // tk

    sel_r = lax.broadcasted_iota(jnp.int32, (LANES, LANES), 0) // HEAD_DIM
    sel_c = lax.broadcasted_iota(jnp.int32, (LANES, LANES), 1)
    sel = (sel_r == sel_c).astype(BF16)

    def max_sq_norm(x):
        xf = x.astype(F32)
        return jnp.max(_dot((xf * xf).astype(BF16), sel), axis=0, keepdims=True)

    @pl.when(i == 0)
    def _():
        ksq_sc[...] = max_sq_norm(k_ref[0])

    q = q_ref[0] * jnp.asarray(1.0 / math.sqrt(HEAD_DIM), BF16)
    lane = lax.broadcasted_iota(jnp.int32, (tq, LANES), 1)
    zero = jnp.zeros_like(q)
    qa_sc[:tq, :LANES] = jnp.where(lane < HEAD_DIM, q, zero)
    qa_sc[tq:, :LANES] = jnp.where(lane >= HEAD_DIM, q, zero)
    qa_sc[:tq, LANES:] = qaug_ref[0]
    qa_sc[tq:, LANES:] = qaug_ref[0]
    m_sc[...] = jnp.full(m_sc.shape, NEG_BIG, F32)
    acc_sc[...] = jnp.zeros(acc_sc.shape, F32)

    def scores(j, side):
        start = pl.multiple_of(j * tk, tk)
        kba = jnp.concatenate([k_ref[0, pl.ds(start, tk), :], kaug_ref[0, side]], axis=1)
        return _dot_nt(kba, qa_sc[...])

    s_bufs = (s0_sc, s1_sc)

    def stash(slot, st):
        s_bufs[slot][...] = st
        mb_sc[slot] = jnp.max(st, axis=0, keepdims=True)

    def consume(slot, j, c):
        m_prev = m_sc[...]
        m_new = jnp.maximum(m_prev, mb_sc[slot] + c)
        alpha = jnp.exp(m_prev - m_new)
        p_t = jnp.exp(s_bufs[slot][...] - (m_new - c)).astype(BF16)
        start = pl.multiple_of(j * tk, tk)
        vta = jnp.concatenate([vt_ref[0, :, pl.ds(start, tk)], jnp.ones((ONES_ROWS, tk), BF16)], axis=0)
        acc_sc[...] = alpha * acc_sc[...] + _dot(vta, p_t)
        m_sc[...] = m_new

    def block_const(j):
        return -slope * jnp.abs(i0 - j * tk).astype(F32)

    rel_t = (lax.broadcasted_iota(jnp.int32, (tk, tq), 0) - lax.broadcasted_iota(jnp.int32, (tk, tq), 1)).astype(F32)
    bias_t = -slope * jnp.abs(rel_t + (jd * tk - i0).astype(F32))
    stash(0, scores(jd, ALIBI_AUG_NONE) + jnp.concatenate([bias_t, bias_t], axis=1))

    qk_bound = jnp.sqrt(jnp.max(max_sq_norm(q) * ksq_sc[...])) * NORM_BOUND_SLACK
    reach = (qk_bound - jnp.min(mb_sc[0]) + EXP_UNDERFLOW) / slope
    reach = jnp.minimum(reach, 2.0 * seq)
    j_lo = jnp.clip(jnp.floor((i0 - tk + 1 - reach) / tk).astype(jnp.int32) + 1, 0, jd)
    j_hi = jnp.clip(jnp.ceil((reach + i0 + tq - 1) / tk).astype(jnp.int32) - 1, jd, nk - 1)
    n_blocks = j_hi - j_lo + 1

    def block_of(t):
        return j_lo + jnp.where(t - 1 < jd - j_lo, t - 1, t)

    def produce(t, slot):
        j = block_of(t)
        stash(slot, scores(j, jnp.where(j < jd, ALIBI_AUG_LEFT, ALIBI_AUG_RIGHT)))

    def consume_step(t, slot):
        j = jnp.where(t == 0, jd, block_of(t))
        consume(slot, j, jnp.where(t == 0, 0.0, block_const(j)))

    def body(u, carry):
        t = 2 * u
        produce(t + 1, 1)
        consume_step(t, 0)
        produce(t + 2, 0)
        consume_step(t + 1, 1)
        return carry

    def body4(u, carry):
        body(2 * u, carry)
        body(2 * u + 1, carry)
        return carry

    quads = (n_blocks - 1) // 4
    lax.fori_loop(0, quads, body4, 0)
    lax.fori_loop(2 * quads, (n_blocks - 1) // 2, body, 0)

    @pl.when(n_blocks % 2 == 0)
    def _():
        produce(n_blocks - 1, 1)
        consume_step(n_blocks - 2, 0)
        consume_step(n_blocks - 1, 1)

    @pl.when(n_blocks % 2 == 1)
    def _():
        consume_step(n_blocks - 1, 0)

    acc = acc_sc[...]
    o = acc[:DIFF_VDIM] * (1.0 / acc[DIFF_VDIM:DIFF_VDIM + 1])
    d = o[:, :tq] - lam * o[:, tq:]
    y = d * lax.rsqrt(jnp.mean(d * d, axis=0, keepdims=True) + RMS_EPS) * g_ref[...] * out_scale
    o_ref[0] = y.T.astype(o_ref.dtype)


def _diff_attn(qkv3, lam, slopes, subln_g, lam_init):
    b, s, _ = qkv3.shape
    tq, tk = DIFF_TQ, DIFF_TK
    assert tk % tq == 0 and s % (2 * tk) == 0
    kcol = DIFF_QK_WIDTH // LANES
    v_t = qkv3[:, :, 2 * DIFF_QK_WIDTH:2 * DIFF_QK_WIDTH + DIFF_WIDTH].transpose(0, 2, 1)
    qaug, kaug = _alibi_aug_tables(tq, tk)
    kern = functools.partial(_diff_kernel, tq=tq, tk=tk, seq=s, out_scale=1.0 - lam_init)
    return pl.pallas_call(
        kern,
        grid=(b, DIFF_HEADS, s // tq),
        in_specs=[pl.BlockSpec(memory_space=pltpu.SMEM),
                  pl.BlockSpec(memory_space=pltpu.SMEM),
                  pl.BlockSpec((1, tq, LANES), lambda bi, h, i: (bi, i, h)),
                  pl.BlockSpec((1, s, LANES), lambda bi, h, i: (bi, 0, kcol + h)),
                  pl.BlockSpec((1, DIFF_VDIM, s), lambda bi, h, i: (bi, h, 0)),
                  pl.BlockSpec((1, tq, LANES), lambda bi, h, i: (h, 0, 0)),
                  pl.BlockSpec((1, 3, tk, LANES), lambda bi, h, i: (h, 0, 0, 0)),
                  pl.BlockSpec((DIFF_VDIM, 1), lambda bi, h, i: (0, 0))],
        out_specs=pl.BlockSpec((1, tq, LANES), lambda bi, h, i: (bi, i, h)),
        out_shape=jax.ShapeDtypeStruct((b, s, DIFF_WIDTH), BF16),
        scratch_shapes=[pltpu.VMEM((2 * tq, 2 * LANES), BF16),
                        pltpu.VMEM((tk, 2 * tq), F32),
                        pltpu.VMEM((tk, 2 * tq), F32),
                        pltpu.VMEM((2, 1, 2 * tq), F32),
                        pltpu.VMEM((1, 2 * tq), F32),
                        pltpu.VMEM((DIFF_VDIM + ONES_ROWS, 2 * tq), F32),
                        pltpu.VMEM((1, LANES), F32)],
        compiler_params=_params(("parallel", "parallel", "arbitrary")),
        name="diff_attn",
    )(lam.reshape(1), slopes, qkv3, qkv3, v_t, qaug, kaug, subln_g.reshape(DIFF_VDIM, 1))


def _na_bias_table(rpb, rows):
    kr = min(NA_ROWS, rows)
    q = np.arange(GRID_W)[:, None]
    j = np.arange(GRID_W)[None, :]
    cs = np.clip(q - NA_COLS // 2, 0, GRID_W - NA_COLS)
    valid = (j >= cs) & (j < cs + NA_COLS)
    dc = j - q + (NA_COLS - 1)
    sel = ((dc[:, :, None] == np.arange(2 * NA_COLS - 1)[None, None, :]) & valid[:, :, None]).astype(np.float32)
    col = jnp.einsum('hrd,qjd->hrqj', rpb.astype(F32), jnp.asarray(sel), precision=lax.Precision.HIGHEST)
    col = jnp.where(valid[None, None], col, NEG_BIG)
    cases = [col[:, NA_ROWS - 1 - c:NA_ROWS - 1 - c + kr] for c in range(kr)]
    tbl = jnp.stack(cases, axis=0).transpose(0, 1, 3, 2, 4)
    return tbl.reshape(kr, NA_HEADS // 2, 2 * GRID_W, kr * GRID_W)


def _na_kernel(q_ref, k_ref, v_ref, bias_ref, o_ref, *, rows_per_step, rows, kr):
    i = pl.program_id(2)
    lane = lax.broadcasted_iota(jnp.int32, (GRID_W, LANES), 1)
    win = kr * GRID_W

    starts, scores, probs = [], [], []
    for rr in range(rows_per_step):
        r = i * rows_per_step + rr
        rs = jnp.clip(r - kr // 2, 0, rows - kr)
        q = q_ref[0, rr * GRID_W:(rr + 1) * GRID_W, :]
        q = q * jnp.asarray(1.0 / math.sqrt(HEAD_DIM), BF16)
        zero = jnp.zeros_like(q)
        qs = jnp.concatenate([jnp.where(lane < HEAD_DIM, q, zero), jnp.where(lane >= HEAD_DIM, q, zero)], axis=0)
        start = pl.multiple_of(rs * GRID_W, GRID_W)
        starts.append(start)
        scores.append(_dot_nt(qs, k_ref[0, pl.ds(start, win), :]) + bias_ref[r - rs, 0])
    for s in scores:
        p = jnp.exp(s - jnp.max(s, axis=1, keepdims=True))
        probs.append((p.astype(BF16), jnp.sum(p, axis=1, keepdims=True)))
    for rr, (p, l) in enumerate(probs):
        o = _dot(p, v_ref[0, pl.ds(starts[rr], win), :]) / l
        out = jnp.where(lane < HEAD_DIM, o[:GRID_W], o[GRID_W:])
        o_ref[0, rr * GRID_W:(rr + 1) * GRID_W, :] = out.astype(o_ref.dtype)


def _na_attn(qkv3, bias_tbl):
    b, s, _ = qkv3.shape
    rows = s // GRID_W
    kr = min(NA_ROWS, rows)
    base = (2 * DIFF_QK_WIDTH + DIFF_WIDTH) // LANES
    pairs = NA_HEADS // 2
    rstep = NA_R
    kern = functools.partial(_na_kernel, rows_per_step=rstep, rows=rows, kr=kr)
    return pl.pallas_call(
        kern,
        grid=(b, pairs, rows // rstep),
        in_specs=[pl.BlockSpec((1, rstep * GRID_W, LANES), lambda bi, p, i: (bi, i, base + p)),
                  pl.BlockSpec((1, s, LANES), lambda bi, p, i: (bi, 0, base + pairs + p)),
                  pl.BlockSpec((1, s, LANES), lambda bi, p, i: (bi, 0, base + 2 * pairs + p)),
                  pl.BlockSpec((kr, 1, 2 * GRID_W, kr * GRID_W), lambda bi, p, i: (0, p, 0, 0))],
        out_specs=pl.BlockSpec((1, rstep * GRID_W, LANES), lambda bi, p, i: (bi, i, p)),
        out_shape=jax.ShapeDtypeStruct((b, s, NA_WIDTH), BF16),
        compiler_params=_params(("parallel", "parallel", "parallel")),
        name="na_attn",
    )(qkv3, qkv3, qkv3, bias_tbl)


def _memkv_kernel(mem_ref, g_ref, w_ref, o_ref):
    h = _rms(mem_ref[0], g_ref[...]).astype(BF16)
    o_ref[0] = _dot(h, w_ref[...]).astype(o_ref.dtype)


def _mem_kv(mem, g, w_bf16):
    b, m, d = mem.shape
    cols = w_bf16.shape[1]
    return pl.pallas_call(
        _memkv_kernel,
        grid=(b,),
        in_specs=[pl.BlockSpec((1, m, d), lambda i: (i, 0, 0)),
                  pl.BlockSpec((1, d), lambda i: (0, 0)),
                  pl.BlockSpec((d, cols), lambda i: (0, 0))],
        out_specs=pl.BlockSpec((1, m, cols), lambda i: (i, 0, 0)),
        out_shape=jax.ShapeDtypeStruct((b, m, cols), BF16),
        compiler_params=_params(("parallel",)),
        name="mem_kv",
    )(mem, g.reshape(1, d), w_bf16)


def _sublane_pack(rows, tm):
    sub = lax.broadcasted_iota(jnp.int32, (SUBLANES, tm), 0)
    out = jnp.zeros((SUBLANES, tm), rows[0].dtype)
    for k, r in enumerate(rows):
        out = jnp.where(sub == k, r, out)
    return out


def _post_kernel(x_ref, od_ref, on_ref, wout_ref, gq_ref, wmq_ref, kv_ref, wmo_ref, gf_ref,
                 wrh_ref, wrl_ref, br_ref,
                 x2_ref, hf_ref, ti_ref, tg_ref, tr_ref, cnt_ref, carry_sc, *, tm, parts):
    step = pl.program_id(0)
    rows = tm // parts

    @pl.when(step == 0)
    def _():
        carry_sc[...] = jnp.zeros(carry_sc.shape, F32)

    head_of_lane = lax.broadcasted_iota(jnp.int32, (rows, MEM_WIDTH), 1) // MEM_HEAD_DIM
    kv = kv_ref[0]

    x1s, qss = [], []
    for g in range(parts):
        rs = slice(g * rows, (g + 1) * rows)
        x1 = (x_ref[rs, :] + _dot(od_ref[rs, :], wout_ref[:DIFF_WIDTH, :])
              + _dot(on_ref[rs, :], wout_ref[DIFF_WIDTH:, :]))
        hm = _rms(x1, gq_ref[...]).astype(BF16)
        q = (_dot(hm, wmq_ref[...]) * (1.0 / math.sqrt(MEM_HEAD_DIM))).astype(BF16)
        zero = jnp.zeros_like(q)
        qss.append(jnp.concatenate([jnp.where(head_of_lane == h, q, zero) for h in range(MEM_HEADS)], axis=0))
        x1s.append(x1)

    probs = []
    for g in range(parts):
        s = _dot_nt(qss[g], kv[:, :MEM_WIDTH])
        p = jnp.exp(s - jnp.max(s, axis=1, keepdims=True))
        probs.append((p.astype(BF16), jnp.sum(p, axis=1, keepdims=True)))

    hfs = []
    for g in range(parts):
        rs = slice(g * rows, (g + 1) * rows)
        p, l = probs[g]
        o4 = _dot(p, kv[:, MEM_WIDTH:]) / l
        o = jnp.zeros((rows, MEM_WIDTH), F32)
        for h in range(MEM_HEADS):
            o = jnp.where(head_of_lane == h, o4[h * rows:(h + 1) * rows], o)
        x2 = x1s[g] + _dot(o.astype(BF16), wmo_ref[...])
        x2_ref[rs, :] = x2
        hf = _rms(x2, gf_ref[...])
        _rows_to_tiles(hf_ref.at[pl.ds(g * rows * SUBLANES, rows * SUBLANES), :], hf, rows)
        hfs.append(hf)

    e_iota = lax.broadcasted_iota(jnp.int32, (N_EXPERTS, rows), 0)
    r_i = lax.broadcasted_iota(jnp.int32, (rows, rows), 0)
    c_i = lax.broadcasted_iota(jnp.int32, (rows, rows), 1)
    earlier = (r_i < c_i).astype(BF16)
    carry = carry_sc[...]
    idx_rows, gate_rows, rank_rows = [], [], []
    for g in range(parts):
        hi = hfs[g].astype(BF16)
        lo = (hfs[g] - hi.astype(F32)).astype(BF16)
        logits = _dot_nt(wrh_ref[...], hi) + _dot_nt(wrl_ref[...], hi) + _dot_nt(wrh_ref[...], lo) + br_ref[...]
        vals, idxs, hots = [], [], []
        cur = logits
        for _ in range(TOP_K):
            mx = jnp.max(cur, axis=0, keepdims=True)
            idx = jnp.min(jnp.where(cur == mx, e_iota, N_EXPERTS), axis=0, keepdims=True)
            hot = e_iota == idx
            vals.append(mx)
            idxs.append(idx)
            hots.append(hot)
            cur = jnp.where(hot, -jnp.inf, cur)
        exps = [jnp.exp(v - vals[0]) for v in vals]
        den = exps[0] + exps[1] + exps[2] + exps[3]
        chosen = jnp.zeros((N_EXPERTS, rows), F32)
        for hot in hots:
            chosen = chosen + hot.astype(F32)
        before = _dot(chosen.astype(BF16), earlier) + carry
        carry = carry + jnp.sum(chosen, axis=1, keepdims=True)
        idx_rows.append(idxs)
        gate_rows.append([e / den for e in exps])
        rank_rows.append([jnp.sum(jnp.where(hot, before, 0.0), axis=0, keepdims=True).astype(jnp.int32)
                          for hot in hots])
    carry_sc[...] = carry
    cnt_ref[...] = carry

    def tile_rows(per_part):
        return [jnp.concatenate([per_part[g][k] for g in range(parts)], axis=1) for k in range(TOP_K)]

    ti_ref[...] = _sublane_pack(tile_rows(idx_rows), tm)
    tg_ref[...] = _sublane_pack(tile_rows(gate_rows), tm)
    tr_ref[...] = _sublane_pack(tile_rows(rank_rows), tm)


def _post_attn(x2d, o_diff, o_na, w_out, gq, w_mq, kv, w_mo, gf, wr_hi_t, wr_lo_t, b_router, seq):
    n, d = x2d.shape
    tm = POST_TM
    m_tok = kv.shape[1]
    const = lambda i: (0, 0)
    row = lambda i: (i, 0)
    col = lambda i: (0, i)
    kern = functools.partial(_post_kernel, tm=tm, parts=POST_PARTS)
    return pl.pallas_call(
        kern,
        grid=(n // tm,),
        in_specs=[pl.BlockSpec((tm, d), row),
                  pl.BlockSpec((tm, DIFF_WIDTH), row),
                  pl.BlockSpec((tm, NA_WIDTH), row),
                  pl.BlockSpec(w_out.shape, const),
                  pl.BlockSpec((1, d), const),
                  pl.BlockSpec(w_mq.shape, const),
                  pl.BlockSpec((1, m_tok, 2 * MEM_WIDTH), lambda i: ((i * tm) // seq, 0, 0)),
                  pl.BlockSpec(w_mo.shape, const),
                  pl.BlockSpec((1, d), const),
                  pl.BlockSpec(wr_hi_t.shape, const),
                  pl.BlockSpec(wr_lo_t.shape, const),
                  pl.BlockSpec((N_EXPERTS, 1), const)],
        out_specs=[pl.BlockSpec((tm, d), row),
                   pl.BlockSpec((tm * SUBLANES, LANES), row),
                   pl.BlockSpec((SUBLANES, tm), col),
                   pl.BlockSpec((SUBLANES, tm), col),
                   pl.BlockSpec((SUBLANES, tm), col),
                   pl.BlockSpec((N_EXPERTS, 1), const)],
        out_shape=[jax.ShapeDtypeStruct((n, d), F32),
                   jax.ShapeDtypeStruct((n * SUBLANES, LANES), F32),
                   jax.ShapeDtypeStruct((SUBLANES, n), jnp.int32),
                   jax.ShapeDtypeStruct((SUBLANES, n), F32),
                   jax.ShapeDtypeStruct((SUBLANES, n), jnp.int32),
                   jax.ShapeDtypeStruct((N_EXPERTS, 1), F32)],
        scratch_shapes=[pltpu.VMEM((N_EXPERTS, 1), F32)],
        compiler_params=_params(("arbitrary",)),
        name="post_attn",
    )(x2d, o_diff, o_na, w_out, gq.reshape(1, d), w_mq, kv, w_mo, gf.reshape(1, d), wr_hi_t, wr_lo_t,
      b_router.reshape(N_EXPERTS, 1))


def _row_copy(src, src_row, dst, dst_row, sem):
    return pltpu.make_async_copy(
        src.at[pl.ds(pl.multiple_of(src_row * SUBLANES, SUBLANES), SUBLANES), :],
        dst.at[pl.ds(pl.multiple_of(dst_row * SUBLANES, SUBLANES), SUBLANES), :],
        sem)


def _ffn_kernel(blk_e_ref, nact_ref, slot_cur_ref, slot_next_ref, slot_prev_ref, hf_ref,
                w1_ref, b1_ref, w2_ref, b2_ref, yc_ref,
                xbuf, ybuf, w1b_sc, w2b_sc, gsem, ssem, *, blk, d_ff, n_tok):
    i = pl.program_id(0)
    nact = nact_ref[0]
    slot = i % 2
    y_slot = i % Y_SLOTS
    y_prev = (i + Y_SLOTS - 1) % Y_SLOTS

    trash_row = n_tok * TOP_K

    def gather(slot_ref, t, to_slot):
        tok = slot_ref[t] & (n_tok - 1)
        return _row_copy(hf_ref, tok, xbuf.at[to_slot], t, gsem.at[to_slot])

    def scatter(t):
        dst = jnp.where(i == 0, trash_row + t, slot_prev_ref[t])
        return _row_copy(ybuf.at[y_prev], t, yc_ref, dst, ssem)

    def wait_all(copy_of_row):
        def drain(g, carry):
            for u in range(WAITS_PER_TRIP):
                copy_of_row(g * WAITS_PER_TRIP + u).wait()
            return carry
        lax.fori_loop(0, blk // WAITS_PER_TRIP, drain, 0)

    @pl.when(i == 0)
    def _():
        ybuf[...] = jnp.zeros(ybuf.shape, F32)
        for t in range(blk):
            gather(slot_cur_ref, t, 0).start(priority=t % 2)

    @pl.when((i < nact) & ((i == 0) | (blk_e_ref[i] != blk_e_ref[jnp.maximum(i - 1, 0)])))
    def _():
        rows = 256
        for c in range(w1b_sc.shape[0] // rows):
            w1b_sc[c * rows:(c + 1) * rows, :] = w1_ref[0, c * rows:(c + 1) * rows, :].astype(BF16)
        for c in range(w2b_sc.shape[0] // rows):
            w2b_sc[c * rows:(c + 1) * rows, :] = w2_ref[0, c * rows:(c + 1) * rows, :].astype(BF16)

    @pl.when(i <= nact)
    def _():
        wait_all(lambda t: gather(slot_cur_ref, t, slot))

    @pl.when((i >= 1) & (i <= nact))
    def _():
        wait_all(scatter)

    @pl.when(i < nact)
    def _():
        for t in range(blk):
            gather(slot_next_ref, t, 1 - slot).start(priority=t % 2)
            scatter(t).start(priority=(t + 1) % 2)
        x = _tiles_to_rows(xbuf, blk, lead=slot).astype(BF16)
        gu = _dot(x, w1b_sc[...]) + b1_ref[0]
        gate = jnp.minimum(gu[:, :d_ff], SWIGLU_LIMIT)
        lin = jnp.clip(gu[:, d_ff:], -SWIGLU_LIMIT, SWIGLU_LIMIT)
        act = gate * jax.nn.sigmoid(SWIGLU_ALPHA * gate) * (lin + 1.0)
        y = _dot(act.astype(BF16), w2b_sc[...]) + b2_ref[0]
        _rows_to_tiles(ybuf.at[y_slot], y, blk)

    @pl.when(i == nact)
    def _():
        def issue(t, carry):
            scatter(t).start()
            return carry
        lax.fori_loop(0, blk, issue, 0)
        wait_all(scatter)


def _expert_ffn(blk_e, nact, row_slot, hf_tiles, w1, b1, w2, b2, n_tok):
    blk = FFN_BLK
    nblk = row_slot.shape[0] // blk
    n_slots = n_tok * TOP_K
    assert n_tok & (n_tok - 1) == 0
    _, d, two_ff = w1.shape
    d_ff = two_ff // 2
    kern = functools.partial(_ffn_kernel, blk=blk, d_ff=d_ff, n_tok=n_tok)
    smem_blk = lambda f: pl.BlockSpec((blk,), f, memory_space=pltpu.SMEM)
    grid_spec = pltpu.PrefetchScalarGridSpec(
        num_scalar_prefetch=2,
        grid=(nblk,),
        in_specs=[smem_blk(lambda i, be, na: (i,)),
                  smem_blk(lambda i, be, na: (jnp.minimum(i + 1, nblk - 1),)),
                  smem_blk(lambda i, be, na: (jnp.maximum(i - 1, 0),)),
                  pl.BlockSpec(memory_space=pl.ANY),
                  pl.BlockSpec((1, d, two_ff), lambda i, be, na: (be[i], 0, 0)),
                  pl.BlockSpec((1, 1, two_ff), lambda i, be, na: (be[i], 0, 0)),
                  pl.BlockSpec((1, d_ff, d), lambda i, be, na: (be[i], 0, 0)),
                  pl.BlockSpec((1, 1, d), lambda i, be, na: (be[i], 0, 0))],
        out_specs=pl.BlockSpec(memory_space=pl.ANY),
        scratch_shapes=[pltpu.VMEM((2, blk * SUBLANES, LANES), F32),
                        pltpu.VMEM((Y_SLOTS, blk * SUBLANES, LANES), F32),
                        pltpu.VMEM((d, two_ff), BF16),
                        pltpu.VMEM((d_ff, d), BF16),
                        pltpu.SemaphoreType.DMA((2,)),
                        pltpu.SemaphoreType.DMA(())],
    )
    return pl.pallas_call(
        kern,
        grid_spec=grid_spec,
        out_shape=jax.ShapeDtypeStruct(((n_slots + blk) * SUBLANES, LANES), F32),
        compiler_params=_params(("arbitrary",)),
        name="expert_ffn",
    )(blk_e, nact, row_slot, row_slot, row_slot, hf_tiles, w1, b1.reshape(N_EXPERTS, 1, two_ff), w2,
      b2.reshape(N_EXPERTS, 1, d))


def _combine_kernel(x2_ref, tg_ref, gfin_ref, *rest, tm, final_norm):
    y_refs, o_ref = rest[:TOP_K], rest[TOP_K]
    gates = tg_ref[...]
    pieces = []
    for j in range(SUBLANES):
        acc = x2_ref[:, j * LANES:(j + 1) * LANES]
        for k in range(TOP_K):
            acc = acc + gates[:, k:k + 1] * y_refs[k][pl.ds(j, tm, stride=SUBLANES), :]
        pieces.append(acc)
    out = jnp.concatenate(pieces, axis=1)
    if final_norm:
        out = _rms(out, gfin_ref[...])
    o_ref[...] = out


def _combine(x2, gates, g_final, y_slots, final_norm):
    n, d = x2.shape
    tm = COMB_TM
    kern = functools.partial(_combine_kernel, tm=tm, final_norm=final_norm)
    tiles = n // tm

    def choice_spec(k):
        return pl.BlockSpec((tm * SUBLANES, LANES), lambda i: (k * tiles + i, 0))

    return pl.pallas_call(
        kern,
        grid=(tiles,),
        in_specs=[pl.BlockSpec((tm, d), lambda i: (i, 0)),
                  pl.BlockSpec((tm, SUBLANES), lambda i: (i, 0)),
                  pl.BlockSpec((1, d), lambda i: (0, 0))] + [choice_spec(k) for k in range(TOP_K)],
        out_specs=pl.BlockSpec((tm, d), lambda i: (i, 0)),
        out_shape=jax.ShapeDtypeStruct((n, d), F32),
        compiler_params=_params(("parallel",)),
        name="combine",
    )(x2, gates, g_final.reshape(1, d), *([y_slots] * TOP_K))


def _invert_kernel(pad_lo_ref, pad_hi_ref, dest_ref, slot_ref, *, chunk, n_tok, blk):
    n_assign = n_tok * TOP_K
    c = pl.program_id(0)

    @pl.when(c == 0)
    def _():
        def expert(e, carry):
            def pad_row(r, carry2):
                slot_ref[r] = n_assign + (r & (blk - 1))
                return carry2
            lax.fori_loop(pad_lo_ref[e], pad_hi_ref[e], pad_row, 0)
            return carry
        lax.fori_loop(0, N_EXPERTS, expert, 0)

    def body(g, carry):
        a0 = g * INVERT_UNROLL
        tok0 = lax.shift_right_logical(c * chunk + a0, TOP_K.bit_length() - 1)
        for u in range(INVERT_UNROLL):
            slot_ref[dest_ref[a0 + u]] = tok0 + u // TOP_K + (u % TOP_K) * n_tok
        return carry
    lax.fori_loop(0, chunk // INVERT_UNROLL, body, 0)


def _invert_routing(dest, pad_lo, pad_hi, p_rows):
    n_assign = dest.shape[0]
    chunk = INVERT_CHUNK
    blk = FFN_BLK
    assert n_assign % chunk == 0 and chunk % INVERT_UNROLL == 0 and blk & (blk - 1) == 0
    assert INVERT_UNROLL % TOP_K == 0 and TOP_K & (TOP_K - 1) == 0
    kern = functools.partial(_invert_kernel, chunk=chunk, n_tok=n_assign // TOP_K, blk=blk)
    whole_smem = pl.BlockSpec(memory_space=pltpu.SMEM)
    grid_spec = pltpu.PrefetchScalarGridSpec(
        num_scalar_prefetch=2,
        grid=(n_assign // chunk,),
        in_specs=[pl.BlockSpec((chunk,), lambda c, lo, hi: (c,), memory_space=pltpu.SMEM)],
        out_specs=whole_smem,
    )
    return pl.pallas_call(
        kern,
        grid_spec=grid_spec,
        out_shape=jax.ShapeDtypeStruct((p_rows,), jnp.int32),
        compiler_params=_params(("arbitrary",)),
        name="invert_routing",
    )(pad_lo, pad_hi, dest)


def _routing_tables(counts_f32, top_i, rank, n_tok):
    blk = FFN_BLK
    n_assign = n_tok * TOP_K
    counts = counts_f32.astype(jnp.int32)
    pcounts = ((counts + blk - 1) // blk) * blk
    pends = jnp.cumsum(pcounts)
    pstarts = pends - pcounts
    dest = (pstarts[top_i] + rank).reshape(-1)
    p_rows = n_assign + N_EXPERTS * blk
    nblk = p_rows // blk
    pad_lo = (pstarts + counts).astype(jnp.int32)
    pad_hi = pends.at[N_EXPERTS - 1].set(p_rows).astype(jnp.int32)
    row_slot = _invert_routing(dest.astype(jnp.int32), pad_lo, pad_hi, p_rows)
    blk_start = jnp.arange(nblk, dtype=jnp.int32) * blk
    blk_e = jnp.minimum(jnp.sum(blk_start[:, None] >= pends[None, :], axis=1), N_EXPERTS - 1).astype(jnp.int32)
    nact = (pends[-1] // blk).astype(jnp.int32).reshape(1)
    return row_slot, blk_e, nact


def _lambda_init(layer):
    return 0.8 - 0.6 * math.exp(-0.3 * layer)


def kernel(x, mem, norm_mix_g, w_in, lambda_q1, lambda_k1, lambda_q2, lambda_k2, subln_g, rpb, w_out,
           norm_mem_q_g, norm_mem_kv_g, w_mq, w_mkv, w_mo, norm_ffn_g, w_router, b_router, w1, b1, w2, b2,
           norm_final_g):
    b, s, d = x.shape
    n = b * s
    depth = w_in.shape[0]
    assert d == SUBLANES * LANES and s % GRID_W == 0
    slopes = jnp.asarray([2.0 ** (-8.0 * (i + 1) / DIFF_HEADS) for i in range(DIFF_HEADS)], F32)
    x2d = x.reshape(n, d)
    for l in range(depth):
        lam_init = _lambda_init(l)
        lam = (jnp.exp(jnp.sum(lambda_q1[l] * lambda_k1[l])) - jnp.exp(jnp.sum(lambda_q2[l] * lambda_k2[l]))
               + lam_init).astype(F32)
        qkv = _qkv_proj(x2d, norm_mix_g[l], w_in[l].astype(BF16))
        qkv3 = qkv.reshape(b, s, qkv.shape[1])
        o_diff = _diff_attn(qkv3, lam, slopes, subln_g[l], lam_init).reshape(n, DIFF_WIDTH)
        o_na = _na_attn(qkv3, _na_bias_table(rpb[l], s // GRID_W)).reshape(n, NA_WIDTH)
        kv = _mem_kv(mem, norm_mem_kv_g[l], w_mkv[l].astype(BF16))
        wr_t = w_router[l].T
        wr_hi_t = wr_t.astype(BF16)
        wr_lo_t = (wr_t - wr_hi_t.astype(F32)).astype(BF16)
        x2, hf_tiles, ti, tg, tr, counts = _post_attn(
            x2d, o_diff, o_na, w_out[l].astype(BF16), norm_mem_q_g[l], w_mq[l].astype(BF16), kv,
            w_mo[l].astype(BF16), norm_ffn_g[l], wr_hi_t, wr_lo_t, b_router[l], s)
        row_slot, blk_e, nact = _routing_tables(counts[:, 0], ti[:TOP_K].T, tr[:TOP_K].T, n)
        y_slots = _expert_ffn(blk_e, nact, row_slot, hf_tiles, w1[l], b1[l], w2[l], b2[l], n)
        x2d = _combine(x2, tg.T, norm_final_g, y_slots, final_norm=(l == depth - 1))
    return x2d.reshape(b, s, d)
```

```python
import functools
import math

import jax
import jax.numpy as jnp
import numpy as np
from jax import lax
from jax.experimental import pallas as pl
from jax.experimental.pallas import tpu as pltpu

HEAD_DIM = 64
DIFF_HEADS = 4
DIFF_VDIM = 2 * HEAD_DIM
DIFF_QK_WIDTH = DIFF_HEADS * 2 * HEAD_DIM
DIFF_WIDTH = DIFF_HEADS * DIFF_VDIM
NA_HEADS = 8
NA_WIDTH = NA_HEADS * HEAD_DIM
GRID_W = 64
NA_ROWS = 8
NA_COLS = 16
MEM_HEADS = 4
MEM_HEAD_DIM = 64
MEM_WIDTH = MEM_HEADS * MEM_HEAD_DIM
N_EXPERTS = 32
TOP_K = 4
SWIGLU_LIMIT = 7.0
SWIGLU_ALPHA = 1.702
RMS_EPS = 1e-5

LANES = 128
SUBLANES = 8
NEG_BIG = -1e30

F32 = jnp.float32
BF16 = jnp.bfloat16

QKV_TM = 512
DIFF_TQ = 512
DIFF_TK = 512
NA_R = 16
POST_TM = 512
POST_PARTS = 2
FFN_BLK = 256
COMB_TM = 256
WAITS_PER_TRIP = 16
INVERT_CHUNK = 8192
INVERT_UNROLL = 16
Y_SLOTS = 3
VMEM_LIMIT = 56 * 1024 * 1024


def _params(sem):
    return pltpu.CompilerParams(dimension_semantics=sem, vmem_limit_bytes=VMEM_LIMIT)


def _rms(x, g):
    return x * lax.rsqrt(jnp.mean(x * x, axis=-1, keepdims=True) + RMS_EPS) * g


def _dot(a, b):
    return jnp.dot(a, b, preferred_element_type=F32)


def _dot_nt(a, b):
    return lax.dot_general(a, b, (((1,), (1,)), ((), ())), preferred_element_type=F32)


def _rows_to_tiles(ref, val, rows):
    for j in range(SUBLANES):
        ref[pl.ds(j, rows, stride=SUBLANES), :] = val[:, j * LANES:(j + 1) * LANES]


def _tiles_to_rows(ref, rows, lead=None):
    if lead is None:
        parts = [ref[pl.ds(j, rows, stride=SUBLANES), :] for j in range(SUBLANES)]
    else:
        parts = [ref[lead, pl.ds(j, rows, stride=SUBLANES), :] for j in range(SUBLANES)]
    return jnp.concatenate(parts, axis=1)


def _qkv_kernel(x_ref, g_ref, w_ref, o_ref):
    h = _rms(x_ref[...], g_ref[...]).astype(BF16)
    cols = o_ref.shape[1]
    step = 512
    for c in range(cols // step):
        o_ref[:, c * step:(c + 1) * step] = _dot(h, w_ref[:, c * step:(c + 1) * step]).astype(o_ref.dtype)


def _qkv_proj(x2d, g, w_bf16):
    n, d = x2d.shape
    cols = w_bf16.shape[1]
    return pl.pallas_call(
        _qkv_kernel,
        grid=(n // QKV_TM,),
        in_specs=[pl.BlockSpec((QKV_TM, d), lambda i: (i, 0)),
                  pl.BlockSpec((1, d), lambda i: (0, 0)),
                  pl.BlockSpec((d, cols), lambda i: (0, 0))],
        out_specs=pl.BlockSpec((QKV_TM, cols), lambda i: (i, 0)),
        out_shape=jax.ShapeDtypeStruct((n, cols), BF16),
        compiler_params=_params(("parallel",)),
        name="qkv_proj",
    )(x2d, g.reshape(1, d), w_bf16)


ALIBI_AUG_LEFT, ALIBI_AUG_RIGHT, ALIBI_AUG_NONE = 0, 1, 2
ONES_ROWS = 16
EXP_UNDERFLOW = 105.0
NORM_BOUND_SLACK = 1.02


def _bf16_exact_split(n):
    lo_bits = max(0, int(n.max()).bit_length() - 8)
    lo = n % (1 << lo_bits)
    return n - lo, lo


def _alibi_aug_tables(tq, tk):
    ii_hi, ii_lo = _bf16_exact_split(np.arange(tq))
    jj_hi, jj_lo = _bf16_exact_split(np.arange(tk))
    qa = np.zeros((DIFF_HEADS, tq, LANES), np.float32)
    ka = np.zeros((DIFF_HEADS, 3, tk, LANES), np.float32)
    for h in range(DIFF_HEADS):
        sl = 2.0 ** (-8.0 * (h + 1) / DIFF_HEADS)
        qa[h, :, 0], qa[h, :, 1], qa[h, :, 2], qa[h, :, 3] = -sl * ii_hi, -sl * ii_lo, 1.0, 1.0
        qa[h, :, 4], qa[h, :, 5], qa[h, :, 6], qa[h, :, 7] = sl * ii_hi, sl * ii_lo, 1.0, 1.0
        ka[h, ALIBI_AUG_LEFT, :, 0], ka[h, ALIBI_AUG_LEFT, :, 1] = 1.0, 1.0
        ka[h, ALIBI_AUG_LEFT, :, 2], ka[h, ALIBI_AUG_LEFT, :, 3] = sl * jj_hi, sl * jj_lo
        ka[h, ALIBI_AUG_RIGHT, :, 4], ka[h, ALIBI_AUG_RIGHT, :, 5] = 1.0, 1.0
        ka[h, ALIBI_AUG_RIGHT, :, 6], ka[h, ALIBI_AUG_RIGHT, :, 7] = -sl * jj_hi, -sl * jj_lo
    return jnp.asarray(qa, BF16), jnp.asarray(ka, BF16)


def _diff_kernel(lam_ref, slope_ref, q_ref, k_ref, vt_ref, qaug_ref, kaug_ref, g_ref, o_ref,
                 qa_sc, s0_sc, s1_sc, mb_sc, m_sc, acc_sc, ksq_sc, *, tq, tk, seq, out_scale):
    h = pl.program_id(1)
    i = pl.program_id(2)
    slope = slope_ref[h]
    lam = lam_ref[0]
    nk = seq // tk
    i0 = i * tq
    jd = i0 // tk

    sel_r = lax.broadcasted_iota(jnp.int32, (LANES, LANES), 0) // HEAD_DIM
    sel_c = lax.broadcasted_iota(jnp.int32, (LANES, LANES), 1)
    sel = (sel_r == sel_c).astype(BF16)

    def max_sq_norm(x):
        xf = x.astype(F32)
        return jnp.max(_dot((xf * xf).astype(BF16), sel), axis=0, keepdims=True)

    @pl.when(i == 0)
    def _():
        ksq_sc[...] = max_sq_norm(k_ref[0])

    q = q_ref[0] * jnp.asarray(1.0 / math.sqrt(HEAD_DIM), BF16)
    lane = lax.broadcasted_iota(jnp.int32, (tq, LANES), 1)
    zero = jnp.zeros_like(q)
    qa_sc[:tq, :LANES] = jnp.where(lane < HEAD_DIM, q, zero)
    qa_sc[tq:, :LANES] = jnp.where(lane >= HEAD_DIM, q, zero)
    qa_sc[:tq, LANES:] = qaug_ref[0]
    qa_sc[tq:, LANES:] = qaug_ref[0]
    m_sc[...] = jnp.full(m_sc.shape, NEG_BIG, F32)
    acc_sc[...] = jnp.zeros(acc_sc.shape, F32)

    def scores(j, side):
        start = pl.multiple_of(j * tk, tk)
        kba = jnp.concatenate([k_ref[0, pl.ds(start, tk), :], kaug_ref[0, side]], axis=1)
        return _dot_nt(kba, qa_sc[...])

    s_bufs = (s0_sc, s1_sc)

    def stash(slot, st):
        s_bufs[slot][...] = st
        mb_sc[slot] = jnp.max(st, axis=0, keepdims=True)

    def consume(slot, j, c):
        m_prev = m_sc[...]
        m_new = jnp.maximum(m_prev, mb_sc[slot] + c)
        alpha = jnp.exp(m_prev - m_new)
        p_t = jnp.exp(s_bufs[slot][...] - (m_new - c)).astype(BF16)
        start = pl.multiple_of(j * tk, tk)
        vta = jnp.concatenate([vt_ref[0, :, pl.ds(start, tk)], jnp.ones((ONES_ROWS, tk), BF16)], axis=0)
        acc_sc[...] = alpha * acc_sc[...] + _dot(vta, p_t)
        m_sc[...] = m_new

    def block_const(j):
        return -slope * jnp.abs(i0 - j * tk).astype(F32)

    rel_t = (lax.broadcasted_iota(jnp.int32, (tk, tq), 0) - lax.broadcasted_iota(jnp.int32, (tk, tq), 1)).astype(F32)
    bias_t = -slope * jnp.abs(rel_t + (jd * tk - i0).astype(F32))
    stash(0, scores(jd, ALIBI_AUG_NONE) + jnp.concatenate([bias_t, bias_t], axis=1))

    qk_bound = jnp.sqrt(jnp.max(max_sq_norm(q) * ksq_sc[...])) * NORM_BOUND_SLACK
    reach = (qk_bound - jnp.min(mb_sc[0]) + EXP_UNDERFLOW) / slope
    reach = jnp.minimum(reach, 2.0 * seq)
    j_lo = jnp.clip(jnp.floor((i0 - tk + 1 - reach) / tk).astype(jnp.int32) + 1, 0, jd)
    j_hi = jnp.clip(jnp.ceil((reach + i0 + tq - 1) / tk).astype(jnp.int32) - 1, jd, nk - 1)
    n_blocks = j_hi - j_lo + 1

    def block_of(t):
        return j_lo + jnp.where(t - 1 < jd - j_lo, t - 1, t)

    def produce(t, slot):
        j = block_of(t)
        stash(slot, scores(j, jnp.where(j < jd, ALIBI_AUG_LEFT, ALIBI_AUG_RIGHT)))

    def consume_step(t, slot):
        j = jnp.where(t == 0, jd, block_of(t))
        consume(slot, j, jnp.where(t == 0, 0.0, block_const(j)))

    def body(u, carry):
        t = 2 * u
        produce(t + 1, 1)
        consume_step(t, 0)
        produce(t + 2, 0)
        consume_step(t + 1, 1)
        return carry

    def body4(u, carry):
        body(2 * u, carry)
        body(2 * u + 1, carry)
        return carry

    quads = (n_blocks - 1) // 4
    lax.fori_loop(0, quads, body4, 0)
    lax.fori_loop(2 * quads, (n_blocks - 1) // 2, body, 0)

    @pl.when(n_blocks % 2 == 0)
    def _():
        produce(n_blocks - 1, 1)
        consume_step(n_blocks - 2, 0)
        consume_step(n_blocks - 1, 1)

    @pl.when(n_blocks % 2 == 1)
    def _():
        consume_step(n_blocks - 1, 0)

    acc = acc_sc[...]
    o = acc[:DIFF_VDIM] * (1.0 / acc[DIFF_VDIM:DIFF_VDIM + 1])
    d = o[:, :tq] - lam * o[:, tq:]
    y = d * lax.rsqrt(jnp.mean(d * d, axis=0, keepdims=True) + RMS_EPS) * g_ref[...] * out_scale
    o_ref[0] = y.T.astype(o_ref.dtype)


def _diff_attn(qkv3, lam, slopes, subln_g, lam_init):
    b, s, _ = qkv3.shape
    tq, tk = DIFF_TQ, DIFF_TK
    assert tk % tq == 0 and s % (2 * tk) == 0
    kcol = DIFF_QK_WIDTH // LANES
    v_t = qkv3[:, :, 2 * DIFF_QK_WIDTH:2 * DIFF_QK_WIDTH + DIFF_WIDTH].transpose(0, 2, 1)
    qaug, kaug = _alibi_aug_tables(tq, tk)
    kern = functools.partial(_diff_kernel, tq=tq, tk=tk, seq=s, out_scale=1.0 - lam_init)
    return pl.pallas_call(
        kern,
        grid=(b, DIFF_HEADS, s // tq),
        in_specs=[pl.BlockSpec(memory_space=pltpu.SMEM),
                  pl.BlockSpec(memory_space=pltpu.SMEM),
                  pl.BlockSpec((1, tq, LANES), lambda bi, h, i: (bi, i, h)),
                  pl.BlockSpec((1, s, LANES), lambda bi, h, i: (bi, 0, kcol + h)),
                  pl.BlockSpec((1, DIFF_VDIM, s), lambda bi, h, i: (bi, h, 0)),
                  pl.BlockSpec((1, tq, LANES), lambda bi, h, i: (h, 0, 0)),
                  pl.BlockSpec((1, 3, tk, LANES), lambda bi, h, i: (h, 0, 0, 0)),
                  pl.BlockSpec((DIFF_VDIM, 1), lambda bi, h, i: (0, 0))],
        out_specs=pl.BlockSpec((1, tq, LANES), lambda bi, h, i: (bi, i, h)),
        out_shape=jax.ShapeDtypeStruct((b, s, DIFF_WIDTH), BF16),
        scratch_shapes=[pltpu.VMEM((2 * tq, 2 * LANES), BF16),
                        pltpu.VMEM((tk, 2 * tq), F32),
                        pltpu.VMEM((tk, 2 * tq), F32),
                        pltpu.VMEM((2, 1, 2 * tq), F32),
                        pltpu.VMEM((1, 2 * tq), F32),
                        pltpu.VMEM((DIFF_VDIM + ONES_ROWS, 2 * tq), F32),
                        pltpu.VMEM((1, LANES), F32)],
        compiler_params=_params(("parallel", "parallel", "arbitrary")),
        name="diff_attn",
    )(lam.reshape(1), slopes, qkv3, qkv3, v_t, qaug, kaug, subln_g.reshape(DIFF_VDIM, 1))


def _na_bias_table(rpb, rows):
    kr = min(NA_ROWS, rows)
    q = np.arange(GRID_W)[:, None]
    j = np.arange(GRID_W)[None, :]
    cs = np.clip(q - NA_COLS // 2, 0, GRID_W - NA_COLS)
    valid = (j >= cs) & (j < cs + NA_COLS)
    dc = j - q + (NA_COLS - 1)
    sel = ((dc[:, :, None] == np.arange(2 * NA_COLS - 1)[None, None, :]) & valid[:, :, None]).astype(np.float32)
    col = jnp.einsum('hrd,qjd->hrqj', rpb.astype(F32), jnp.asarray(sel), precision=lax.Precision.HIGHEST)
    col = jnp.where(valid[None, None], col, NEG_BIG)
    cases = [col[:, NA_ROWS - 1 - c:NA_ROWS - 1 - c + kr] for c in range(kr)]
    tbl = jnp.stack(cases, axis=0).transpose(0, 1, 3, 2, 4)
    return tbl.reshape(kr, NA_HEADS // 2, 2 * GRID_W, kr * GRID_W)


def _na_kernel(q_ref, k_ref, v_ref, bias_ref, o_ref, *, rows_per_step, rows, kr):
    i = pl.program_id(2)
    lane = lax.broadcasted_iota(jnp.int32, (GRID_W, LANES), 1)
    win = kr * GRID_W

    starts, scores, probs = [], [], []
    for rr in range(rows_per_step):
        r = i * rows_per_step + rr
        rs = jnp.clip(r - kr // 2, 0, rows - kr)
        q = q_ref[0, rr * GRID_W:(rr + 1) * GRID_W, :]
        q = q * jnp.asarray(1.0 / math.sqrt(HEAD_DIM), BF16)
        zero = jnp.zeros_like(q)
        qs = jnp.concatenate([jnp.where(lane < HEAD_DIM, q, zero), jnp.where(lane >= HEAD_DIM, q, zero)], axis=0)
        start = pl.multiple_of(rs * GRID_W, GRID_W)
        starts.append(start)
        scores.append(_dot_nt(qs, k_ref[0, pl.ds(start, win), :]) + bias_ref[r - rs, 0])
    for s in scores:
        p = jnp.exp(s - jnp.max(s, axis=1, keepdims=True))
        probs.append((p.astype(BF16), jnp.sum(p, axis=1, keepdims=True)))
    for rr, (p, l) in enumerate(probs):
        o = _dot(p, v_ref[0, pl.ds(starts[rr], win), :]) / l
        out = jnp.where(lane < HEAD_DIM, o[:GRID_W], o[GRID_W:])
        o_ref[0, rr * GRID_W:(rr + 1) * GRID_W, :] = out.astype(o_ref.dtype)


def _na_attn(qkv3, bias_tbl):
    b, s, _ = qkv3.shape
    rows = s // GRID_W
    kr = min(NA_ROWS, rows)
    base = (2 * DIFF_QK_WIDTH + DIFF_WIDTH) // LANES
    pairs = NA_HEADS // 2
    rstep = NA_R
    kern = functools.partial(_na_kernel, rows_per_step=rstep, rows=rows, kr=kr)
    return pl.pallas_call(
        kern,
        grid=(b, pairs, rows // rstep),
        in_specs=[pl.BlockSpec((1, rstep * GRID_W, LANES), lambda bi, p, i: (bi, i, base + p)),
                  pl.BlockSpec((1, s, LANES), lambda bi, p, i: (bi, 0, base + pairs + p)),
                  pl.BlockSpec((1, s, LANES), lambda bi, p, i: (bi, 0, base + 2 * pairs + p)),
                  pl.BlockSpec((kr, 1, 2 * GRID_W, kr * GRID_W), lambda bi, p, i: (0, p, 0, 0))],
        out_specs=pl.BlockSpec((1, rstep * GRID_W, LANES), lambda bi, p, i: (bi, i, p)),
        out_shape=jax.ShapeDtypeStruct((b, s, NA_WIDTH), BF16),
        compiler_params=_params(("parallel", "parallel", "parallel")),
        name="na_attn",
    )(qkv3, qkv3, qkv3, bias_tbl)


def _memkv_kernel(mem_ref, g_ref, w_ref, o_ref):
    h = _rms(mem_ref[0], g_ref[...]).astype(BF16)
    o_ref[0] = _dot(h, w_ref[...]).astype(o_ref.dtype)


def _mem_kv(mem, g, w_bf16):
    b, m, d = mem.shape
    cols = w_bf16.shape[1]
    return pl.pallas_call(
        _memkv_kernel,
        grid=(b,),
        in_specs=[pl.BlockSpec((1, m, d), lambda i: (i, 0, 0)),
                  pl.BlockSpec((1, d), lambda i: (0, 0)),
                  pl.BlockSpec((d, cols), lambda i: (0, 0))],
        out_specs=pl.BlockSpec((1, m, cols), lambda i: (i, 0, 0)),
        out_shape=jax.ShapeDtypeStruct((b, m, cols), BF16),
        compiler_params=_params(("parallel",)),
        name="mem_kv",
    )(mem, g.reshape(1, d), w_bf16)


def _sublane_pack(rows, tm):
    sub = lax.broadcasted_iota(jnp.int32, (SUBLANES, tm), 0)
    out = jnp.zeros((SUBLANES, tm), rows[0].dtype)
    for k, r in enumerate(rows):
        out = jnp.where(sub == k, r, out)
    return out


def _post_kernel(x_ref, od_ref, on_ref, wout_ref, gq_ref, wmq_ref, kv_ref, wmo_ref, gf_ref,
                 wrh_ref, wrl_ref, br_ref,
                 x2_ref, hf_ref, ti_ref, tg_ref, tr_ref, cnt_ref, carry_sc, *, tm, parts):
    step = pl.program_id(0)
    rows = tm // parts

    @pl.when(step == 0)
    def _():
        carry_sc[...] = jnp.zeros(carry_sc.shape, F32)

    head_of_lane = lax.broadcasted_iota(jnp.int32, (rows, MEM_WIDTH), 1) // MEM_HEAD_DIM
    kv = kv_ref[0]

    x1s, qss = [], []
    for g in range(parts):
        rs = slice(g * rows, (g + 1) * rows)
        x1 = (x_ref[rs, :] + _dot(od_ref[rs, :], wout_ref[:DIFF_WIDTH, :])
              + _dot(on_ref[rs, :], wout_ref[DIFF_WIDTH:, :]))
        hm = _rms(x1, gq_ref[...]).astype(BF16)
        q = (_dot(hm, wmq_ref[...]) * (1.0 / math.sqrt(MEM_HEAD_DIM))).astype(BF16)
        zero = jnp.zeros_like(q)
        qss.append(jnp.concatenate([jnp.where(head_of_lane == h, q, zero) for h in range(MEM_HEADS)], axis=0))
        x1s.append(x1)

    probs = []
    for g in range(parts):
        s = _dot_nt(qss[g], kv[:, :MEM_WIDTH])
        p = jnp.exp(s - jnp.max(s, axis=1, keepdims=True))
        probs.append((p.astype(BF16), jnp.sum(p, axis=1, keepdims=True)))

    hfs = []
    for g in range(parts):
        rs = slice(g * rows, (g + 1) * rows)
        p, l = probs[g]
        o4 = _dot(p, kv[:, MEM_WIDTH:]) / l
        o = jnp.zeros((rows, MEM_WIDTH), F32)
        for h in range(MEM_HEADS):
            o = jnp.where(head_of_lane == h, o4[h * rows:(h + 1) * rows], o)
        x2 = x1s[g] + _dot(o.astype(BF16), wmo_ref[...])
        x2_ref[rs, :] = x2
        hf = _rms(x2, gf_ref[...])
        _rows_to_tiles(hf_ref.at[pl.ds(g * rows * SUBLANES, rows * SUBLANES), :], hf, rows)
        hfs.append(hf)

    e_iota = lax.broadcasted_iota(jnp.int32, (N_EXPERTS, rows), 0)
    r_i = lax.broadcasted_iota(jnp.int32, (rows, rows), 0)
    c_i = lax.broadcasted_iota(jnp.int32, (rows, rows), 1)
    earlier = (r_i < c_i).astype(BF16)
    carry = carry_sc[...]
    idx_rows, gate_rows, rank_rows = [], [], []
    for g in range(parts):
        hi = hfs[g].astype(BF16)
        lo = (hfs[g] - hi.astype(F32)).astype(BF16)
        logits = _dot_nt(wrh_ref[...], hi) + _dot_nt(wrl_ref[...], hi) + _dot_nt(wrh_ref[...], lo) + br_ref[...]
        vals, idxs, hots = [], [], []
        cur = logits
        for _ in range(TOP_K):
            mx = jnp.max(cur, axis=0, keepdims=True)
            idx = jnp.min(jnp.where(cur == mx, e_iota, N_EXPERTS), axis=0, keepdims=True)
            hot = e_iota == idx
            vals.append(mx)
            idxs.append(idx)
            hots.append(hot)
            cur = jnp.where(hot, -jnp.inf, cur)
        exps = [jnp.exp(v - vals[0]) for v in vals]
        den = exps[0] + exps[1] + exps[2] + exps[3]
        chosen = jnp.zeros((N_EXPERTS, rows), F32)
        for hot in hots:
            chosen = chosen + hot.astype(F32)
        before = _dot(chosen.astype(BF16), earlier) + carry
        carry = carry + jnp.sum(chosen, axis=1, keepdims=True)
        idx_rows.append(idxs)
        gate_rows.append([e / den for e in exps])
        rank_rows.append([jnp.sum(jnp.where(hot, before, 0.0), axis=0, keepdims=True).astype(jnp.int32)
                          for hot in hots])
    carry_sc[...] = carry
    cnt_ref[...] = carry

    def tile_rows(per_part):
        return [jnp.concatenate([per_part[g][k] for g in range(parts)], axis=1) for k in range(TOP_K)]

    ti_ref[...] = _sublane_pack(tile_rows(idx_rows), tm)
    tg_ref[...] = _sublane_pack(tile_rows(gate_rows), tm)
    tr_ref[...] = _sublane_pack(tile_rows(rank_rows), tm)


def _post_attn(x2d, o_diff, o_na, w_out, gq, w_mq, kv, w_mo, gf, wr_hi_t, wr_lo_t, b_router, seq):
    n, d = x2d.shape
    tm = POST_TM
    m_tok = kv.shape[1]
    const = lambda i: (0, 0)
    row = lambda i: (i, 0)
    col = lambda i: (0, i)
    kern = functools.partial(_post_kernel, tm=tm, parts=POST_PARTS)
    return pl.pallas_call(
        kern,
        grid=(n // tm,),
        in_specs=[pl.BlockSpec((tm, d), row),
                  pl.BlockSpec((tm, DIFF_WIDTH), row),
                  pl.BlockSpec((tm, NA_WIDTH), row),
                  pl.BlockSpec(w_out.shape, const),
                  pl.BlockSpec((1, d), const),
                  pl.BlockSpec(w_mq.shape, const),
                  pl.BlockSpec((1, m_tok, 2 * MEM_WIDTH), lambda i: ((i * tm) // seq, 0, 0)),
                  pl.BlockSpec(w_mo.shape, const),
                  pl.BlockSpec((1, d), const),
                  pl.BlockSpec(wr_hi_t.shape, const),
                  pl.BlockSpec(wr_lo_t.shape, const),
                  pl.BlockSpec((N_EXPERTS, 1), const)],
        out_specs=[pl.BlockSpec((tm, d), row),
                   pl.BlockSpec((tm * SUBLANES, LANES), row),
                   pl.BlockSpec((SUBLANES, tm), col),
                   pl.BlockSpec((SUBLANES, tm), col),
                   pl.BlockSpec((SUBLANES, tm), col),
                   pl.BlockSpec((N_EXPERTS, 1), const)],
        out_shape=[jax.ShapeDtypeStruct((n, d), F32),
                   jax.ShapeDtypeStruct((n * SUBLANES, LANES), F32),
                   jax.ShapeDtypeStruct((SUBLANES, n), jnp.int32),
                   jax.ShapeDtypeStruct((SUBLANES, n), F32),
                   jax.ShapeDtypeStruct((SUBLANES, n), jnp.int32),
                   jax.ShapeDtypeStruct((N_EXPERTS, 1), F32)],
        scratch_shapes=[pltpu.VMEM((N_EXPERTS, 1), F32)],
        compiler_params=_params(("arbitrary",)),
        name="post_attn",
    )(x2d, o_diff, o_na, w_out, gq.reshape(1, d), w_mq, kv, w_mo, gf.reshape(1, d), wr_hi_t, wr_lo_t,
      b_router.reshape(N_EXPERTS, 1))


def _row_copy(src, src_row, dst, dst_row, sem):
    return pltpu.make_async_copy(
        src.at[pl.ds(pl.multiple_of(src_row * SUBLANES, SUBLANES), SUBLANES), :],
        dst.at[pl.ds(pl.multiple_of(dst_row * SUBLANES, SUBLANES), SUBLANES), :],
        sem)


def _ffn_kernel(blk_e_ref, nact_ref, next_e_ref, slot_cur_ref, slot_next_ref, slot_prev_ref, hf_ref,
                w1_ref, b1_ref, w2_ref, b2_ref, yc_ref,
                xbuf, ybuf, w1f_sc, w2f_sc, w1b_sc, w2b_sc, wslot_sc, gsem, ssem, wsem, *, blk, d_ff, n_tok):
    i = pl.program_id(0)
    nact = nact_ref[0]
    slot = i % 2
    y_slot = i % Y_SLOTS
    y_prev = (i + Y_SLOTS - 1) % Y_SLOTS

    trash_row = n_tok * TOP_K

    def gather(slot_ref, t, to_slot):
        tok = slot_ref[t] & (n_tok - 1)
        return _row_copy(hf_ref, tok, xbuf.at[to_slot], t, gsem.at[to_slot])

    def scatter(t):
        dst = jnp.where(i == 0, trash_row + t, slot_prev_ref[t])
        return _row_copy(ybuf.at[y_prev], t, yc_ref, dst, ssem)

    def wait_all(copy_of_row):
        def drain(g, carry):
            for u in range(WAITS_PER_TRIP):
                copy_of_row(g * WAITS_PER_TRIP + u).wait()
            return carry
        lax.fori_loop(0, blk // WAITS_PER_TRIP, drain, 0)

    @pl.when(i == 0)
    def _():
        ybuf[...] = jnp.zeros(ybuf.shape, F32)
        for t in range(blk):
            gather(slot_cur_ref, t, 0).start(priority=t % 2)

    def weight_copies(e, s):
        return (pltpu.make_async_copy(w1_ref.at[e], w1f_sc.at[s], wsem.at[s]),
                pltpu.make_async_copy(w2_ref.at[e], w2f_sc.at[s], wsem.at[s]))

    @pl.when(i == 0)
    def _():
        wslot_sc[0] = 0
        for cp in weight_copies(blk_e_ref[0], 0):
            cp.start(priority=1)

    @pl.when((i < nact) & ((i == 0) | (blk_e_ref[i] != blk_e_ref[jnp.maximum(i - 1, 0)])))
    def _():
        e = blk_e_ref[i]
        s = wslot_sc[0]
        for cp in weight_copies(e, s):
            cp.wait()
        nxt = next_e_ref[e]

        @pl.when(nxt >= 0)
        def _():
            for cp in weight_copies(nxt, 1 - s):
                cp.start(priority=1)

        rows = 256
        for c in range(w1b_sc.shape[0] // rows):
            w1b_sc[c * rows:(c + 1) * rows, :] = w1f_sc[s, c * rows:(c + 1) * rows, :].astype(BF16)
        for c in range(w2b_sc.shape[0] // rows):
            w2b_sc[c * rows:(c + 1) * rows, :] = w2f_sc[s, c * rows:(c + 1) * rows, :].astype(BF16)
        wslot_sc[0] = 1 - s

    @pl.when(i <= nact)
    def _():
        wait_all(lambda t: gather(slot_cur_ref, t, slot))

    @pl.when((i >= 1) & (i <= nact))
    def _():
        wait_all(scatter)

    @pl.when(i < nact)
    def _():
        for t in range(blk):
            gather(slot_next_ref, t, 1 - slot).start(priority=t % 2)
            scatter(t).start(priority=(t + 1) % 2)
        x = _tiles_to_rows(xbuf, blk, lead=slot).astype(BF16)
        gu = _dot(x, w1b_sc[...]) + b1_ref[0]
        gate = jnp.minimum(gu[:, :d_ff], SWIGLU_LIMIT)
        lin = jnp.clip(gu[:, d_ff:], -SWIGLU_LIMIT, SWIGLU_LIMIT)
        act = gate * jax.nn.sigmoid(SWIGLU_ALPHA * gate) * (lin + 1.0)
        y = _dot(act.astype(BF16), w2b_sc[...]) + b2_ref[0]
        _rows_to_tiles(ybuf.at[y_slot], y, blk)

    @pl.when(i == nact)
    def _():
        def issue(t, carry):
            scatter(t).start()
            return carry
        lax.fori_loop(0, blk, issue, 0)
        wait_all(scatter)


def _expert_ffn(blk_e, nact, next_e, row_slot, hf_tiles, w1, b1, w2, b2, n_tok):
    blk = FFN_BLK
    nblk = row_slot.shape[0] // blk
    n_slots = n_tok * TOP_K
    assert n_tok & (n_tok - 1) == 0
    _, d, two_ff = w1.shape
    d_ff = two_ff // 2
    kern = functools.partial(_ffn_kernel, blk=blk, d_ff=d_ff, n_tok=n_tok)
    smem_blk = lambda f: pl.BlockSpec((blk,), f, memory_space=pltpu.SMEM)
    grid_spec = pltpu.PrefetchScalarGridSpec(
        num_scalar_prefetch=3,
        grid=(nblk,),
        in_specs=[smem_blk(lambda i, be, na, ne: (i,)),
                  smem_blk(lambda i, be, na, ne: (jnp.minimum(i + 1, nblk - 1),)),
                  smem_blk(lambda i, be, na, ne: (jnp.maximum(i - 1, 0),)),
                  pl.BlockSpec(memory_space=pl.ANY),
                  pl.BlockSpec(memory_space=pl.ANY),
                  pl.BlockSpec((1, 1, two_ff), lambda i, be, na, ne: (be[i], 0, 0)),
                  pl.BlockSpec(memory_space=pl.ANY),
                  pl.BlockSpec((1, 1, d), lambda i, be, na, ne: (be[i], 0, 0))],
        out_specs=pl.BlockSpec(memory_space=pl.ANY),
        scratch_shapes=[pltpu.VMEM((2, blk * SUBLANES, LANES), F32),
                        pltpu.VMEM((Y_SLOTS, blk * SUBLANES, LANES), F32),
                        pltpu.VMEM((2, d, two_ff), F32),
                        pltpu.VMEM((2, d_ff, d), F32),
                        pltpu.VMEM((d, two_ff), BF16),
                        pltpu.VMEM((d_ff, d), BF16),
                        pltpu.SMEM((1,), jnp.int32),
                        pltpu.SemaphoreType.DMA((2,)),
                        pltpu.SemaphoreType.DMA(()),
                        pltpu.SemaphoreType.DMA((2,))],
    )
    return pl.pallas_call(
        kern,
        grid_spec=grid_spec,
        out_shape=jax.ShapeDtypeStruct(((n_slots + blk) * SUBLANES, LANES), F32),
        compiler_params=_params(("arbitrary",)),
        name="expert_ffn",
    )(blk_e, nact, next_e, row_slot, row_slot, row_slot, hf_tiles, w1, b1.reshape(N_EXPERTS, 1, two_ff), w2,
      b2.reshape(N_EXPERTS, 1, d))


def _combine_kernel(x2_ref, tg_ref, gfin_ref, *rest, tm, final_norm):
    y_refs, o_ref = rest[:TOP_K], rest[TOP_K]
    gates = tg_ref[...]
    pieces = []
    for j in range(SUBLANES):
        acc = x2_ref[:, j * LANES:(j + 1) * LANES]
        for k in range(TOP_K):
            acc = acc + gates[:, k:k + 1] * y_refs[k][pl.ds(j, tm, stride=SUBLANES), :]
        pieces.append(acc)
    out = jnp.concatenate(pieces, axis=1)
    if final_norm:
        out = _rms(out, gfin_ref[...])
    o_ref[...] = out


def _combine(x2, gates, g_final, y_slots, final_norm):
    n, d = x2.shape
    tm = COMB_TM
    kern = functools.partial(_combine_kernel, tm=tm, final_norm=final_norm)
    tiles = n // tm

    def choice_spec(k):
        return pl.BlockSpec((tm * SUBLANES, LANES), lambda i: (k * tiles + i, 0))

    return pl.pallas_call(
        kern,
        grid=(tiles,),
        in_specs=[pl.BlockSpec((tm, d), lambda i: (i, 0)),
                  pl.BlockSpec((tm, SUBLANES), lambda i: (i, 0)),
                  pl.BlockSpec((1, d), lambda i: (0, 0))] + [choice_spec(k) for k in range(TOP_K)],
        out_specs=pl.BlockSpec((tm, d), lambda i: (i, 0)),
        out_shape=jax.ShapeDtypeStruct((n, d), F32),
        compiler_params=_params(("parallel",)),
        name="combine",
    )(x2, gates, g_final.reshape(1, d), *([y_slots] * TOP_K))


def _invert_kernel(pad_lo_ref, pad_hi_ref, dest_ref, slot_ref, *, chunk, n_tok, blk):
    n_assign = n_tok * TOP_K
    c = pl.program_id(0)

    @pl.when(c == 0)
    def _():
        def expert(e, carry):
            def pad_row(r, carry2):
                slot_ref[r] = n_assign + (r & (blk - 1))
                return carry2
            lax.fori_loop(pad_lo_ref[e], pad_hi_ref[e], pad_row, 0)
            return carry
        lax.fori_loop(0, N_EXPERTS, expert, 0)

    def body(g, carry):
        a0 = g * INVERT_UNROLL
        tok0 = lax.shift_right_logical(c * chunk + a0, TOP_K.bit_length() - 1)
        for u in range(INVERT_UNROLL):
            slot_ref[dest_ref[a0 + u]] = tok0 + u // TOP_K + (u % TOP_K) * n_tok
        return carry
    lax.fori_loop(0, chunk // INVERT_UNROLL, body, 0)


def _invert_routing(dest, pad_lo, pad_hi, p_rows):
    n_assign = dest.shape[0]
    chunk = INVERT_CHUNK
    blk = FFN_BLK
    assert n_assign % chunk == 0 and chunk % INVERT_UNROLL == 0 and blk & (blk - 1) == 0
    assert INVERT_UNROLL % TOP_K == 0 and TOP_K & (TOP_K - 1) == 0
    kern = functools.partial(_invert_kernel, chunk=chunk, n_tok=n_assign // TOP_K, blk=blk)
    whole_smem = pl.BlockSpec(memory_space=pltpu.SMEM)
    grid_spec = pltpu.PrefetchScalarGridSpec(
        num_scalar_prefetch=2,
        grid=(n_assign // chunk,),
        in_specs=[pl.BlockSpec((chunk,), lambda c, lo, hi: (c,), memory_space=pltpu.SMEM)],
        out_specs=whole_smem,
    )
    return pl.pallas_call(
        kern,
        grid_spec=grid_spec,
        out_shape=jax.ShapeDtypeStruct((p_rows,), jnp.int32),
        compiler_params=_params(("arbitrary",)),
        name="invert_routing",
    )(pad_lo, pad_hi, dest)


def _routing_tables(counts_f32, top_i, rank, n_tok):
    blk = FFN_BLK
    n_assign = n_tok * TOP_K
    counts = counts_f32.astype(jnp.int32)
    pcounts = ((counts + blk - 1) // blk) * blk
    pends = jnp.cumsum(pcounts)
    pstarts = pends - pcounts
    dest = (pstarts[top_i] + rank).reshape(-1)
    p_rows = n_assign + N_EXPERTS * blk
    nblk = p_rows // blk
    pad_lo = (pstarts + counts).astype(jnp.int32)
    pad_hi = pends.at[N_EXPERTS - 1].set(p_rows).astype(jnp.int32)
    row_slot = _invert_routing(dest.astype(jnp.int32), pad_lo, pad_hi, p_rows)
    blk_start = jnp.arange(nblk, dtype=jnp.int32) * blk
    blk_e = jnp.minimum(jnp.sum(blk_start[:, None] >= pends[None, :], axis=1), N_EXPERTS - 1).astype(jnp.int32)
    nact = (pends[-1] // blk).astype(jnp.int32).reshape(1)
    e_ids = jnp.arange(N_EXPERTS, dtype=jnp.int32)
    later_active = jnp.where((counts[None, :] > 0) & (e_ids[None, :] > e_ids[:, None]), e_ids[None, :], N_EXPERTS)
    next_e = jnp.min(later_active, axis=1)
    next_e = jnp.where(next_e == N_EXPERTS, -1, next_e).astype(jnp.int32)
    return row_slot, blk_e, nact, next_e


def _lambda_init(layer):
    return 0.8 - 0.6 * math.exp(-0.3 * layer)


def kernel(x, mem, norm_mix_g, w_in, lambda_q1, lambda_k1, lambda_q2, lambda_k2, subln_g, rpb, w_out,
           norm_mem_q_g, norm_mem_kv_g, w_mq, w_mkv, w_mo, norm_ffn_g, w_router, b_router, w1, b1, w2, b2,
           norm_final_g):
    b, s, d = x.shape
    n = b * s
    depth = w_in.shape[0]
    assert d == SUBLANES * LANES and s % GRID_W == 0
    slopes = jnp.asarray([2.0 ** (-8.0 * (i + 1) / DIFF_HEADS) for i in range(DIFF_HEADS)], F32)
    x2d = x.reshape(n, d)
    for l in range(depth):
        lam_init = _lambda_init(l)
        lam = (jnp.exp(jnp.sum(lambda_q1[l] * lambda_k1[l])) - jnp.exp(jnp.sum(lambda_q2[l] * lambda_k2[l]))
               + lam_init).astype(F32)
        qkv = _qkv_proj(x2d, norm_mix_g[l], w_in[l].astype(BF16))
        qkv3 = qkv.reshape(b, s, qkv.shape[1])
        o_diff = _diff_attn(qkv3, lam, slopes, subln_g[l], lam_init).reshape(n, DIFF_WIDTH)
        o_na = _na_attn(qkv3, _na_bias_table(rpb[l], s // GRID_W)).reshape(n, NA_WIDTH)
        kv = _mem_kv(mem, norm_mem_kv_g[l], w_mkv[l].astype(BF16))
        wr_t = w_router[l].T
        wr_hi_t = wr_t.astype(BF16)
        wr_lo_t = (wr_t - wr_hi_t.astype(F32)).astype(BF16)
        x2, hf_tiles, ti, tg, tr, counts = _post_attn(
            x2d, o_diff, o_na, w_out[l].astype(BF16), norm_mem_q_g[l], w_mq[l].astype(BF16), kv,
            w_mo[l].astype(BF16), norm_ffn_g[l], wr_hi_t, wr_lo_t, b_router[l], s)
        row_slot, blk_e, nact, next_e = _routing_tables(counts[:, 0], ti[:TOP_K].T, tr[:TOP_K].T, n)
        y_slots = _expert_ffn(blk_e, nact, next_e, row_slot, hf_tiles, w1[l], b1[l], w2[l], b2[l], n)
        x2d = _combine(x2, tg.T, norm_final_g, y_slots, final_norm=(l == depth - 1))
    return x2d.reshape(b, s, d)
```

```python
import functools
import math

import jax
import jax.numpy as jnp
import numpy as np
from jax import lax
from jax.experimental import pallas as pl
from jax.experimental.pallas import tpu as pltpu

HEAD_DIM = 64
DIFF_HEADS = 4
DIFF_VDIM = 2 * HEAD_DIM
DIFF_QK_WIDTH = DIFF_HEADS * 2 * HEAD_DIM
DIFF_WIDTH = DIFF_HEADS * DIFF_VDIM
NA_HEADS = 8
NA_WIDTH = NA_HEADS * HEAD_DIM
GRID_W = 64
NA_ROWS = 8
NA_COLS = 16
MEM_HEADS = 4
MEM_HEAD_DIM = 64
MEM_WIDTH = MEM_HEADS * MEM_HEAD_DIM
N_EXPERTS = 32
TOP_K = 4
SWIGLU_LIMIT = 7.0
SWIGLU_ALPHA = 1.702
RMS_EPS = 1e-5

LANES = 128
SUBLANES = 8
NEG_BIG = -1e30

F32 = jnp.float32
BF16 = jnp.bfloat16

QKV_TM = 512
DIFF_TQ = 512
DIFF_TK = 512
NA_R = 16
POST_TM = 512
POST_PARTS = 2
FFN_BLK = 256
COMB_TM = 256
WAITS_PER_TRIP = 16
INVERT_CHUNK = 8192
INVERT_UNROLL = 16
Y_SLOTS = 3
VMEM_LIMIT = 56 * 1024 * 1024


def _params(sem):
    return pltpu.CompilerParams(dimension_semantics=sem, vmem_limit_bytes=VMEM_LIMIT)


def _rms(x, g):
    return x * lax.rsqrt(jnp.mean(x * x, axis=-1, keepdims=True) + RMS_EPS) * g


def _dot(a, b):
    return jnp.dot(a, b, preferred_element_type=F32)


def _dot_nt(a, b):
    return lax.dot_general(a, b, (((1,), (1,)), ((), ())), preferred_element_type=F32)


def _rows_to_tiles(ref, val, rows):
    for j in range(SUBLANES):
        ref[pl.ds(j, rows, stride=SUBLANES), :] = val[:, j * LANES:(j + 1) * LANES]


def _tiles_to_rows(ref, rows, lead=None):
    if lead is None:
        parts = [ref[pl.ds(j, rows, stride=SUBLANES), :] for j in range(SUBLANES)]
    else:
        parts = [ref[lead, pl.ds(j, rows, stride=SUBLANES), :] for j in range(SUBLANES)]
    return jnp.concatenate(parts, axis=1)


def _qkv_kernel(x_ref, g_ref, w_ref, o_ref):
    h = _rms(x_ref[...], g_ref[...]).astype(BF16)
    cols = o_ref.shape[1]
    step = 512
    for c in range(cols // step):
        o_ref[:, c * step:(c + 1) * step] = _dot(h, w_ref[:, c * step:(c + 1) * step]).astype(o_ref.dtype)


def _qkv_proj(x2d, g, w_bf16):
    n, d = x2d.shape
    cols = w_bf16.shape[1]
    return pl.pallas_call(
        _qkv_kernel,
        grid=(n // QKV_TM,),
        in_specs=[pl.BlockSpec((QKV_TM, d), lambda i: (i, 0)),
                  pl.BlockSpec((1, d), lambda i: (0, 0)),
                  pl.BlockSpec((d, cols), lambda i: (0, 0))],
        out_specs=pl.BlockSpec((QKV_TM, cols), lambda i: (i, 0)),
        out_shape=jax.ShapeDtypeStruct((n, cols), BF16),
        compiler_params=_params(("parallel",)),
        name="qkv_proj",
    )(x2d, g.reshape(1, d), w_bf16)


ALIBI_AUG_LEFT, ALIBI_AUG_RIGHT, ALIBI_AUG_NONE = 0, 1, 2
ONES_ROWS = 16
EXP_UNDERFLOW = 105.0
NORM_BOUND_SLACK = 1.02


def _bf16_exact_split(n):
    lo_bits = max(0, int(n.max()).bit_length() - 8)
    lo = n % (1 << lo_bits)
    return n - lo, lo


def _alibi_aug_tables(tq, tk):
    ii_hi, ii_lo = _bf16_exact_split(np.arange(tq))
    jj_hi, jj_lo = _bf16_exact_split(np.arange(tk))
    qa = np.zeros((DIFF_HEADS, tq, LANES), np.float32)
    ka = np.zeros((DIFF_HEADS, 3, tk, LANES), np.float32)
    for h in range(DIFF_HEADS):
        sl = 2.0 ** (-8.0 * (h + 1) / DIFF_HEADS)
        qa[h, :, 0], qa[h, :, 1], qa[h, :, 2], qa[h, :, 3] = -sl * ii_hi, -sl * ii_lo, 1.0, 1.0
        qa[h, :, 4], qa[h, :, 5], qa[h, :, 6], qa[h, :, 7] = sl * ii_hi, sl * ii_lo, 1.0, 1.0
        ka[h, ALIBI_AUG_LEFT, :, 0], ka[h, ALIBI_AUG_LEFT, :, 1] = 1.0, 1.0
        ka[h, ALIBI_AUG_LEFT, :, 2], ka[h, ALIBI_AUG_LEFT, :, 3] = sl * jj_hi, sl * jj_lo
        ka[h, ALIBI_AUG_RIGHT, :, 4], ka[h, ALIBI_AUG_RIGHT, :, 5] = 1.0, 1.0
        ka[h, ALIBI_AUG_RIGHT, :, 6], ka[h, ALIBI_AUG_RIGHT, :, 7] = -sl * jj_hi, -sl * jj_lo
    return jnp.asarray(qa, BF16), jnp.asarray(ka, BF16)


def _diff_kernel(lam_ref, slope_ref, q_ref, k_ref, vt_ref, qaug_ref, kaug_ref, g_ref, o_ref,
                 qa_sc, s0_sc, s1_sc, mb_sc, m_sc, acc_sc, ksq_sc, *, tq, tk, seq, out_scale):
    h = pl.program_id(1)
    i = pl.program_id(2)
    slope = slope_ref[h]
    lam = lam_ref[0]
    nk = seq // tk
    i0 = i * tq
    jd = i0 // tk

    sel_r = lax.broadcasted_iota(jnp.int32, (LANES, LANES), 0) // HEAD_DIM
    sel_c = lax.broadcasted_iota(jnp.int32, (LANES, LANES), 1)
    sel = (sel_r == sel_c).astype(BF16)

    def max_sq_norm(x):
        xf = x.astype(F32)
        return jnp.max(_dot((xf * xf).astype(BF16), sel), axis=0, keepdims=True)

    @pl.when(i == 0)
    def _():
        ksq_sc[...] = max_sq_norm(k_ref[0])

    q = q_ref[0] * jnp.asarray(1.0 / math.sqrt(HEAD_DIM), BF16)
    lane = lax.broadcasted_iota(jnp.int32, (tq, LANES), 1)
    zero = jnp.zeros_like(q)
    qa_sc[:tq, :LANES] = jnp.where(lane < HEAD_DIM, q, zero)
    qa_sc[tq:, :LANES] = jnp.where(lane >= HEAD_DIM, q, zero)
    qa_sc[:tq, LANES:] = qaug_ref[0]
    qa_sc[tq:, LANES:] = qaug_ref[0]
    m_sc[...] = jnp.full(m_sc.shape, NEG_BIG, F32)
    acc_sc[...] = jnp.zeros(acc_sc.shape, F32)

    def scores(j, side):
        start = pl.multiple_of(j * tk, tk)
        kba = jnp.concatenate([k_ref[0, pl.ds(start, tk), :], kaug_ref[0, side]], axis=1)
        return _dot_nt(kba, qa_sc[...])

    s_bufs = (s0_sc, s1_sc)

    def stash(slot, st):
        s_bufs[slot][...] = st
        mb_sc[slot] = jnp.max(st, axis=0, keepdims=True)

    def consume(slot, j, c):
        m_prev = m_sc[...]
        m_new = jnp.maximum(m_prev, mb_sc[slot] + c)
        alpha = jnp.exp(m_prev - m_new)
        p_t = jnp.exp(s_bufs[slot][...] - (m_new - c)).astype(BF16)
        start = pl.multiple_of(j * tk, tk)
        vta = jnp.concatenate([vt_ref[0, :, pl.ds(start, tk)], jnp.ones((ONES_ROWS, tk), BF16)], axis=0)
        acc_sc[...] = alpha * acc_sc[...] + _dot(vta, p_t)
        m_sc[...] = m_new

    def block_const(j):
        return -slope * jnp.abs(i0 - j * tk).astype(F32)

    rel_t = (lax.broadcasted_iota(jnp.int32, (tk, tq), 0) - lax.broadcasted_iota(jnp.int32, (tk, tq), 1)).astype(F32)
    bias_t = -slope * jnp.abs(rel_t + (jd * tk - i0).astype(F32))
    stash(0, scores(jd, ALIBI_AUG_NONE) + jnp.concatenate([bias_t, bias_t], axis=1))

    qk_bound = jnp.sqrt(jnp.max(max_sq_norm(q) * ksq_sc[...])) * NORM_BOUND_SLACK
    reach = (qk_bound - jnp.min(mb_sc[0]) + EXP_UNDERFLOW) / slope
    reach = jnp.minimum(reach, 2.0 * seq)
    j_lo = jnp.clip(jnp.floor((i0 - tk + 1 - reach) / tk).astype(jnp.int32) + 1, 0, jd)
    j_hi = jnp.clip(jnp.ceil((reach + i0 + tq - 1) / tk).astype(jnp.int32) - 1, jd, nk - 1)
    n_blocks = j_hi - j_lo + 1

    def block_of(t):
        return j_lo + jnp.where(t - 1 < jd - j_lo, t - 1, t)

    def produce(t, slot):
        j = block_of(t)
        stash(slot, scores(j, jnp.where(j < jd, ALIBI_AUG_LEFT, ALIBI_AUG_RIGHT)))

    def consume_step(t, slot):
        j = jnp.where(t == 0, jd, block_of(t))
        consume(slot, j, jnp.where(t == 0, 0.0, block_const(j)))

    def body(u, carry):
        t = 2 * u
        produce(t + 1, 1)
        consume_step(t, 0)
        produce(t + 2, 0)
        consume_step(t + 1, 1)
        return carry

    def body4(u, carry):
        body(2 * u, carry)
        body(2 * u + 1, carry)
        return carry

    quads = (n_blocks - 1) // 4
    lax.fori_loop(0, quads, body4, 0)
    lax.fori_loop(2 * quads, (n_blocks - 1) // 2, body, 0)

    @pl.when(n_blocks % 2 == 0)
    def _():
        produce(n_blocks - 1, 1)
        consume_step(n_blocks - 2, 0)
        consume_step(n_blocks - 1, 1)

    @pl.when(n_blocks % 2 == 1)
    def _():
        consume_step(n_blocks - 1, 0)

    acc = acc_sc[...]
    o = acc[:DIFF_VDIM] * (1.0 / acc[DIFF_VDIM:DIFF_VDIM + 1])
    d = o[:, :tq] - lam * o[:, tq:]
    y = d * lax.rsqrt(jnp.mean(d * d, axis=0, keepdims=True) + RMS_EPS) * g_ref[...] * out_scale
    o_ref[0] = y.T.astype(o_ref.dtype)


def _diff_attn(qkv3, lam, slopes, subln_g, lam_init):
    b, s, _ = qkv3.shape
    tq, tk = DIFF_TQ, DIFF_TK
    assert tk % tq == 0 and s % (2 * tk) == 0
    kcol = DIFF_QK_WIDTH // LANES
    v_t = qkv3[:, :, 2 * DIFF_QK_WIDTH:2 * DIFF_QK_WIDTH + DIFF_WIDTH].transpose(0, 2, 1)
    qaug, kaug = _alibi_aug_tables(tq, tk)
    kern = functools.partial(_diff_kernel, tq=tq, tk=tk, seq=s, out_scale=1.0 - lam_init)
    return pl.pallas_call(
        kern,
        grid=(b, DIFF_HEADS, s // tq),
        in_specs=[pl.BlockSpec(memory_space=pltpu.SMEM),
                  pl.BlockSpec(memory_space=pltpu.SMEM),
                  pl.BlockSpec((1, tq, LANES), lambda bi, h, i: (bi, i, h)),
                  pl.BlockSpec((1, s, LANES), lambda bi, h, i: (bi, 0, kcol + h)),
                  pl.BlockSpec((1, DIFF_VDIM, s), lambda bi, h, i: (bi, h, 0)),
                  pl.BlockSpec((1, tq, LANES), lambda bi, h, i: (h, 0, 0)),
                  pl.BlockSpec((1, 3, tk, LANES), lambda bi, h, i: (h, 0, 0, 0)),
                  pl.BlockSpec((DIFF_VDIM, 1), lambda bi, h, i: (0, 0))],
        out_specs=pl.BlockSpec((1, tq, LANES), lambda bi, h, i: (bi, i, h)),
        out_shape=jax.ShapeDtypeStruct((b, s, DIFF_WIDTH), BF16),
        scratch_shapes=[pltpu.VMEM((2 * tq, 2 * LANES), BF16),
                        pltpu.VMEM((tk, 2 * tq), F32),
                        pltpu.VMEM((tk, 2 * tq), F32),
                        pltpu.VMEM((2, 1, 2 * tq), F32),
                        pltpu.VMEM((1, 2 * tq), F32),
                        pltpu.VMEM((DIFF_VDIM + ONES_ROWS, 2 * tq), F32),
                        pltpu.VMEM((1, LANES), F32)],
        compiler_params=_params(("parallel", "parallel", "arbitrary")),
        name="diff_attn",
    )(lam.reshape(1), slopes, qkv3, qkv3, v_t, qaug, kaug, subln_g.reshape(DIFF_VDIM, 1))


def _na_bias_table(rpb, rows):
    kr = min(NA_ROWS, rows)
    q = np.arange(GRID_W)[:, None]
    j = np.arange(GRID_W)[None, :]
    cs = np.clip(q - NA_COLS // 2, 0, GRID_W - NA_COLS)
    valid = (j >= cs) & (j < cs + NA_COLS)
    dc = j - q + (NA_COLS - 1)
    sel = ((dc[:, :, None] == np.arange(2 * NA_COLS - 1)[None, None, :]) & valid[:, :, None]).astype(np.float32)
    col = jnp.einsum('hrd,qjd->hqrj', rpb.astype(F32), jnp.asarray(sel), precision=lax.Precision.HIGHEST)
    col = jnp.where(valid[None, :, None, :], col, NEG_BIG)
    cases = [col[:, :, NA_ROWS - 1 - c:NA_ROWS - 1 - c + kr] for c in range(kr)]
    return jnp.stack(cases, axis=0).reshape(kr, NA_HEADS // 2, 2 * GRID_W, kr * GRID_W)


def _na_kernel(q_ref, k_ref, v_ref, bias_ref, o_ref, *, rows_per_step, rows, kr):
    i = pl.program_id(2)
    lane = lax.broadcasted_iota(jnp.int32, (GRID_W, LANES), 1)
    win = kr * GRID_W

    starts, scores, probs = [], [], []
    for rr in range(rows_per_step):
        r = i * rows_per_step + rr
        rs = jnp.clip(r - kr // 2, 0, rows - kr)
        q = q_ref[0, rr * GRID_W:(rr + 1) * GRID_W, :]
        q = q * jnp.asarray(1.0 / math.sqrt(HEAD_DIM), BF16)
        zero = jnp.zeros_like(q)
        qs = jnp.concatenate([jnp.where(lane < HEAD_DIM, q, zero), jnp.where(lane >= HEAD_DIM, q, zero)], axis=0)
        start = pl.multiple_of(rs * GRID_W, GRID_W)
        starts.append(start)
        scores.append(_dot_nt(qs, k_ref[0, pl.ds(start, win), :]) + bias_ref[r - rs, 0])
    for s in scores:
        p = jnp.exp(s - jnp.max(s, axis=1, keepdims=True))
        probs.append((p.astype(BF16), jnp.sum(p, axis=1, keepdims=True)))
    for rr, (p, l) in enumerate(probs):
        o = _dot(p, v_ref[0, pl.ds(starts[rr], win), :]) / l
        out = jnp.where(lane < HEAD_DIM, o[:GRID_W], o[GRID_W:])
        o_ref[0, rr * GRID_W:(rr + 1) * GRID_W, :] = out.astype(o_ref.dtype)


def _na_attn(qkv3, bias_tbl):
    b, s, _ = qkv3.shape
    rows = s // GRID_W
    kr = min(NA_ROWS, rows)
    base = (2 * DIFF_QK_WIDTH + DIFF_WIDTH) // LANES
    pairs = NA_HEADS // 2
    rstep = NA_R
    kern = functools.partial(_na_kernel, rows_per_step=rstep, rows=rows, kr=kr)
    return pl.pallas_call(
        kern,
        grid=(b, pairs, rows // rstep),
        in_specs=[pl.BlockSpec((1, rstep * GRID_W, LANES), lambda bi, p, i: (bi, i, base + p)),
                  pl.BlockSpec((1, s, LANES), lambda bi, p, i: (bi, 0, base + pairs + p)),
                  pl.BlockSpec((1, s, LANES), lambda bi, p, i: (bi, 0, base + 2 * pairs + p)),
                  pl.BlockSpec((kr, 1, 2 * GRID_W, kr * GRID_W), lambda bi, p, i: (0, p, 0, 0))],
        out_specs=pl.BlockSpec((1, rstep * GRID_W, LANES), lambda bi, p, i: (bi, i, p)),
        out_shape=jax.ShapeDtypeStruct((b, s, NA_WIDTH), BF16),
        compiler_params=_params(("parallel", "parallel", "parallel")),
        name="na_attn",
    )(qkv3, qkv3, qkv3, bias_tbl)


def _memkv_kernel(mem_ref, g_ref, w_ref, o_ref):
    h = _rms(mem_ref[0], g_ref[...]).astype(BF16)
    o_ref[0] = _dot(h, w_ref[...]).astype(o_ref.dtype)


def _mem_kv(mem, g, w_bf16):
    b, m, d = mem.shape
    cols = w_bf16.shape[1]
    return pl.pallas_call(
        _memkv_kernel,
        grid=(b,),
        in_specs=[pl.BlockSpec((1, m, d), lambda i: (i, 0, 0)),
                  pl.BlockSpec((1, d), lambda i: (0, 0)),
                  pl.BlockSpec((d, cols), lambda i: (0, 0))],
        out_specs=pl.BlockSpec((1, m, cols), lambda i: (i, 0, 0)),
        out_shape=jax.ShapeDtypeStruct((b, m, cols), BF16),
        compiler_params=_params(("parallel",)),
        name="mem_kv",
    )(mem, g.reshape(1, d), w_bf16)


def _sublane_pack(rows, tm):
    sub = lax.broadcasted_iota(jnp.int32, (SUBLANES, tm), 0)
    out = jnp.zeros((SUBLANES, tm), rows[0].dtype)
    for k, r in enumerate(rows):
        out = jnp.where(sub == k, r, out)
    return out


def _post_kernel(x_ref, od_ref, on_ref, wout_ref, gq_ref, wmq_ref, kv_ref, wmo_ref, gf_ref,
                 wrh_ref, wrl_ref, br_ref,
                 x2_ref, hf_ref, ti_ref, tg_ref, tr_ref, cnt_ref, carry_sc, *, tm, parts):
    step = pl.program_id(0)
    rows = tm // parts

    @pl.when(step == 0)
    def _():
        carry_sc[...] = jnp.zeros(carry_sc.shape, F32)

    head_of_lane = lax.broadcasted_iota(jnp.int32, (rows, MEM_WIDTH), 1) // MEM_HEAD_DIM
    kv = kv_ref[0]

    x1s, qss = [], []
    for g in range(parts):
        rs = slice(g * rows, (g + 1) * rows)
        x1 = (x_ref[rs, :] + _dot(od_ref[rs, :], wout_ref[:DIFF_WIDTH, :])
              + _dot(on_ref[rs, :], wout_ref[DIFF_WIDTH:, :]))
        hm = _rms(x1, gq_ref[...]).astype(BF16)
        q = (_dot(hm, wmq_ref[...]) * (1.0 / math.sqrt(MEM_HEAD_DIM))).astype(BF16)
        zero = jnp.zeros_like(q)
        qss.append(jnp.concatenate([jnp.where(head_of_lane == h, q, zero) for h in range(MEM_HEADS)], axis=0))
        x1s.append(x1)

    probs = []
    for g in range(parts):
        s = _dot_nt(qss[g], kv[:, :MEM_WIDTH])
        p = jnp.exp(s - jnp.max(s, axis=1, keepdims=True))
        probs.append((p.astype(BF16), jnp.sum(p, axis=1, keepdims=True)))

    hfs = []
    for g in range(parts):
        rs = slice(g * rows, (g + 1) * rows)
        p, l = probs[g]
        o4 = _dot(p, kv[:, MEM_WIDTH:]) / l
        o = jnp.zeros((rows, MEM_WIDTH), F32)
        for h in range(MEM_HEADS):
            o = jnp.where(head_of_lane == h, o4[h * rows:(h + 1) * rows], o)
        x2 = x1s[g] + _dot(o.astype(BF16), wmo_ref[...])
        x2_ref[rs, :] = x2
        hf = _rms(x2, gf_ref[...])
        _rows_to_tiles(hf_ref.at[pl.ds(g * rows * SUBLANES, rows * SUBLANES), :], hf, rows)
        hfs.append(hf)

    e_iota = lax.broadcasted_iota(jnp.int32, (N_EXPERTS, rows), 0)
    r_i = lax.broadcasted_iota(jnp.int32, (rows, rows), 0)
    c_i = lax.broadcasted_iota(jnp.int32, (rows, rows), 1)
    earlier = (r_i < c_i).astype(BF16)
    carry = carry_sc[...]
    idx_rows, gate_rows, rank_rows = [], [], []
    for g in range(parts):
        hi = hfs[g].astype(BF16)
        lo = (hfs[g] - hi.astype(F32)).astype(BF16)
        logits = _dot_nt(wrh_ref[...], hi) + _dot_nt(wrl_ref[...], hi) + _dot_nt(wrh_ref[...], lo) + br_ref[...]
        vals, idxs, hots = [], [], []
        cur = logits
        for _ in range(TOP_K):
            mx = jnp.max(cur, axis=0, keepdims=True)
            idx = jnp.min(jnp.where(cur == mx, e_iota, N_EXPERTS), axis=0, keepdims=True)
            hot = e_iota == idx
            vals.append(mx)
            idxs.append(idx)
            hots.append(hot)
            cur = jnp.where(hot, -jnp.inf, cur)
        exps = [jnp.exp(v - vals[0]) for v in vals]
        den = exps[0] + exps[1] + exps[2] + exps[3]
        chosen = jnp.zeros((N_EXPERTS, rows), F32)
        for hot in hots:
            chosen = chosen + hot.astype(F32)
        before = _dot(chosen.astype(BF16), earlier) + carry
        carry = carry + jnp.sum(chosen, axis=1, keepdims=True)
        idx_rows.append(idxs)
        gate_rows.append([e / den for e in exps])
        rank_rows.append([jnp.sum(jnp.where(hot, before, 0.0), axis=0, keepdims=True).astype(jnp.int32)
                          for hot in hots])
    carry_sc[...] = carry
    cnt_ref[...] = carry

    def tile_rows(per_part):
        return [jnp.concatenate([per_part[g][k] for g in range(parts)], axis=1) for k in range(TOP_K)]

    ti_ref[...] = _sublane_pack(tile_rows(idx_rows), tm)
    tg_ref[...] = _sublane_pack(tile_rows(gate_rows), tm)
    tr_ref[...] = _sublane_pack(tile_rows(rank_rows), tm)


def _post_attn(x2d, o_diff, o_na, w_out, gq, w_mq, kv, w_mo, gf, wr_hi_t, wr_lo_t, b_router, seq):
    n, d = x2d.shape
    tm = POST_TM
    m_tok = kv.shape[1]
    const = lambda i: (0, 0)
    row = lambda i: (i, 0)
    col = lambda i: (0, i)
    kern = functools.partial(_post_kernel, tm=tm, parts=POST_PARTS)
    return pl.pallas_call(
        kern,
        grid=(n // tm,),
        in_specs=[pl.BlockSpec((tm, d), row),
                  pl.BlockSpec((tm, DIFF_WIDTH), row),
                  pl.BlockSpec((tm, NA_WIDTH), row),
                  pl.BlockSpec(w_out.shape, const),
                  pl.BlockSpec((1, d), const),
                  pl.BlockSpec(w_mq.shape, const),
                  pl.BlockSpec((1, m_tok, 2 * MEM_WIDTH), lambda i: ((i * tm) // seq, 0, 0)),
                  pl.BlockSpec(w_mo.shape, const),
                  pl.BlockSpec((1, d), const),
                  pl.BlockSpec(wr_hi_t.shape, const),
                  pl.BlockSpec(wr_lo_t.shape, const),
                  pl.BlockSpec((N_EXPERTS, 1), const)],
        out_specs=[pl.BlockSpec((tm, d), row),
                   pl.BlockSpec((tm * SUBLANES, LANES), row),
                   pl.BlockSpec((SUBLANES, tm), col),
                   pl.BlockSpec((SUBLANES, tm), col),
                   pl.BlockSpec((SUBLANES, tm), col),
                   pl.BlockSpec((N_EXPERTS, 1), const)],
        out_shape=[jax.ShapeDtypeStruct((n, d), F32),
                   jax.ShapeDtypeStruct((n * SUBLANES, LANES), F32),
                   jax.ShapeDtypeStruct((SUBLANES, n), jnp.int32),
                   jax.ShapeDtypeStruct((SUBLANES, n), F32),
                   jax.ShapeDtypeStruct((SUBLANES, n), jnp.int32),
                   jax.ShapeDtypeStruct((N_EXPERTS, 1), F32)],
        scratch_shapes=[pltpu.VMEM((N_EXPERTS, 1), F32)],
        compiler_params=_params(("arbitrary",)),
        name="post_attn",
    )(x2d, o_diff, o_na, w_out, gq.reshape(1, d), w_mq, kv, w_mo, gf.reshape(1, d), wr_hi_t, wr_lo_t,
      b_router.reshape(N_EXPERTS, 1))


def _row_copy(src, src_row, dst, dst_row, sem):
    return pltpu.make_async_copy(
        src.at[pl.ds(pl.multiple_of(src_row * SUBLANES, SUBLANES), SUBLANES), :],
        dst.at[pl.ds(pl.multiple_of(dst_row * SUBLANES, SUBLANES), SUBLANES), :],
        sem)


def _ffn_kernel(blk_e_ref, nact_ref, next_e_ref, slot_cur_ref, slot_next_ref, slot_prev_ref, hf_ref,
                w1_ref, b1_ref, w2_ref, b2_ref, yc_ref,
                xbuf, ybuf, w1f_sc, w2f_sc, w1b_sc, w2b_sc, wslot_sc, gsem, ssem, wsem, *, blk, d_ff, n_tok):
    i = pl.program_id(0)
    nact = nact_ref[0]
    slot = i % 2
    y_slot = i % Y_SLOTS
    y_prev = (i + Y_SLOTS - 1) % Y_SLOTS

    trash_row = n_tok * TOP_K

    def gather(slot_ref, t, to_slot):
        tok = slot_ref[t] & (n_tok - 1)
        return _row_copy(hf_ref, tok, xbuf.at[to_slot], t, gsem.at[to_slot])

    def scatter(t):
        dst = jnp.where(i == 0, trash_row + t, slot_prev_ref[t])
        return _row_copy(ybuf.at[y_prev], t, yc_ref, dst, ssem)

    def wait_all(copy_of_row):
        def drain(g, carry):
            for u in range(WAITS_PER_TRIP):
                copy_of_row(g * WAITS_PER_TRIP + u).wait()
            return carry
        lax.fori_loop(0, blk // WAITS_PER_TRIP, drain, 0)

    @pl.when(i == 0)
    def _():
        ybuf[...] = jnp.zeros(ybuf.shape, F32)
        for t in range(blk):
            gather(slot_cur_ref, t, 0).start(priority=t % 2)

    def weight_copies(e, s):
        return (pltpu.make_async_copy(w1_ref.at[e], w1f_sc.at[s], wsem.at[s]),
                pltpu.make_async_copy(w2_ref.at[e], w2f_sc.at[s], wsem.at[s]))

    @pl.when(i == 0)
    def _():
        wslot_sc[0] = 0
        for cp in weight_copies(blk_e_ref[0], 0):
            cp.start(priority=1)

    @pl.when((i < nact) & ((i == 0) | (blk_e_ref[i] != blk_e_ref[jnp.maximum(i - 1, 0)])))
    def _():
        e = blk_e_ref[i]
        s = wslot_sc[0]
        for cp in weight_copies(e, s):
            cp.wait()
        nxt = next_e_ref[e]

        @pl.when(nxt >= 0)
        def _():
            for cp in weight_copies(nxt, 1 - s):
                cp.start(priority=1)

        rows = 256
        for c in range(w1b_sc.shape[0] // rows):
            w1b_sc[c * rows:(c + 1) * rows, :] = w1f_sc[s, c * rows:(c + 1) * rows, :].astype(BF16)
        for c in range(w2b_sc.shape[0] // rows):
            w2b_sc[c * rows:(c + 1) * rows, :] = w2f_sc[s, c * rows:(c + 1) * rows, :].astype(BF16)
        wslot_sc[0] = 1 - s

    @pl.when(i <= nact)
    def _():
        wait_all(lambda t: gather(slot_cur_ref, t, slot))

    @pl.when((i >= 1) & (i <= nact))
    def _():
        wait_all(scatter)

    @pl.when(i < nact)
    def _():
        for t in range(blk):
            gather(slot_next_ref, t, 1 - slot).start(priority=0)
            scatter(t).start(priority=(t + 1) % 2)
        x = _tiles_to_rows(xbuf, blk, lead=slot).astype(BF16)
        gu = _dot(x, w1b_sc[...]) + b1_ref[0]
        gate = jnp.minimum(gu[:, :d_ff], SWIGLU_LIMIT)
        lin = jnp.clip(gu[:, d_ff:], -SWIGLU_LIMIT, SWIGLU_LIMIT)
        act = gate * jax.nn.sigmoid(SWIGLU_ALPHA * gate) * (lin + 1.0)
        y = _dot(act.astype(BF16), w2b_sc[...]) + b2_ref[0]
        _rows_to_tiles(ybuf.at[y_slot], y, blk)

    @pl.when(i == nact)
    def _():
        def issue(t, carry):
            scatter(t).start()
            return carry
        lax.fori_loop(0, blk, issue, 0)
        wait_all(scatter)


def _expert_ffn(blk_e, nact, next_e, row_slot, hf_tiles, w1, b1, w2, b2, n_tok):
    blk = FFN_BLK
    nblk = row_slot.shape[0] // blk
    n_slots = n_tok * TOP_K
    assert n_tok & (n_tok - 1) == 0
    _, d, two_ff = w1.shape
    d_ff = two_ff // 2
    kern = functools.partial(_ffn_kernel, blk=blk, d_ff=d_ff, n_tok=n_tok)
    smem_blk = lambda f: pl.BlockSpec((blk,), f, memory_space=pltpu.SMEM)
    grid_spec = pltpu.PrefetchScalarGridSpec(
        num_scalar_prefetch=3,
        grid=(nblk,),
        in_specs=[smem_blk(lambda i, be, na, ne: (i,)),
                  smem_blk(lambda i, be, na, ne: (jnp.minimum(i + 1, nblk - 1),)),
                  smem_blk(lambda i, be, na, ne: (jnp.maximum(i - 1, 0),)),
                  pl.BlockSpec(memory_space=pl.ANY),
                  pl.BlockSpec(memory_space=pl.ANY),
                  pl.BlockSpec((1, 1, two_ff), lambda i, be, na, ne: (be[i], 0, 0)),
                  pl.BlockSpec(memory_space=pl.ANY),
                  pl.BlockSpec((1, 1, d), lambda i, be, na, ne: (be[i], 0, 0))],
        out_specs=pl.BlockSpec(memory_space=pl.ANY),
        scratch_shapes=[pltpu.VMEM((2, blk * SUBLANES, LANES), F32),
                        pltpu.VMEM((Y_SLOTS, blk * SUBLANES, LANES), F32),
                        pltpu.VMEM((2, d, two_ff), F32),
                        pltpu.VMEM((2, d_ff, d), F32),
                        pltpu.VMEM((d, two_ff), BF16),
                        pltpu.VMEM((d_ff, d), BF16),
                        pltpu.SMEM((1,), jnp.int32),
                        pltpu.SemaphoreType.DMA((2,)),
                        pltpu.SemaphoreType.DMA(()),
                        pltpu.SemaphoreType.DMA((2,))],
    )
    return pl.pallas_call(
        kern,
        grid_spec=grid_spec,
        out_shape=jax.ShapeDtypeStruct(((n_slots + blk) * SUBLANES, LANES), F32),
        compiler_params=_params(("arbitrary",)),
        name="expert_ffn",
    )(blk_e, nact, next_e, row_slot, row_slot, row_slot, hf_tiles, w1, b1.reshape(N_EXPERTS, 1, two_ff), w2,
      b2.reshape(N_EXPERTS, 1, d))


def _combine_kernel(x2_ref, tg_ref, gfin_ref, *rest, tm, final_norm):
    y_refs, o_ref = rest[:TOP_K], rest[TOP_K]
    gates = tg_ref[...]
    pieces = []
    for j in range(SUBLANES):
        acc = x2_ref[:, j * LANES:(j + 1) * LANES]
        for k in range(TOP_K):
            acc = acc + gates[:, k:k + 1] * y_refs[k][pl.ds(j, tm, stride=SUBLANES), :]
        pieces.append(acc)
    out = jnp.concatenate(pieces, axis=1)
    if final_norm:
        out = _rms(out, gfin_ref[...])
    o_ref[...] = out


def _combine(x2, gates, g_final, y_slots, final_norm):
    n, d = x2.shape
    tm = COMB_TM
    kern = functools.partial(_combine_kernel, tm=tm, final_norm=final_norm)
    tiles = n // tm

    def choice_spec(k):
        return pl.BlockSpec((tm * SUBLANES, LANES), lambda i: (k * tiles + i, 0))

    return pl.pallas_call(
        kern,
        grid=(tiles,),
        in_specs=[pl.BlockSpec((tm, d), lambda i: (i, 0)),
                  pl.BlockSpec((tm, SUBLANES), lambda i: (i, 0)),
                  pl.BlockSpec((1, d), lambda i: (0, 0))] + [choice_spec(k) for k in range(TOP_K)],
        out_specs=pl.BlockSpec((tm, d), lambda i: (i, 0)),
        out_shape=jax.ShapeDtypeStruct((n, d), F32),
        compiler_params=_params(("parallel",)),
        name="combine",
    )(x2, gates, g_final.reshape(1, d), *([y_slots] * TOP_K))


def _invert_kernel(pad_lo_ref, pad_hi_ref, dest_ref, slot_ref, *, chunk, n_tok, blk):
    n_assign = n_tok * TOP_K
    c = pl.program_id(0)

    @pl.when(c == 0)
    def _():
        def expert(e, carry):
            def pad_row(r, carry2):
                slot_ref[r] = n_assign + (r & (blk - 1))
                return carry2
            lax.fori_loop(pad_lo_ref[e], pad_hi_ref[e], pad_row, 0)
            return carry
        lax.fori_loop(0, N_EXPERTS, expert, 0)

    def body(g, carry):
        a0 = g * INVERT_UNROLL
        tok0 = lax.shift_right_logical(c * chunk + a0, TOP_K.bit_length() - 1)
        for u in range(INVERT_UNROLL):
            slot_ref[dest_ref[a0 + u]] = tok0 + u // TOP_K + (u % TOP_K) * n_tok
        return carry
    lax.fori_loop(0, chunk // INVERT_UNROLL, body, 0)


def _invert_routing(dest, pad_lo, pad_hi, p_rows):
    n_assign = dest.shape[0]
    chunk = INVERT_CHUNK
    blk = FFN_BLK
    assert n_assign % chunk == 0 and chunk % INVERT_UNROLL == 0 and blk & (blk - 1) == 0
    assert INVERT_UNROLL % TOP_K == 0 and TOP_K & (TOP_K - 1) == 0
    kern = functools.partial(_invert_kernel, chunk=chunk, n_tok=n_assign // TOP_K, blk=blk)
    whole_smem = pl.BlockSpec(memory_space=pltpu.SMEM)
    grid_spec = pltpu.PrefetchScalarGridSpec(
        num_scalar_prefetch=2,
        grid=(n_assign // chunk,),
        in_specs=[pl.BlockSpec((chunk,), lambda c, lo, hi: (c,), memory_space=pltpu.SMEM)],
        out_specs=whole_smem,
    )
    return pl.pallas_call(
        kern,
        grid_spec=grid_spec,
        out_shape=jax.ShapeDtypeStruct((p_rows,), jnp.int32),
        compiler_params=_params(("arbitrary",)),
        name="invert_routing",
    )(pad_lo, pad_hi, dest)


def _routing_tables(counts_f32, top_i, rank, n_tok):
    blk = FFN_BLK
    n_assign = n_tok * TOP_K
    counts = counts_f32.astype(jnp.int32)
    pcounts = ((counts + blk - 1) // blk) * blk
    pends = jnp.cumsum(pcounts)
    pstarts = pends - pcounts
    dest = (pstarts[top_i] + rank).reshape(-1)
    p_rows = n_assign + N_EXPERTS * blk
    nblk = p_rows // blk
    pad_lo = (pstarts + counts).astype(jnp.int32)
    pad_hi = pends.at[N_EXPERTS - 1].set(p_rows).astype(jnp.int32)
    row_slot = _invert_routing(dest.astype(jnp.int32), pad_lo, pad_hi, p_rows)
    blk_start = jnp.arange(nblk, dtype=jnp.int32) * blk
    blk_e = jnp.minimum(jnp.sum(blk_start[:, None] >= pends[None, :], axis=1), N_EXPERTS - 1).astype(jnp.int32)
    nact = (pends[-1] // blk).astype(jnp.int32).reshape(1)
    e_ids = jnp.arange(N_EXPERTS, dtype=jnp.int32)
    later_active = jnp.where((counts[None, :] > 0) & (e_ids[None, :] > e_ids[:, None]), e_ids[None, :], N_EXPERTS)
    next_e = jnp.min(later_active, axis=1)
    next_e = jnp.where(next_e == N_EXPERTS, -1, next_e).astype(jnp.int32)
    return row_slot, blk_e, nact, next_e


def _lambda_init(layer):
    return 0.8 - 0.6 * math.exp(-0.3 * layer)


def kernel(x, mem, norm_mix_g, w_in, lambda_q1, lambda_k1, lambda_q2, lambda_k2, subln_g, rpb, w_out,
           norm_mem_q_g, norm_mem_kv_g, w_mq, w_mkv, w_mo, norm_ffn_g, w_router, b_router, w1, b1, w2, b2,
           norm_final_g):
    b, s, d = x.shape
    n = b * s
    depth = w_in.shape[0]
    assert d == SUBLANES * LANES and s % GRID_W == 0
    slopes = jnp.asarray([2.0 ** (-8.0 * (i + 1) / DIFF_HEADS) for i in range(DIFF_HEADS)], F32)
    x2d = x.reshape(n, d)
    for l in range(depth):
        lam_init = _lambda_init(l)
        lam = (jnp.exp(jnp.sum(lambda_q1[l] * lambda_k1[l])) - jnp.exp(jnp.sum(lambda_q2[l] * lambda_k2[l]))
               + lam_init).astype(F32)
        qkv = _qkv_proj(x2d, norm_mix_g[l], w_in[l].astype(BF16))
        qkv3 = qkv.reshape(b, s, qkv.shape[1])
        o_diff = _diff_attn(qkv3, lam, slopes, subln_g[l], lam_init).reshape(n, DIFF_WIDTH)
        o_na = _na_attn(qkv3, _na_bias_table(rpb[l], s // GRID_W)).reshape(n, NA_WIDTH)
        kv = _mem_kv(mem, norm_mem_kv_g[l], w_mkv[l].astype(BF16))
        wr_t = w_router[l].T
        wr_hi_t = wr_t.astype(BF16)
        wr_lo_t = (wr_t - wr_hi_t.astype(F32)).astype(BF16)
        x2, hf_tiles, ti, tg, tr, counts = _post_attn(
            x2d, o_diff, o_na, w_out[l].astype(BF16), norm_mem_q_g[l], w_mq[l].astype(BF16), kv,
            w_mo[l].astype(BF16), norm_ffn_g[l], wr_hi_t, wr_lo_t, b_router[l], s)
        row_slot, blk_e, nact, next_e = _routing_tables(counts[:, 0], ti[:TOP_K].T, tr[:TOP_K].T, n)
        y_slots = _expert_ffn(blk_e, nact, next_e, row_slot, hf_tiles, w1[l], b1[l], w2[l], b2[l], n)
        x2d = _combine(x2, tg.T, norm_final_g, y_slots, final_norm=(l == depth - 1))
    return x2d.reshape(b, s, d)
```

```python
import functools
import math

import jax
import jax.numpy as jnp
import numpy as np
from jax import lax
from jax.experimental import pallas as pl
from jax.experimental.pallas import tpu as pltpu

HEAD_DIM = 64
DIFF_HEADS = 4
DIFF_VDIM = 2 * HEAD_DIM
DIFF_QK_WIDTH = DIFF_HEADS * 2 * HEAD_DIM
DIFF_WIDTH = DIFF_HEADS * DIFF_VDIM
NA_HEADS = 8
NA_WIDTH = NA_HEADS * HEAD_DIM
GRID_W = 64
NA_ROWS = 8
NA_COLS = 16
MEM_HEADS = 4
MEM_HEAD_DIM = 64
MEM_WIDTH = MEM_HEADS * MEM_HEAD_DIM
N_EXPERTS = 32
TOP_K = 4
SWIGLU_LIMIT = 7.0
SWIGLU_ALPHA = 1.702
RMS_EPS = 1e-5

LANES = 128
SUBLANES = 8
NEG_BIG = -1e30

F32 = jnp.float32
BF16 = jnp.bfloat16

QKV_TM = 512
DIFF_TQ = 512
DIFF_TK = 512
DIFF_LANE_PARTS = 4
NA_R = 16
POST_TM = 512
POST_PARTS = 2
FFN_BLK = 256
COMB_TM = 256
WAITS_PER_TRIP = 16
INVERT_CHUNK = 8192
INVERT_UNROLL = 16
Y_SLOTS = 3
VMEM_LIMIT = 56 * 1024 * 1024


def _params(sem):
    return pltpu.CompilerParams(dimension_semantics=sem, vmem_limit_bytes=VMEM_LIMIT)


def _rms(x, g):
    return x * lax.rsqrt(jnp.mean(x * x, axis=-1, keepdims=True) + RMS_EPS) * g


def _dot(a, b):
    return jnp.dot(a, b, preferred_element_type=F32)


def _dot_nt(a, b):
    return lax.dot_general(a, b, (((1,), (1,)), ((), ())), preferred_element_type=F32)


def _rows_to_tiles(ref, val, rows):
    for j in range(SUBLANES):
        ref[pl.ds(j, rows, stride=SUBLANES), :] = val[:, j * LANES:(j + 1) * LANES]


def _tiles_to_rows(ref, rows, lead=None):
    if lead is None:
        parts = [ref[pl.ds(j, rows, stride=SUBLANES), :] for j in range(SUBLANES)]
    else:
        parts = [ref[lead, pl.ds(j, rows, stride=SUBLANES), :] for j in range(SUBLANES)]
    return jnp.concatenate(parts, axis=1)


def _qkv_kernel(x_ref, g_ref, w_ref, o_ref):
    h = _rms(x_ref[...], g_ref[...]).astype(BF16)
    cols = o_ref.shape[1]
    step = 512
    for c in range(cols // step):
        o_ref[:, c * step:(c + 1) * step] = _dot(h, w_ref[:, c * step:(c + 1) * step]).astype(o_ref.dtype)


def _qkv_proj(x2d, g, w_bf16):
    n, d = x2d.shape
    cols = w_bf16.shape[1]
    return pl.pallas_call(
        _qkv_kernel,
        grid=(n // QKV_TM,),
        in_specs=[pl.BlockSpec((QKV_TM, d), lambda i: (i, 0)),
                  pl.BlockSpec((1, d), lambda i: (0, 0)),
                  pl.BlockSpec((d, cols), lambda i: (0, 0))],
        out_specs=pl.BlockSpec((QKV_TM, cols), lambda i: (i, 0)),
        out_shape=jax.ShapeDtypeStruct((n, cols), BF16),
        compiler_params=_params(("parallel",)),
        name="qkv_proj",
    )(x2d, g.reshape(1, d), w_bf16)


ALIBI_AUG_LEFT, ALIBI_AUG_RIGHT, ALIBI_AUG_NONE = 0, 1, 2
ONES_ROWS = 16
EXP_UNDERFLOW = 105.0
NORM_BOUND_SLACK = 1.02


def _bf16_exact_split(n):
    lo_bits = max(0, int(n.max()).bit_length() - 8)
    lo = n % (1 << lo_bits)
    return n - lo, lo


def _alibi_aug_tables(tq, tk):
    ii_hi, ii_lo = _bf16_exact_split(np.arange(tq))
    jj_hi, jj_lo = _bf16_exact_split(np.arange(tk))
    qa = np.zeros((DIFF_HEADS, tq, LANES), np.float32)
    ka = np.zeros((DIFF_HEADS, 3, tk, LANES), np.float32)
    for h in range(DIFF_HEADS):
        sl = 2.0 ** (-8.0 * (h + 1) / DIFF_HEADS)
        qa[h, :, 0], qa[h, :, 1], qa[h, :, 2], qa[h, :, 3] = -sl * ii_hi, -sl * ii_lo, 1.0, 1.0
        qa[h, :, 4], qa[h, :, 5], qa[h, :, 6], qa[h, :, 7] = sl * ii_hi, sl * ii_lo, 1.0, 1.0
        ka[h, ALIBI_AUG_LEFT, :, 0], ka[h, ALIBI_AUG_LEFT, :, 1] = 1.0, 1.0
        ka[h, ALIBI_AUG_LEFT, :, 2], ka[h, ALIBI_AUG_LEFT, :, 3] = sl * jj_hi, sl * jj_lo
        ka[h, ALIBI_AUG_RIGHT, :, 4], ka[h, ALIBI_AUG_RIGHT, :, 5] = 1.0, 1.0
        ka[h, ALIBI_AUG_RIGHT, :, 6], ka[h, ALIBI_AUG_RIGHT, :, 7] = -sl * jj_hi, -sl * jj_lo
    return jnp.asarray(qa, BF16), jnp.asarray(ka, BF16)


def _diff_kernel(lam_ref, slope_ref, q_ref, k_ref, vt_ref, qaug_ref, kaug_ref, g_ref, o_ref,
                 qa_sc, s0_sc, s1_sc, mb_sc, m_sc, acc_sc, ksq_sc, *, tq, tk, seq, out_scale):
    h = pl.program_id(1)
    i = pl.program_id(2)
    slope = slope_ref[h]
    lam = lam_ref[0]
    nk = seq // tk
    i0 = i * tq
    jd = i0 // tk

    sel_r = lax.broadcasted_iota(jnp.int32, (LANES, LANES), 0) // HEAD_DIM
    sel_c = lax.broadcasted_iota(jnp.int32, (LANES, LANES), 1)
    sel = (sel_r == sel_c).astype(BF16)

    def max_sq_norm(x):
        xf = x.astype(F32)
        return jnp.max(_dot((xf * xf).astype(BF16), sel), axis=0, keepdims=True)

    @pl.when(i == 0)
    def _():
        ksq_sc[...] = max_sq_norm(k_ref[0])

    q = q_ref[0] * jnp.asarray(1.0 / math.sqrt(HEAD_DIM), BF16)
    lane = lax.broadcasted_iota(jnp.int32, (tq, LANES), 1)
    zero = jnp.zeros_like(q)
    qa_sc[:tq, :LANES] = jnp.where(lane < HEAD_DIM, q, zero)
    qa_sc[tq:, :LANES] = jnp.where(lane >= HEAD_DIM, q, zero)
    qa_sc[:tq, LANES:] = qaug_ref[0]
    qa_sc[tq:, LANES:] = qaug_ref[0]
    m_sc[...] = jnp.full(m_sc.shape, NEG_BIG, F32)
    acc_sc[...] = jnp.zeros(acc_sc.shape, F32)

    part = 2 * tq // DIFF_LANE_PARTS
    maps = tuple(slice(c * part, (c + 1) * part) for c in range(DIFF_LANE_PARTS))

    def scores(j, side, mp):
        start = pl.multiple_of(j * tk, tk)
        kba = jnp.concatenate([k_ref[0, pl.ds(start, tk), :], kaug_ref[0, side]], axis=1)
        return _dot_nt(kba, qa_sc[mp, :])

    s_bufs = (s0_sc, s1_sc)

    def stash(slot, mp, st):
        s_bufs[slot][:, mp] = st
        mb_sc[slot, :, mp] = jnp.max(st, axis=0, keepdims=True)

    def consume(slot, mp, j, c):
        m_prev = m_sc[:, mp]
        m_new = jnp.maximum(m_prev, mb_sc[slot, :, mp] + c)
        alpha = jnp.exp(m_prev - m_new)
        p_t = jnp.exp(s_bufs[slot][:, mp] - (m_new - c)).astype(BF16)
        start = pl.multiple_of(j * tk, tk)
        vta = jnp.concatenate([vt_ref[0, :, pl.ds(start, tk)], jnp.ones((ONES_ROWS, tk), BF16)], axis=0)
        acc_sc[:, mp] = alpha * acc_sc[:, mp] + _dot(vta, p_t)
        m_sc[:, mp] = m_new

    def block_const(j):
        return -slope * jnp.abs(i0 - j * tk).astype(F32)

    rel_t = (lax.broadcasted_iota(jnp.int32, (tk, tq), 0) - lax.broadcasted_iota(jnp.int32, (tk, tq), 1)).astype(F32)
    bias_t = -slope * jnp.abs(rel_t + (jd * tk - i0).astype(F32))
    for mp in maps:
        stash(0, mp, scores(jd, ALIBI_AUG_NONE, mp) + bias_t[:, mp.start % tq:mp.start % tq + part])

    qk_bound = jnp.sqrt(jnp.max(max_sq_norm(q) * ksq_sc[...])) * NORM_BOUND_SLACK
    reach = (qk_bound - jnp.min(mb_sc[0]) + EXP_UNDERFLOW) / slope
    reach = jnp.minimum(reach, 2.0 * seq)
    j_lo = jnp.clip(jnp.floor((i0 - tk + 1 - reach) / tk).astype(jnp.int32) + 1, 0, jd)
    j_hi = jnp.clip(jnp.ceil((reach + i0 + tq - 1) / tk).astype(jnp.int32) - 1, jd, nk - 1)
    n_blocks = j_hi - j_lo + 1

    def block_of(t):
        return j_lo + jnp.where(t - 1 < jd - j_lo, t - 1, t)

    def produce(t, slot, mp):
        j = block_of(t)
        stash(slot, mp, scores(j, jnp.where(j < jd, ALIBI_AUG_LEFT, ALIBI_AUG_RIGHT), mp))

    def consume_step(t, slot, mp):
        j = jnp.where(t == 0, jd, block_of(t))
        consume(slot, mp, j, jnp.where(t == 0, 0.0, block_const(j)))

    def stage(t_produce, t_consume, slot_produce, slot_consume):
        for mp in maps:
            if t_produce is not None:
                produce(t_produce, slot_produce, mp)
            consume_step(t_consume, slot_consume, mp)

    def body(u, carry):
        t = 2 * u
        stage(t + 1, t, 1, 0)
        stage(t + 2, t + 1, 0, 1)
        return carry

    def body4(u, carry):
        body(2 * u, carry)
        body(2 * u + 1, carry)
        return carry

    quads = (n_blocks - 1) // 4
    lax.fori_loop(0, quads, body4, 0)
    lax.fori_loop(2 * quads, (n_blocks - 1) // 2, body, 0)

    @pl.when(n_blocks % 2 == 0)
    def _():
        stage(n_blocks - 1, n_blocks - 2, 1, 0)
        stage(None, n_blocks - 1, None, 1)

    @pl.when(n_blocks % 2 == 1)
    def _():
        stage(None, n_blocks - 1, None, 0)

    acc = acc_sc[...]
    o = acc[:DIFF_VDIM] * (1.0 / acc[DIFF_VDIM:DIFF_VDIM + 1])
    d = o[:, :tq] - lam * o[:, tq:]
    y = d * lax.rsqrt(jnp.mean(d * d, axis=0, keepdims=True) + RMS_EPS) * g_ref[...] * out_scale
    o_ref[0] = y.T.astype(o_ref.dtype)


def _diff_attn(qkv3, lam, slopes, subln_g, lam_init):
    b, s, _ = qkv3.shape
    tq, tk = DIFF_TQ, DIFF_TK
    assert tk % tq == 0 and s % (2 * tk) == 0
    kcol = DIFF_QK_WIDTH // LANES
    v_t = qkv3[:, :, 2 * DIFF_QK_WIDTH:2 * DIFF_QK_WIDTH + DIFF_WIDTH].transpose(0, 2, 1)
    qaug, kaug = _alibi_aug_tables(tq, tk)
    kern = functools.partial(_diff_kernel, tq=tq, tk=tk, seq=s, out_scale=1.0 - lam_init)
    return pl.pallas_call(
        kern,
        grid=(b, DIFF_HEADS, s // tq),
        in_specs=[pl.BlockSpec(memory_space=pltpu.SMEM),
                  pl.BlockSpec(memory_space=pltpu.SMEM),
                  pl.BlockSpec((1, tq, LANES), lambda bi, h, i: (bi, i, h)),
                  pl.BlockSpec((1, s, LANES), lambda bi, h, i: (bi, 0, kcol + h)),
                  pl.BlockSpec((1, DIFF_VDIM, s), lambda bi, h, i: (bi, h, 0)),
                  pl.BlockSpec((1, tq, LANES), lambda bi, h, i: (h, 0, 0)),
                  pl.BlockSpec((1, 3, tk, LANES), lambda bi, h, i: (h, 0, 0, 0)),
                  pl.BlockSpec((DIFF_VDIM, 1), lambda bi, h, i: (0, 0))],
        out_specs=pl.BlockSpec((1, tq, LANES), lambda bi, h, i: (bi, i, h)),
        out_shape=jax.ShapeDtypeStruct((b, s, DIFF_WIDTH), BF16),
        scratch_shapes=[pltpu.VMEM((2 * tq, 2 * LANES), BF16),
                        pltpu.VMEM((tk, 2 * tq), F32),
                        pltpu.VMEM((tk, 2 * tq), F32),
                        pltpu.VMEM((2, 1, 2 * tq), F32),
                        pltpu.VMEM((1, 2 * tq), F32),
                        pltpu.VMEM((DIFF_VDIM + ONES_ROWS, 2 * tq), F32),
                        pltpu.VMEM((1, LANES), F32)],
        compiler_params=_params(("parallel", "parallel", "arbitrary")),
        name="diff_attn",
    )(lam.reshape(1), slopes, qkv3, qkv3, v_t, qaug, kaug, subln_g.reshape(DIFF_VDIM, 1))


def _na_bias_table(rpb, rows):
    kr = min(NA_ROWS, rows)
    q = np.arange(GRID_W)[:, None]
    j = np.arange(GRID_W)[None, :]
    cs = np.clip(q - NA_COLS // 2, 0, GRID_W - NA_COLS)
    valid = (j >= cs) & (j < cs + NA_COLS)
    dc = j - q + (NA_COLS - 1)
    sel = ((dc[:, :, None] == np.arange(2 * NA_COLS - 1)[None, None, :]) & valid[:, :, None]).astype(np.float32)
    col = jnp.einsum('hrd,qjd->hqrj', rpb.astype(F32), jnp.asarray(sel), precision=lax.Precision.HIGHEST)
    col = jnp.where(valid[None, :, None, :], col, NEG_BIG)
    cases = [col[:, :, NA_ROWS - 1 - c:NA_ROWS - 1 - c + kr] for c in range(kr)]
    return jnp.stack(cases, axis=0).reshape(kr, NA_HEADS // 2, 2 * GRID_W, kr * GRID_W)


def _na_kernel(q_ref, k_ref, v_ref, bias_ref, o_ref, *, rows_per_step, rows, kr):
    i = pl.program_id(2)
    lane = lax.broadcasted_iota(jnp.int32, (GRID_W, LANES), 1)
    win = kr * GRID_W

    starts, scores, probs = [], [], []
    for rr in range(rows_per_step):
        r = i * rows_per_step + rr
        rs = jnp.clip(r - kr // 2, 0, rows - kr)
        q = q_ref[0, rr * GRID_W:(rr + 1) * GRID_W, :]
        q = q * jnp.asarray(1.0 / math.sqrt(HEAD_DIM), BF16)
        zero = jnp.zeros_like(q)
        qs = jnp.concatenate([jnp.where(lane < HEAD_DIM, q, zero), jnp.where(lane >= HEAD_DIM, q, zero)], axis=0)
        start = pl.multiple_of(rs * GRID_W, GRID_W)
        starts.append(start)
        scores.append(_dot_nt(qs, k_ref[0, pl.ds(start, win), :]) + bias_ref[r - rs, 0])
    for s in scores:
        p = jnp.exp(s - jnp.max(s, axis=1, keepdims=True))
        probs.append((p.astype(BF16), jnp.sum(p, axis=1, keepdims=True)))
    for rr, (p, l) in enumerate(probs):
        o = _dot(p, v_ref[0, pl.ds(starts[rr], win), :]) / l
        out = jnp.where(lane < HEAD_DIM, o[:GRID_W], o[GRID_W:])
        o_ref[0, rr * GRID_W:(rr + 1) * GRID_W, :] = out.astype(o_ref.dtype)


def _na_attn(qkv3, bias_tbl):
    b, s, _ = qkv3.shape
    rows = s // GRID_W
    kr = min(NA_ROWS, rows)
    base = (2 * DIFF_QK_WIDTH + DIFF_WIDTH) // LANES
    pairs = NA_HEADS // 2
    rstep = NA_R
    kern = functools.partial(_na_kernel, rows_per_step=rstep, rows=rows, kr=kr)
    return pl.pallas_call(
        kern,
        grid=(b, pairs, rows // rstep),
        in_specs=[pl.BlockSpec((1, rstep * GRID_W, LANES), lambda bi, p, i: (bi, i, base + p)),
                  pl.BlockSpec((1, s, LANES), lambda bi, p, i: (bi, 0, base + pairs + p)),
                  pl.BlockSpec((1, s, LANES), lambda bi, p, i: (bi, 0, base + 2 * pairs + p)),
                  pl.BlockSpec((kr, 1, 2 * GRID_W, kr * GRID_W), lambda bi, p, i: (0, p, 0, 0))],
        out_specs=pl.BlockSpec((1, rstep * GRID_W, LANES), lambda bi, p, i: (bi, i, p)),
        out_shape=jax.ShapeDtypeStruct((b, s, NA_WIDTH), BF16),
        compiler_params=_params(("parallel", "parallel", "parallel")),
        name="na_attn",
    )(qkv3, qkv3, qkv3, bias_tbl)


def _memkv_kernel(mem_ref, g_ref, w_ref, o_ref):
    h = _rms(mem_ref[0], g_ref[...]).astype(BF16)
    o_ref[0] = _dot(h, w_ref[...]).astype(o_ref.dtype)


def _mem_kv(mem, g, w_bf16):
    b, m, d = mem.shape
    cols = w_bf16.shape[1]
    return pl.pallas_call(
        _memkv_kernel,
        grid=(b,),
        in_specs=[pl.BlockSpec((1, m, d), lambda i: (i, 0, 0)),
                  pl.BlockSpec((1, d), lambda i: (0, 0)),
                  pl.BlockSpec((d, cols), lambda i: (0, 0))],
        out_specs=pl.BlockSpec((1, m, cols), lambda i: (i, 0, 0)),
        out_shape=jax.ShapeDtypeStruct((b, m, cols), BF16),
        compiler_params=_params(("parallel",)),
        name="mem_kv",
    )(mem, g.reshape(1, d), w_bf16)


def _sublane_pack(rows, tm):
    sub = lax.broadcasted_iota(jnp.int32, (SUBLANES, tm), 0)
    out = jnp.zeros((SUBLANES, tm), rows[0].dtype)
    for k, r in enumerate(rows):
        out = jnp.where(sub == k, r, out)
    return out


def _post_kernel(x_ref, od_ref, on_ref, wout_ref, gq_ref, wmq_ref, kv_ref, wmo_ref, gf_ref,
                 wrh_ref, wrl_ref, br_ref,
                 x2_ref, hf_ref, ti_ref, tg_ref, tr_ref, cnt_ref, carry_sc, *, tm, parts):
    step = pl.program_id(0)
    rows = tm // parts

    @pl.when(step == 0)
    def _():
        carry_sc[...] = jnp.zeros(carry_sc.shape, F32)

    head_of_lane = lax.broadcasted_iota(jnp.int32, (rows, MEM_WIDTH), 1) // MEM_HEAD_DIM
    kv = kv_ref[0]

    x1s, qss = [], []
    for g in range(parts):
        rs = slice(g * rows, (g + 1) * rows)
        x1 = (x_ref[rs, :] + _dot(od_ref[rs, :], wout_ref[:DIFF_WIDTH, :])
              + _dot(on_ref[rs, :], wout_ref[DIFF_WIDTH:, :]))
        hm = _rms(x1, gq_ref[...]).astype(BF16)
        q = (_dot(hm, wmq_ref[...]) * (1.0 / math.sqrt(MEM_HEAD_DIM))).astype(BF16)
        zero = jnp.zeros_like(q)
        qss.append(jnp.concatenate([jnp.where(head_of_lane == h, q, zero) for h in range(MEM_HEADS)], axis=0))
        x1s.append(x1)

    probs = []
    for g in range(parts):
        s = _dot_nt(qss[g], kv[:, :MEM_WIDTH])
        p = jnp.exp(s - jnp.max(s, axis=1, keepdims=True))
        probs.append((p.astype(BF16), jnp.sum(p, axis=1, keepdims=True)))

    hfs = []
    for g in range(parts):
        rs = slice(g * rows, (g + 1) * rows)
        p, l = probs[g]
        o4 = _dot(p, kv[:, MEM_WIDTH:]) / l
        o = jnp.zeros((rows, MEM_WIDTH), F32)
        for h in range(MEM_HEADS):
            o = jnp.where(head_of_lane == h, o4[h * rows:(h + 1) * rows], o)
        x2 = x1s[g] + _dot(o.astype(BF16), wmo_ref[...])
        x2_ref[rs, :] = x2
        hf = _rms(x2, gf_ref[...])
        _rows_to_tiles(hf_ref.at[pl.ds(g * rows * SUBLANES, rows * SUBLANES), :], hf, rows)
        hfs.append(hf)

    e_iota = lax.broadcasted_iota(jnp.int32, (N_EXPERTS, rows), 0)
    r_i = lax.broadcasted_iota(jnp.int32, (rows, rows), 0)
    c_i = lax.broadcasted_iota(jnp.int32, (rows, rows), 1)
    earlier = (r_i < c_i).astype(BF16)
    carry = carry_sc[...]
    idx_rows, gate_rows, rank_rows = [], [], []
    for g in range(parts):
        hi = hfs[g].astype(BF16)
        lo = (hfs[g] - hi.astype(F32)).astype(BF16)
        logits = _dot_nt(wrh_ref[...], hi) + _dot_nt(wrl_ref[...], hi) + _dot_nt(wrh_ref[...], lo) + br_ref[...]
        vals, idxs, hots = [], [], []
        cur = logits
        for _ in range(TOP_K):
            mx = jnp.max(cur, axis=0, keepdims=True)
            idx = jnp.min(jnp.where(cur == mx, e_iota, N_EXPERTS), axis=0, keepdims=True)
            hot = e_iota == idx
            vals.append(mx)
            idxs.append(idx)
            hots.append(hot)
            cur = jnp.where(hot, -jnp.inf, cur)
        exps = [jnp.exp(v - vals[0]) for v in vals]
        den = exps[0] + exps[1] + exps[2] + exps[3]
        chosen = jnp.zeros((N_EXPERTS, rows), F32)
        for hot in hots:
            chosen = chosen + hot.astype(F32)
        before = _dot(chosen.astype(BF16), earlier) + carry
        carry = carry + jnp.sum(chosen, axis=1, keepdims=True)
        idx_rows.append(idxs)
        gate_rows.append([e / den for e in exps])
        rank_rows.append([jnp.sum(jnp.where(hot, before, 0.0), axis=0, keepdims=True).astype(jnp.int32)
                          for hot in hots])
    carry_sc[...] = carry
    cnt_ref[...] = carry

    def tile_rows(per_part):
        return [jnp.concatenate([per_part[g][k] for g in range(parts)], axis=1) for k in range(TOP_K)]

    ti_ref[...] = _sublane_pack(tile_rows(idx_rows), tm)
    tg_ref[...] = _sublane_pack(tile_rows(gate_rows), tm)
    tr_ref[...] = _sublane_pack(tile_rows(rank_rows), tm)


def _post_attn(x2d, o_diff, o_na, w_out, gq, w_mq, kv, w_mo, gf, wr_hi_t, wr_lo_t, b_router, seq):
    n, d = x2d.shape
    tm = POST_TM
    m_tok = kv.shape[1]
    const = lambda i: (0, 0)
    row = lambda i: (i, 0)
    col = lambda i: (0, i)
    kern = functools.partial(_post_kernel, tm=tm, parts=POST_PARTS)
    return pl.pallas_call(
        kern,
        grid=(n // tm,),
        in_specs=[pl.BlockSpec((tm, d), row),
                  pl.BlockSpec((tm, DIFF_WIDTH), row),
                  pl.BlockSpec((tm, NA_WIDTH), row),
                  pl.BlockSpec(w_out.shape, const),
                  pl.BlockSpec((1, d), const),
                  pl.BlockSpec(w_mq.shape, const),
                  pl.BlockSpec((1, m_tok, 2 * MEM_WIDTH), lambda i: ((i * tm) // seq, 0, 0)),
                  pl.BlockSpec(w_mo.shape, const),
                  pl.BlockSpec((1, d), const),
                  pl.BlockSpec(wr_hi_t.shape, const),
                  pl.BlockSpec(wr_lo_t.shape, const),
                  pl.BlockSpec((N_EXPERTS, 1), const)],
        out_specs=[pl.BlockSpec((tm, d), row),
                   pl.BlockSpec((tm * SUBLANES, LANES), row),
                   pl.BlockSpec((SUBLANES, tm), col),
                   pl.BlockSpec((SUBLANES, tm), col),
                   pl.BlockSpec((SUBLANES, tm), col),
                   pl.BlockSpec((N_EXPERTS, 1), const)],
        out_shape=[jax.ShapeDtypeStruct((n, d), F32),
                   jax.ShapeDtypeStruct((n * SUBLANES, LANES), F32),
                   jax.ShapeDtypeStruct((SUBLANES, n), jnp.int32),
                   jax.ShapeDtypeStruct((SUBLANES, n), F32),
                   jax.ShapeDtypeStruct((SUBLANES, n), jnp.int32),
                   jax.ShapeDtypeStruct((N_EXPERTS, 1), F32)],
        scratch_shapes=[pltpu.VMEM((N_EXPERTS, 1), F32)],
        compiler_params=_params(("arbitrary",)),
        name="post_attn",
    )(x2d, o_diff, o_na, w_out, gq.reshape(1, d), w_mq, kv, w_mo, gf.reshape(1, d), wr_hi_t, wr_lo_t,
      b_router.reshape(N_EXPERTS, 1))


def _row_copy(src, src_row, dst, dst_row, sem):
    return pltpu.make_async_copy(
        src.at[pl.ds(pl.multiple_of(src_row * SUBLANES, SUBLANES), SUBLANES), :],
        dst.at[pl.ds(pl.multiple_of(dst_row * SUBLANES, SUBLANES), SUBLANES), :],
        sem)


def _ffn_kernel(blk_e_ref, nact_ref, next_e_ref, slot_cur_ref, slot_next_ref, slot_prev_ref, hf_ref,
                w1_ref, b1_ref, w2_ref, b2_ref, yc_ref,
                xbuf, ybuf, w1f_sc, w2f_sc, w1b_sc, w2b_sc, wslot_sc, gsem, ssem, wsem, *, blk, d_ff, n_tok):
    i = pl.program_id(0)
    nact = nact_ref[0]
    slot = i % 2
    y_slot = i % Y_SLOTS
    y_prev = (i + Y_SLOTS - 1) % Y_SLOTS

    trash_row = n_tok * TOP_K

    def gather(slot_ref, t, to_slot):
        tok = slot_ref[t] & (n_tok - 1)
        return _row_copy(hf_ref, tok, xbuf.at[to_slot], t, gsem.at[to_slot])

    def scatter(t):
        dst = jnp.where(i == 0, trash_row + t, slot_prev_ref[t])
        return _row_copy(ybuf.at[y_prev], t, yc_ref, dst, ssem)

    def wait_all(copy_of_row):
        def drain(g, carry):
            for u in range(WAITS_PER_TRIP):
                copy_of_row(g * WAITS_PER_TRIP + u).wait()
            return carry
        lax.fori_loop(0, blk // WAITS_PER_TRIP, drain, 0)

    @pl.when(i == 0)
    def _():
        ybuf[...] = jnp.zeros(ybuf.shape, F32)
        for t in range(blk):
            gather(slot_cur_ref, t, 0).start(priority=t % 2)

    def weight_copies(e, s):
        return (pltpu.make_async_copy(w1_ref.at[e], w1f_sc.at[s], wsem.at[s]),
                pltpu.make_async_copy(w2_ref.at[e], w2f_sc.at[s], wsem.at[s]))

    @pl.when(i == 0)
    def _():
        wslot_sc[0] = 0
        for cp in weight_copies(blk_e_ref[0], 0):
            cp.start(priority=1)

    @pl.when((i < nact) & ((i == 0) | (blk_e_ref[i] != blk_e_ref[jnp.maximum(i - 1, 0)])))
    def _():
        e = blk_e_ref[i]
        s = wslot_sc[0]
        for cp in weight_copies(e, s):
            cp.wait()
        nxt = next_e_ref[e]

        @pl.when(nxt >= 0)
        def _():
            for cp in weight_copies(nxt, 1 - s):
                cp.start(priority=1)

        rows = 256
        for c in range(w1b_sc.shape[0] // rows):
            w1b_sc[c * rows:(c + 1) * rows, :] = w1f_sc[s, c * rows:(c + 1) * rows, :].astype(BF16)
        for c in range(w2b_sc.shape[0] // rows):
            w2b_sc[c * rows:(c + 1) * rows, :] = w2f_sc[s, c * rows:(c + 1) * rows, :].astype(BF16)
        wslot_sc[0] = 1 - s

    @pl.when(i <= nact)
    def _():
        wait_all(lambda t: gather(slot_cur_ref, t, slot))

    @pl.when((i >= 1) & (i <= nact))
    def _():
        wait_all(scatter)

    @pl.when(i < nact)
    def _():
        for t in range(blk):
            gather(slot_next_ref, t, 1 - slot).start(priority=0)
            scatter(t).start(priority=(t + 1) % 2)
        x = _tiles_to_rows(xbuf, blk, lead=slot).astype(BF16)
        gu = _dot(x, w1b_sc[...]) + b1_ref[0]
        gate = jnp.minimum(gu[:, :d_ff], SWIGLU_LIMIT)
        lin = jnp.clip(gu[:, d_ff:], -SWIGLU_LIMIT, SWIGLU_LIMIT)
        act = gate * jax.nn.sigmoid(SWIGLU_ALPHA * gate) * (lin + 1.0)
        y = _dot(act.astype(BF16), w2b_sc[...]) + b2_ref[0]
        _rows_to_tiles(ybuf.at[y_slot], y, blk)

    @pl.when(i == nact)
    def _():
        def issue(t, carry):
            scatter(t).start()
            return carry
        lax.fori_loop(0, blk, issue, 0)
        wait_all(scatter)


def _expert_ffn(blk_e, nact, next_e, row_slot, hf_tiles, w1, b1, w2, b2, n_tok):
    blk = FFN_BLK
    nblk = row_slot.shape[0] // blk
    n_slots = n_tok * TOP_K
    assert n_tok & (n_tok - 1) == 0
    _, d, two_ff = w1.shape
    d_ff = two_ff // 2
    kern = functools.partial(_ffn_kernel, blk=blk, d_ff=d_ff, n_tok=n_tok)
    smem_blk = lambda f: pl.BlockSpec((blk,), f, memory_space=pltpu.SMEM)
    grid_spec = pltpu.PrefetchScalarGridSpec(
        num_scalar_prefetch=3,
        grid=(nblk,),
        in_specs=[smem_blk(lambda i, be, na, ne: (i,)),
                  smem_blk(lambda i, be, na, ne: (jnp.minimum(i + 1, nblk - 1),)),
                  smem_blk(lambda i, be, na, ne: (jnp.maximum(i - 1, 0),)),
                  pl.BlockSpec(memory_space=pl.ANY),
                  pl.BlockSpec(memory_space=pl.ANY),
                  pl.BlockSpec((1, 1, two_ff), lambda i, be, na, ne: (be[i], 0, 0)),
                  pl.BlockSpec(memory_space=pl.ANY),
                  pl.BlockSpec((1, 1, d), lambda i, be, na, ne: (be[i], 0, 0))],
        out_specs=pl.BlockSpec(memory_space=pl.ANY),
        scratch_shapes=[pltpu.VMEM((2, blk * SUBLANES, LANES), F32),
                        pltpu.VMEM((Y_SLOTS, blk * SUBLANES, LANES), F32),
                        pltpu.VMEM((2, d, two_ff), F32),
                        pltpu.VMEM((2, d_ff, d), F32),
                        pltpu.VMEM((d, two_ff), BF16),
                        pltpu.VMEM((d_ff, d), BF16),
                        pltpu.SMEM((1,), jnp.int32),
                        pltpu.SemaphoreType.DMA((2,)),
                        pltpu.SemaphoreType.DMA(()),
                        pltpu.SemaphoreType.DMA((2,))],
    )
    return pl.pallas_call(
        kern,
        grid_spec=grid_spec,
        out_shape=jax.ShapeDtypeStruct(((n_slots + blk) * SUBLANES, LANES), F32),
        compiler_params=_params(("arbitrary",)),
        name="expert_ffn",
    )(blk_e, nact, next_e, row_slot, row_slot, row_slot, hf_tiles, w1, b1.reshape(N_EXPERTS, 1, two_ff), w2,
      b2.reshape(N_EXPERTS, 1, d))


def _combine_kernel(x2_ref, tg_ref, gfin_ref, *rest, tm, final_norm):
    y_refs, o_ref = rest[:TOP_K], rest[TOP_K]
    gates = tg_ref[...]
    pieces = []
    for j in range(SUBLANES):
        acc = x2_ref[:, j * LANES:(j + 1) * LANES]
        for k in range(TOP_K):
            acc = acc + gates[:, k:k + 1] * y_refs[k][pl.ds(j, tm, stride=SUBLANES), :]
        pieces.append(acc)
    out = jnp.concatenate(pieces, axis=1)
    if final_norm:
        out = _rms(out, gfin_ref[...])
    o_ref[...] = out


def _combine(x2, gates, g_final, y_slots, final_norm):
    n, d = x2.shape
    tm = COMB_TM
    kern = functools.partial(_combine_kernel, tm=tm, final_norm=final_norm)
    tiles = n // tm

    def choice_spec(k):
        return pl.BlockSpec((tm * SUBLANES, LANES), lambda i: (k * tiles + i, 0))

    return pl.pallas_call(
        kern,
        grid=(tiles,),
        in_specs=[pl.BlockSpec((tm, d), lambda i: (i, 0)),
                  pl.BlockSpec((tm, SUBLANES), lambda i: (i, 0)),
                  pl.BlockSpec((1, d), lambda i: (0, 0))] + [choice_spec(k) for k in range(TOP_K)],
        out_specs=pl.BlockSpec((tm, d), lambda i: (i, 0)),
        out_shape=jax.ShapeDtypeStruct((n, d), F32),
        compiler_params=_params(("parallel",)),
        name="combine",
    )(x2, gates, g_final.reshape(1, d), *([y_slots] * TOP_K))


def _invert_kernel(pad_lo_ref, pad_hi_ref, dest_ref, slot_ref, *, chunk, n_tok, blk):
    n_assign = n_tok * TOP_K
    c = pl.program_id(0)

    @pl.when(c == 0)
    def _():
        def expert(e, carry):
            def pad_row(r, carry2):
                slot_ref[r] = n_assign + (r & (blk - 1))
                return carry2
            lax.fori_loop(pad_lo_ref[e], pad_hi_ref[e], pad_row, 0)
            return carry
        lax.fori_loop(0, N_EXPERTS, expert, 0)

    def body(g, carry):
        a0 = g * INVERT_UNROLL
        tok0 = lax.shift_right_logical(c * chunk + a0, TOP_K.bit_length() - 1)
        for u in range(INVERT_UNROLL):
            slot_ref[dest_ref[a0 + u]] = tok0 + u // TOP_K + (u % TOP_K) * n_tok
        return carry
    lax.fori_loop(0, chunk // INVERT_UNROLL, body, 0)


def _invert_routing(dest, pad_lo, pad_hi, p_rows):
    n_assign = dest.shape[0]
    chunk = INVERT_CHUNK
    blk = FFN_BLK
    assert n_assign % chunk == 0 and chunk % INVERT_UNROLL == 0 and blk & (blk - 1) == 0
    assert INVERT_UNROLL % TOP_K == 0 and TOP_K & (TOP_K - 1) == 0
    kern = functools.partial(_invert_kernel, chunk=chunk, n_tok=n_assign // TOP_K, blk=blk)
    whole_smem = pl.BlockSpec(memory_space=pltpu.SMEM)
    grid_spec = pltpu.PrefetchScalarGridSpec(
        num_scalar_prefetch=2,
        grid=(n_assign // chunk,),
        in_specs=[pl.BlockSpec((chunk,), lambda c, lo, hi: (c,), memory_space=pltpu.SMEM)],
        out_specs=whole_smem,
    )
    return pl.pallas_call(
        kern,
        grid_spec=grid_spec,
        out_shape=jax.ShapeDtypeStruct((p_rows,), jnp.int32),
        compiler_params=_params(("arbitrary",)),
        name="invert_routing",
    )(pad_lo, pad_hi, dest)


def _routing_tables(counts_f32, top_i, rank, n_tok):
    blk = FFN_BLK
    n_assign = n_tok * TOP_K
    counts = counts_f32.astype(jnp.int32)
    pcounts = ((counts + blk - 1) // blk) * blk
    pends = jnp.cumsum(pcounts)
    pstarts = pends - pcounts
    dest = (pstarts[top_i] + rank).reshape(-1)
    p_rows = n_assign + N_EXPERTS * blk
    nblk = p_rows // blk
    pad_lo = (pstarts + counts).astype(jnp.int32)
    pad_hi = pends.at[N_EXPERTS - 1].set(p_rows).astype(jnp.int32)
    row_slot = _invert_routing(dest.astype(jnp.int32), pad_lo, pad_hi, p_rows)
    blk_start = jnp.arange(nblk, dtype=jnp.int32) * blk
    blk_e = jnp.minimum(jnp.sum(blk_start[:, None] >= pends[None, :], axis=1), N_EXPERTS - 1).astype(jnp.int32)
    nact = (pends[-1] // blk).astype(jnp.int32).reshape(1)
    e_ids = jnp.arange(N_EXPERTS, dtype=jnp.int32)
    later_active = jnp.where((counts[None, :] > 0) & (e_ids[None, :] > e_ids[:, None]), e_ids[None, :], N_EXPERTS)
    next_e = jnp.min(later_active, axis=1)
    next_e = jnp.where(next_e == N_EXPERTS, -1, next_e).astype(jnp.int32)
    return row_slot, blk_e, nact, next_e


def _lambda_init(layer):
    return 0.8 - 0.6 * math.exp(-0.3 * layer)


def kernel(x, mem, norm_mix_g, w_in, lambda_q1, lambda_k1, lambda_q2, lambda_k2, subln_g, rpb, w_out,
           norm_mem_q_g, norm_mem_kv_g, w_mq, w_mkv, w_mo, norm_ffn_g, w_router, b_router, w1, b1, w2, b2,
           norm_final_g):
    b, s, d = x.shape
    n = b * s
    depth = w_in.shape[0]
    assert d == SUBLANES * LANES and s % GRID_W == 0
    slopes = jnp.asarray([2.0 ** (-8.0 * (i + 1) / DIFF_HEADS) for i in range(DIFF_HEADS)], F32)
    x2d = x.reshape(n, d)
    for l in range(depth):
        lam_init = _lambda_init(l)
        lam = (jnp.exp(jnp.sum(lambda_q1[l] * lambda_k1[l])) - jnp.exp(jnp.sum(lambda_q2[l] * lambda_k2[l]))
               + lam_init).astype(F32)
        qkv = _qkv_proj(x2d, norm_mix_g[l], w_in[l].astype(BF16))
        qkv3 = qkv.reshape(b, s, qkv.shape[1])
        o_diff = _diff_attn(qkv3, lam, slopes, subln_g[l], lam_init).reshape(n, DIFF_WIDTH)
        o_na = _na_attn(qkv3, _na_bias_table(rpb[l], s // GRID_W)).reshape(n, NA_WIDTH)
        kv = _mem_kv(mem, norm_mem_kv_g[l], w_mkv[l].astype(BF16))
        wr_t = w_router[l].T
        wr_hi_t = wr_t.astype(BF16)
        wr_lo_t = (wr_t - wr_hi_t.astype(F32)).astype(BF16)
        x2, hf_tiles, ti, tg, tr, counts = _post_attn(
            x2d, o_diff, o_na, w_out[l].astype(BF16), norm_mem_q_g[l], w_mq[l].astype(BF16), kv,
            w_mo[l].astype(BF16), norm_ffn_g[l], wr_hi_t, wr_lo_t, b_router[l], s)
        row_slot, blk_e, nact, next_e = _routing_tables(counts[:, 0], ti[:TOP_K].T, tr[:TOP_K].T, n)
        y_slots = _expert_ffn(blk_e, nact, next_e, row_slot, hf_tiles, w1[l], b1[l], w2[l], b2[l], n)
        x2d = _combine(x2, tg.T, norm_final_g, y_slots, final_norm=(l == depth - 1))
    return x2d.reshape(b, s, d)
```

```python
import functools
import math

import jax
import jax.numpy as jnp
import numpy as np
from jax import lax
from jax.experimental import pallas as pl
from jax.experimental.pallas import tpu as pltpu

HEAD_DIM = 64
DIFF_HEADS = 4
DIFF_VDIM = 2 * HEAD_DIM
DIFF_QK_WIDTH = DIFF_HEADS * 2 * HEAD_DIM
DIFF_WIDTH = DIFF_HEADS * DIFF_VDIM
NA_HEADS = 8
NA_WIDTH = NA_HEADS * HEAD_DIM
GRID_W = 64
NA_ROWS = 8
NA_COLS = 16
MEM_HEADS = 4
MEM_HEAD_DIM = 64
MEM_WIDTH = MEM_HEADS * MEM_HEAD_DIM
N_EXPERTS = 32
TOP_K = 4
SWIGLU_LIMIT = 7.0
SWIGLU_ALPHA = 1.702
RMS_EPS = 1e-5

LANES = 128
SUBLANES = 8
NEG_BIG = -1e30

F32 = jnp.float32
BF16 = jnp.bfloat16

QKV_TM = 512
DIFF_TQ = 512
DIFF_TK = 512
DIFF_LANE_PARTS = 4
NA_R = 16
POST_TM = 512
POST_PARTS = 2
FFN_BLK = 256
COMB_TM = 256
WAITS_PER_TRIP = 16
INVERT_CHUNK = 8192
INVERT_UNROLL = 16
FFN_CHUNKS = 4
Y_SLOTS = 3
VMEM_LIMIT = 56 * 1024 * 1024


def _params(sem):
    return pltpu.CompilerParams(dimension_semantics=sem, vmem_limit_bytes=VMEM_LIMIT)


def _rms(x, g):
    return x * lax.rsqrt(jnp.mean(x * x, axis=-1, keepdims=True) + RMS_EPS) * g


def _dot(a, b):
    return jnp.dot(a, b, preferred_element_type=F32)


def _dot_nt(a, b):
    return lax.dot_general(a, b, (((1,), (1,)), ((), ())), preferred_element_type=F32)


def _rows_to_tiles(ref, val, rows):
    for j in range(SUBLANES):
        ref[pl.ds(j, rows, stride=SUBLANES), :] = val[:, j * LANES:(j + 1) * LANES]


def _tiles_to_rows(ref, rows, lead=None):
    if lead is None:
        parts = [ref[pl.ds(j, rows, stride=SUBLANES), :] for j in range(SUBLANES)]
    else:
        parts = [ref[lead, pl.ds(j, rows, stride=SUBLANES), :] for j in range(SUBLANES)]
    return jnp.concatenate(parts, axis=1)


def _qkv_kernel(x_ref, g_ref, w_ref, o_ref):
    h = _rms(x_ref[...], g_ref[...]).astype(BF16)
    cols = o_ref.shape[1]
    step = 512
    for c in range(cols // step):
        o_ref[:, c * step:(c + 1) * step] = _dot(h, w_ref[:, c * step:(c + 1) * step]).astype(o_ref.dtype)


def _qkv_proj(x2d, g, w_bf16):
    n, d = x2d.shape
    cols = w_bf16.shape[1]
    return pl.pallas_call(
        _qkv_kernel,
        grid=(n // QKV_TM,),
        in_specs=[pl.BlockSpec((QKV_TM, d), lambda i: (i, 0)),
                  pl.BlockSpec((1, d), lambda i: (0, 0)),
                  pl.BlockSpec((d, cols), lambda i: (0, 0))],
        out_specs=pl.BlockSpec((QKV_TM, cols), lambda i: (i, 0)),
        out_shape=jax.ShapeDtypeStruct((n, cols), BF16),
        compiler_params=_params(("parallel",)),
        name="qkv_proj",
    )(x2d, g.reshape(1, d), w_bf16)


ALIBI_AUG_LEFT, ALIBI_AUG_RIGHT, ALIBI_AUG_NONE = 0, 1, 2
ONES_ROWS = 16
EXP_UNDERFLOW = 105.0
NORM_BOUND_SLACK = 1.02


def _bf16_exact_split(n):
    lo_bits = max(0, int(n.max()).bit_length() - 8)
    lo = n % (1 << lo_bits)
    return n - lo, lo


def _alibi_aug_tables(tq, tk):
    ii_hi, ii_lo = _bf16_exact_split(np.arange(tq))
    jj_hi, jj_lo = _bf16_exact_split(np.arange(tk))
    qa = np.zeros((DIFF_HEADS, tq, LANES), np.float32)
    ka = np.zeros((DIFF_HEADS, 3, tk, LANES), np.float32)
    for h in range(DIFF_HEADS):
        sl = 2.0 ** (-8.0 * (h + 1) / DIFF_HEADS)
        qa[h, :, 0], qa[h, :, 1], qa[h, :, 2], qa[h, :, 3] = -sl * ii_hi, -sl * ii_lo, 1.0, 1.0
        qa[h, :, 4], qa[h, :, 5], qa[h, :, 6], qa[h, :, 7] = sl * ii_hi, sl * ii_lo, 1.0, 1.0
        ka[h, ALIBI_AUG_LEFT, :, 0], ka[h, ALIBI_AUG_LEFT, :, 1] = 1.0, 1.0
        ka[h, ALIBI_AUG_LEFT, :, 2], ka[h, ALIBI_AUG_LEFT, :, 3] = sl * jj_hi, sl * jj_lo
        ka[h, ALIBI_AUG_RIGHT, :, 4], ka[h, ALIBI_AUG_RIGHT, :, 5] = 1.0, 1.0
        ka[h, ALIBI_AUG_RIGHT, :, 6], ka[h, ALIBI_AUG_RIGHT, :, 7] = -sl * jj_hi, -sl * jj_lo
    return jnp.asarray(qa, BF16), jnp.asarray(ka, BF16)


def _diff_kernel(lam_ref, slope_ref, q_ref, k_ref, vt_ref, qaug_ref, kaug_ref, g_ref, o_ref,
                 qa_sc, s0_sc, s1_sc, mb_sc, m_sc, acc_sc, ksq_sc, *, tq, tk, seq, out_scale):
    h = pl.program_id(1)
    i = pl.program_id(2)
    slope = slope_ref[h]
    lam = lam_ref[0]
    nk = seq // tk
    i0 = i * tq
    jd = i0 // tk

    sel_r = lax.broadcasted_iota(jnp.int32, (LANES, LANES), 0) // HEAD_DIM
    sel_c = lax.broadcasted_iota(jnp.int32, (LANES, LANES), 1)
    sel = (sel_r == sel_c).astype(BF16)

    def max_sq_norm(x):
        xf = x.astype(F32)
        return jnp.max(_dot((xf * xf).astype(BF16), sel), axis=0, keepdims=True)

    @pl.when(i == 0)
    def _():
        ksq_sc[...] = max_sq_norm(k_ref[0])

    q = q_ref[0] * jnp.asarray(1.0 / math.sqrt(HEAD_DIM), BF16)
    lane = lax.broadcasted_iota(jnp.int32, (tq, LANES), 1)
    zero = jnp.zeros_like(q)
    qa_sc[:tq, :LANES] = jnp.where(lane < HEAD_DIM, q, zero)
    qa_sc[tq:, :LANES] = jnp.where(lane >= HEAD_DIM, q, zero)
    qa_sc[:tq, LANES:] = qaug_ref[0]
    qa_sc[tq:, LANES:] = qaug_ref[0]
    m_sc[...] = jnp.full(m_sc.shape, NEG_BIG, F32)
    acc_sc[...] = jnp.zeros(acc_sc.shape, F32)

    part = 2 * tq // DIFF_LANE_PARTS
    maps = tuple(slice(c * part, (c + 1) * part) for c in range(DIFF_LANE_PARTS))

    def scores(j, side, mp):
        start = pl.multiple_of(j * tk, tk)
        kba = jnp.concatenate([k_ref[0, pl.ds(start, tk), :], kaug_ref[0, side]], axis=1)
        return _dot_nt(kba, qa_sc[mp, :])

    s_bufs = (s0_sc, s1_sc)

    def stash(slot, mp, st):
        s_bufs[slot][:, mp] = st
        mb_sc[slot, :, mp] = jnp.max(st, axis=0, keepdims=True)

    def consume(slot, mp, j, c):
        m_prev = m_sc[:, mp]
        m_new = jnp.maximum(m_prev, mb_sc[slot, :, mp] + c)
        alpha = jnp.exp(m_prev - m_new)
        p_t = jnp.exp(s_bufs[slot][:, mp] - (m_new - c)).astype(BF16)
        start = pl.multiple_of(j * tk, tk)
        vta = jnp.concatenate([vt_ref[0, :, pl.ds(start, tk)], jnp.ones((ONES_ROWS, tk), BF16)], axis=0)
        acc_sc[:, mp] = alpha * acc_sc[:, mp] + _dot(vta, p_t)
        m_sc[:, mp] = m_new

    def block_const(j):
        return -slope * jnp.abs(i0 - j * tk).astype(F32)

    rel_t = (lax.broadcasted_iota(jnp.int32, (tk, tq), 0) - lax.broadcasted_iota(jnp.int32, (tk, tq), 1)).astype(F32)
    bias_t = -slope * jnp.abs(rel_t + (jd * tk - i0).astype(F32))
    for mp in maps:
        stash(0, mp, scores(jd, ALIBI_AUG_NONE, mp) + bias_t[:, mp.start % tq:mp.start % tq + part])

    qk_bound = jnp.sqrt(jnp.max(max_sq_norm(q) * ksq_sc[...])) * NORM_BOUND_SLACK
    reach = (qk_bound - jnp.min(mb_sc[0]) + EXP_UNDERFLOW) / slope
    reach = jnp.minimum(reach, 2.0 * seq)
    j_lo = jnp.clip(jnp.floor((i0 - tk + 1 - reach) / tk).astype(jnp.int32) + 1, 0, jd)
    j_hi = jnp.clip(jnp.ceil((reach + i0 + tq - 1) / tk).astype(jnp.int32) - 1, jd, nk - 1)
    n_blocks = j_hi - j_lo + 1

    def block_of(t):
        return j_lo + jnp.where(t - 1 < jd - j_lo, t - 1, t)

    def produce(t, slot, mp):
        j = block_of(t)
        stash(slot, mp, scores(j, jnp.where(j < jd, ALIBI_AUG_LEFT, ALIBI_AUG_RIGHT), mp))

    def consume_step(t, slot, mp):
        j = jnp.where(t == 0, jd, block_of(t))
        consume(slot, mp, j, jnp.where(t == 0, 0.0, block_const(j)))

    def stage(t_produce, t_consume, slot_produce, slot_consume):
        for mp in maps:
            if t_produce is not None:
                produce(t_produce, slot_produce, mp)
            consume_step(t_consume, slot_consume, mp)

    def body(u, carry):
        t = 2 * u
        stage(t + 1, t, 1, 0)
        stage(t + 2, t + 1, 0, 1)
        return carry

    def body4(u, carry):
        body(2 * u, carry)
        body(2 * u + 1, carry)
        return carry

    quads = (n_blocks - 1) // 4
    lax.fori_loop(0, quads, body4, 0)
    lax.fori_loop(2 * quads, (n_blocks - 1) // 2, body, 0)

    @pl.when(n_blocks % 2 == 0)
    def _():
        stage(n_blocks - 1, n_blocks - 2, 1, 0)
        stage(None, n_blocks - 1, None, 1)

    @pl.when(n_blocks % 2 == 1)
    def _():
        stage(None, n_blocks - 1, None, 0)

    acc = acc_sc[...]
    o = acc[:DIFF_VDIM] * (1.0 / acc[DIFF_VDIM:DIFF_VDIM + 1])
    d = o[:, :tq] - lam * o[:, tq:]
    y = d * lax.rsqrt(jnp.mean(d * d, axis=0, keepdims=True) + RMS_EPS) * g_ref[...] * out_scale
    o_ref[0] = y.T.astype(o_ref.dtype)


def _diff_attn(qkv3, lam, slopes, subln_g, lam_init):
    b, s, _ = qkv3.shape
    tq, tk = DIFF_TQ, DIFF_TK
    assert tk % tq == 0 and s % (2 * tk) == 0
    kcol = DIFF_QK_WIDTH // LANES
    v_t = qkv3[:, :, 2 * DIFF_QK_WIDTH:2 * DIFF_QK_WIDTH + DIFF_WIDTH].transpose(0, 2, 1)
    qaug, kaug = _alibi_aug_tables(tq, tk)
    kern = functools.partial(_diff_kernel, tq=tq, tk=tk, seq=s, out_scale=1.0 - lam_init)
    return pl.pallas_call(
        kern,
        grid=(b, DIFF_HEADS, s // tq),
        in_specs=[pl.BlockSpec(memory_space=pltpu.SMEM),
                  pl.BlockSpec(memory_space=pltpu.SMEM),
                  pl.BlockSpec((1, tq, LANES), lambda bi, h, i: (bi, i, h)),
                  pl.BlockSpec((1, s, LANES), lambda bi, h, i: (bi, 0, kcol + h)),
                  pl.BlockSpec((1, DIFF_VDIM, s), lambda bi, h, i: (bi, h, 0)),
                  pl.BlockSpec((1, tq, LANES), lambda bi, h, i: (h, 0, 0)),
                  pl.BlockSpec((1, 3, tk, LANES), lambda bi, h, i: (h, 0, 0, 0)),
                  pl.BlockSpec((DIFF_VDIM, 1), lambda bi, h, i: (0, 0))],
        out_specs=pl.BlockSpec((1, tq, LANES), lambda bi, h, i: (bi, i, h)),
        out_shape=jax.ShapeDtypeStruct((b, s, DIFF_WIDTH), BF16),
        scratch_shapes=[pltpu.VMEM((2 * tq, 2 * LANES), BF16),
                        pltpu.VMEM((tk, 2 * tq), F32),
                        pltpu.VMEM((tk, 2 * tq), F32),
                        pltpu.VMEM((2, 1, 2 * tq), F32),
                        pltpu.VMEM((1, 2 * tq), F32),
                        pltpu.VMEM((DIFF_VDIM + ONES_ROWS, 2 * tq), F32),
                        pltpu.VMEM((1, LANES), F32)],
        compiler_params=_params(("parallel", "parallel", "arbitrary")),
        name="diff_attn",
    )(lam.reshape(1), slopes, qkv3, qkv3, v_t, qaug, kaug, subln_g.reshape(DIFF_VDIM, 1))


def _na_bias_table(rpb, rows):
    kr = min(NA_ROWS, rows)
    q = np.arange(GRID_W)[:, None]
    j = np.arange(GRID_W)[None, :]
    cs = np.clip(q - NA_COLS // 2, 0, GRID_W - NA_COLS)
    valid = (j >= cs) & (j < cs + NA_COLS)
    dc = j - q + (NA_COLS - 1)
    sel = ((dc[:, :, None] == np.arange(2 * NA_COLS - 1)[None, None, :]) & valid[:, :, None]).astype(np.float32)
    col = jnp.einsum('hrd,qjd->hqrj', rpb.astype(F32), jnp.asarray(sel), precision=lax.Precision.HIGHEST)
    col = jnp.where(valid[None, :, None, :], col, NEG_BIG)
    cases = [col[:, :, NA_ROWS - 1 - c:NA_ROWS - 1 - c + kr] for c in range(kr)]
    return jnp.stack(cases, axis=0).reshape(kr, NA_HEADS // 2, 2 * GRID_W, kr * GRID_W)


def _na_kernel(q_ref, k_ref, v_ref, bias_ref, o_ref, *, rows_per_step, rows, kr):
    i = pl.program_id(2)
    lane = lax.broadcasted_iota(jnp.int32, (GRID_W, LANES), 1)
    win = kr * GRID_W

    starts, scores, probs = [], [], []
    for rr in range(rows_per_step):
        r = i * rows_per_step + rr
        rs = jnp.clip(r - kr // 2, 0, rows - kr)
        q = q_ref[0, rr * GRID_W:(rr + 1) * GRID_W, :]
        q = q * jnp.asarray(1.0 / math.sqrt(HEAD_DIM), BF16)
        zero = jnp.zeros_like(q)
        qs = jnp.concatenate([jnp.where(lane < HEAD_DIM, q, zero), jnp.where(lane >= HEAD_DIM, q, zero)], axis=0)
        start = pl.multiple_of(rs * GRID_W, GRID_W)
        starts.append(start)
        scores.append(_dot_nt(qs, k_ref[0, pl.ds(start, win), :]) + bias_ref[r - rs, 0])
    for s in scores:
        p = jnp.exp(s - jnp.max(s, axis=1, keepdims=True))
        probs.append((p.astype(BF16), jnp.sum(p, axis=1, keepdims=True)))
    for rr, (p, l) in enumerate(probs):
        o = _dot(p, v_ref[0, pl.ds(starts[rr], win), :]) / l
        out = jnp.where(lane < HEAD_DIM, o[:GRID_W], o[GRID_W:])
        o_ref[0, rr * GRID_W:(rr + 1) * GRID_W, :] = out.astype(o_ref.dtype)


def _na_attn(qkv3, bias_tbl):
    b, s, _ = qkv3.shape
    rows = s // GRID_W
    kr = min(NA_ROWS, rows)
    base = (2 * DIFF_QK_WIDTH + DIFF_WIDTH) // LANES
    pairs = NA_HEADS // 2
    rstep = NA_R
    kern = functools.partial(_na_kernel, rows_per_step=rstep, rows=rows, kr=kr)
    return pl.pallas_call(
        kern,
        grid=(b, pairs, rows // rstep),
        in_specs=[pl.BlockSpec((1, rstep * GRID_W, LANES), lambda bi, p, i: (bi, i, base + p)),
                  pl.BlockSpec((1, s, LANES), lambda bi, p, i: (bi, 0, base + pairs + p)),
                  pl.BlockSpec((1, s, LANES), lambda bi, p, i: (bi, 0, base + 2 * pairs + p)),
                  pl.BlockSpec((kr, 1, 2 * GRID_W, kr * GRID_W), lambda bi, p, i: (0, p, 0, 0))],
        out_specs=pl.BlockSpec((1, rstep * GRID_W, LANES), lambda bi, p, i: (bi, i, p)),
        out_shape=jax.ShapeDtypeStruct((b, s, NA_WIDTH), BF16),
        compiler_params=_params(("parallel", "parallel", "parallel")),
        name="na_attn",
    )(qkv3, qkv3, qkv3, bias_tbl)


def _memkv_kernel(mem_ref, g_ref, w_ref, o_ref):
    h = _rms(mem_ref[0], g_ref[...]).astype(BF16)
    o_ref[0] = _dot(h, w_ref[...]).astype(o_ref.dtype)


def _mem_kv(mem, g, w_bf16):
    b, m, d = mem.shape
    cols = w_bf16.shape[1]
    return pl.pallas_call(
        _memkv_kernel,
        grid=(b,),
        in_specs=[pl.BlockSpec((1, m, d), lambda i: (i, 0, 0)),
                  pl.BlockSpec((1, d), lambda i: (0, 0)),
                  pl.BlockSpec((d, cols), lambda i: (0, 0))],
        out_specs=pl.BlockSpec((1, m, cols), lambda i: (i, 0, 0)),
        out_shape=jax.ShapeDtypeStruct((b, m, cols), BF16),
        compiler_params=_params(("parallel",)),
        name="mem_kv",
    )(mem, g.reshape(1, d), w_bf16)


def _sublane_pack(rows, tm):
    sub = lax.broadcasted_iota(jnp.int32, (SUBLANES, tm), 0)
    out = jnp.zeros((SUBLANES, tm), rows[0].dtype)
    for k, r in enumerate(rows):
        out = jnp.where(sub == k, r, out)
    return out


def _post_kernel(x_ref, od_ref, on_ref, wout_ref, gq_ref, wmq_ref, kv_ref, wmo_ref, gf_ref,
                 wrh_ref, wrl_ref, br_ref,
                 x2_ref, hf_ref, ti_ref, tg_ref, tr_ref, cnt_ref, carry_sc, *, tm, parts):
    step = pl.program_id(0)
    rows = tm // parts

    @pl.when(step == 0)
    def _():
        carry_sc[...] = jnp.zeros(carry_sc.shape, F32)

    head_of_lane = lax.broadcasted_iota(jnp.int32, (rows, MEM_WIDTH), 1) // MEM_HEAD_DIM
    kv = kv_ref[0]

    x1s, qss = [], []
    for g in range(parts):
        rs = slice(g * rows, (g + 1) * rows)
        x1 = (x_ref[rs, :] + _dot(od_ref[rs, :], wout_ref[:DIFF_WIDTH, :])
              + _dot(on_ref[rs, :], wout_ref[DIFF_WIDTH:, :]))
        hm = _rms(x1, gq_ref[...]).astype(BF16)
        q = (_dot(hm, wmq_ref[...]) * (1.0 / math.sqrt(MEM_HEAD_DIM))).astype(BF16)
        zero = jnp.zeros_like(q)
        qss.append(jnp.concatenate([jnp.where(head_of_lane == h, q, zero) for h in range(MEM_HEADS)], axis=0))
        x1s.append(x1)

    probs = []
    for g in range(parts):
        s = _dot_nt(qss[g], kv[:, :MEM_WIDTH])
        p = jnp.exp(s - jnp.max(s, axis=1, keepdims=True))
        probs.append((p.astype(BF16), jnp.sum(p, axis=1, keepdims=True)))

    hfs = []
    for g in range(parts):
        rs = slice(g * rows, (g + 1) * rows)
        p, l = probs[g]
        o4 = _dot(p, kv[:, MEM_WIDTH:]) / l
        o = jnp.zeros((rows, MEM_WIDTH), F32)
        for h in range(MEM_HEADS):
            o = jnp.where(head_of_lane == h, o4[h * rows:(h + 1) * rows], o)
        x2 = x1s[g] + _dot(o.astype(BF16), wmo_ref[...])
        x2_ref[rs, :] = x2
        hf = _rms(x2, gf_ref[...])
        _rows_to_tiles(hf_ref.at[pl.ds(g * rows * SUBLANES, rows * SUBLANES), :], hf, rows)
        hfs.append(hf)

    e_iota = lax.broadcasted_iota(jnp.int32, (N_EXPERTS, rows), 0)
    r_i = lax.broadcasted_iota(jnp.int32, (rows, rows), 0)
    c_i = lax.broadcasted_iota(jnp.int32, (rows, rows), 1)
    earlier = (r_i < c_i).astype(BF16)
    carry = carry_sc[...]
    idx_rows, gate_rows, rank_rows = [], [], []
    for g in range(parts):
        hi = hfs[g].astype(BF16)
        lo = (hfs[g] - hi.astype(F32)).astype(BF16)
        logits = _dot_nt(wrh_ref[...], hi) + _dot_nt(wrl_ref[...], hi) + _dot_nt(wrh_ref[...], lo) + br_ref[...]
        vals, idxs, hots = [], [], []
        cur = logits
        for _ in range(TOP_K):
            mx = jnp.max(cur, axis=0, keepdims=True)
            idx = jnp.min(jnp.where(cur == mx, e_iota, N_EXPERTS), axis=0, keepdims=True)
            hot = e_iota == idx
            vals.append(mx)
            idxs.append(idx)
            hots.append(hot)
            cur = jnp.where(hot, -jnp.inf, cur)
        exps = [jnp.exp(v - vals[0]) for v in vals]
        den = exps[0] + exps[1] + exps[2] + exps[3]
        chosen = jnp.zeros((N_EXPERTS, rows), F32)
        for hot in hots:
            chosen = chosen + hot.astype(F32)
        before = _dot(chosen.astype(BF16), earlier) + carry
        carry = carry + jnp.sum(chosen, axis=1, keepdims=True)
        idx_rows.append(idxs)
        gate_rows.append([e / den for e in exps])
        rank_rows.append([jnp.sum(jnp.where(hot, before, 0.0), axis=0, keepdims=True).astype(jnp.int32)
                          for hot in hots])
    carry_sc[...] = carry
    cnt_ref[...] = carry

    def tile_rows(per_part):
        return [jnp.concatenate([per_part[g][k] for g in range(parts)], axis=1) for k in range(TOP_K)]

    ti_ref[...] = _sublane_pack(tile_rows(idx_rows), tm)
    tg_ref[...] = _sublane_pack(tile_rows(gate_rows), tm)
    tr_ref[...] = _sublane_pack(tile_rows(rank_rows), tm)


def _post_attn(x2d, o_diff, o_na, w_out, gq, w_mq, kv, w_mo, gf, wr_hi_t, wr_lo_t, b_router, seq):
    n, d = x2d.shape
    tm = POST_TM
    m_tok = kv.shape[1]
    const = lambda i: (0, 0)
    row = lambda i: (i, 0)
    col = lambda i: (0, i)
    kern = functools.partial(_post_kernel, tm=tm, parts=POST_PARTS)
    return pl.pallas_call(
        kern,
        grid=(n // tm,),
        in_specs=[pl.BlockSpec((tm, d), row),
                  pl.BlockSpec((tm, DIFF_WIDTH), row),
                  pl.BlockSpec((tm, NA_WIDTH), row),
                  pl.BlockSpec(w_out.shape, const),
                  pl.BlockSpec((1, d), const),
                  pl.BlockSpec(w_mq.shape, const),
                  pl.BlockSpec((1, m_tok, 2 * MEM_WIDTH), lambda i: ((i * tm) // seq, 0, 0)),
                  pl.BlockSpec(w_mo.shape, const),
                  pl.BlockSpec((1, d), const),
                  pl.BlockSpec(wr_hi_t.shape, const),
                  pl.BlockSpec(wr_lo_t.shape, const),
                  pl.BlockSpec((N_EXPERTS, 1), const)],
        out_specs=[pl.BlockSpec((tm, d), row),
                   pl.BlockSpec((tm * SUBLANES, LANES), row),
                   pl.BlockSpec((SUBLANES, tm), col),
                   pl.BlockSpec((SUBLANES, tm), col),
                   pl.BlockSpec((SUBLANES, tm), col),
                   pl.BlockSpec((N_EXPERTS, 1), const)],
        out_shape=[jax.ShapeDtypeStruct((n, d), F32),
                   jax.ShapeDtypeStruct((n * SUBLANES, LANES), F32),
                   jax.ShapeDtypeStruct((SUBLANES, n), jnp.int32),
                   jax.ShapeDtypeStruct((SUBLANES, n), F32),
                   jax.ShapeDtypeStruct((SUBLANES, n), jnp.int32),
                   jax.ShapeDtypeStruct((N_EXPERTS, 1), F32)],
        scratch_shapes=[pltpu.VMEM((N_EXPERTS, 1), F32)],
        compiler_params=_params(("arbitrary",)),
        name="post_attn",
    )(x2d, o_diff, o_na, w_out, gq.reshape(1, d), w_mq, kv, w_mo, gf.reshape(1, d), wr_hi_t, wr_lo_t,
      b_router.reshape(N_EXPERTS, 1))


def _row_copy(src, src_row, dst, dst_row, sem):
    return pltpu.make_async_copy(
        src.at[pl.ds(pl.multiple_of(src_row * SUBLANES, SUBLANES), SUBLANES), :],
        dst.at[pl.ds(pl.multiple_of(dst_row * SUBLANES, SUBLANES), SUBLANES), :],
        sem)


def _ffn_kernel(blk_e_ref, nact_ref, next_e_ref, slot_cur_ref, slot_next_ref, slot_prev_ref, hf_ref,
                w1_ref, b1_ref, w2_ref, b2_ref, yc_ref,
                xbuf, ybuf, w1f_sc, w2f_sc, w1b_sc, w2b_sc, wslot_sc, gsem, ssem, wsem, *, blk, d_ff, n_tok):
    i = pl.program_id(0)
    nact = nact_ref[0]
    slot = i % 2
    y_slot = i % Y_SLOTS
    y_prev = (i + Y_SLOTS - 1) % Y_SLOTS

    trash_row = n_tok * TOP_K

    def gather(slot_ref, t, to_slot):
        tok = slot_ref[t] & (n_tok - 1)
        return _row_copy(hf_ref, tok, xbuf.at[to_slot], t, gsem.at[to_slot])

    def scatter(t):
        dst = jnp.where(i == 0, trash_row + t, slot_prev_ref[t])
        return _row_copy(ybuf.at[y_prev], t, yc_ref, dst, ssem)

    def wait_all(copy_of_row):
        def drain(g, carry):
            for u in range(WAITS_PER_TRIP):
                copy_of_row(g * WAITS_PER_TRIP + u).wait()
            return carry
        lax.fori_loop(0, blk // WAITS_PER_TRIP, drain, 0)

    @pl.when(i == 0)
    def _():
        ybuf[...] = jnp.zeros(ybuf.shape, F32)
        for t in range(blk):
            gather(slot_cur_ref, t, 0).start(priority=t % 2)

    def weight_copies(e, s):
        return (pltpu.make_async_copy(w1_ref.at[e], w1f_sc.at[s], wsem.at[s]),
                pltpu.make_async_copy(w2_ref.at[e], w2f_sc.at[s], wsem.at[s]))

    @pl.when(i == 0)
    def _():
        wslot_sc[0] = 0
        for cp in weight_copies(blk_e_ref[0], 0):
            cp.start(priority=1)

    @pl.when((i < nact) & ((i == 0) | (blk_e_ref[i] != blk_e_ref[jnp.maximum(i - 1, 0)])))
    def _():
        e = blk_e_ref[i]
        s = wslot_sc[0]
        for cp in weight_copies(e, s):
            cp.wait()
        nxt = next_e_ref[e]

        @pl.when(nxt >= 0)
        def _():
            for cp in weight_copies(nxt, 1 - s):
                cp.start(priority=1)

        rows = 256
        for c in range(w1b_sc.shape[0] // rows):
            w1b_sc[c * rows:(c + 1) * rows, :] = w1f_sc[s, c * rows:(c + 1) * rows, :].astype(BF16)
        for c in range(w2b_sc.shape[0] // rows):
            w2b_sc[c * rows:(c + 1) * rows, :] = w2f_sc[s, c * rows:(c + 1) * rows, :].astype(BF16)
        wslot_sc[0] = 1 - s

    @pl.when(i <= nact)
    def _():
        wait_all(lambda t: gather(slot_cur_ref, t, slot))

    @pl.when((i >= 1) & (i <= nact))
    def _():
        wait_all(scatter)

    @pl.when(i < nact)
    def _():
        for t in range(blk):
            gather(slot_next_ref, t, 1 - slot).start(priority=0)
            scatter(t).start(priority=(t + 1) % 2)
        x = _tiles_to_rows(xbuf, blk, lead=slot).astype(BF16)
        cw = d_ff // FFN_CHUNKS
        pre = []
        for c in range(FFN_CHUNKS):
            g_cols, l_cols = slice(c * cw, (c + 1) * cw), slice(d_ff + c * cw, d_ff + (c + 1) * cw)
            pre.append((_dot(x, w1b_sc[:, g_cols]) + b1_ref[0, :, g_cols],
                        _dot(x, w1b_sc[:, l_cols]) + b1_ref[0, :, l_cols]))
        y = b2_ref[0]
        for c, (gate, lin) in enumerate(pre):
            gate = jnp.minimum(gate, SWIGLU_LIMIT)
            lin = jnp.clip(lin, -SWIGLU_LIMIT, SWIGLU_LIMIT)
            act = gate * jax.nn.sigmoid(SWIGLU_ALPHA * gate) * (lin + 1.0)
            y = y + _dot(act.astype(BF16), w2b_sc[c * cw:(c + 1) * cw, :])
        _rows_to_tiles(ybuf.at[y_slot], y, blk)

    @pl.when(i == nact)
    def _():
        def issue(t, carry):
            scatter(t).start()
            return carry
        lax.fori_loop(0, blk, issue, 0)
        wait_all(scatter)


def _expert_ffn(blk_e, nact, next_e, row_slot, hf_tiles, w1, b1, w2, b2, n_tok):
    blk = FFN_BLK
    nblk = row_slot.shape[0] // blk
    n_slots = n_tok * TOP_K
    assert n_tok & (n_tok - 1) == 0
    _, d, two_ff = w1.shape
    d_ff = two_ff // 2
    kern = functools.partial(_ffn_kernel, blk=blk, d_ff=d_ff, n_tok=n_tok)
    smem_blk = lambda f: pl.BlockSpec((blk,), f, memory_space=pltpu.SMEM)
    grid_spec = pltpu.PrefetchScalarGridSpec(
        num_scalar_prefetch=3,
        grid=(nblk,),
        in_specs=[smem_blk(lambda i, be, na, ne: (i,)),
                  smem_blk(lambda i, be, na, ne: (jnp.minimum(i + 1, nblk - 1),)),
                  smem_blk(lambda i, be, na, ne: (jnp.maximum(i - 1, 0),)),
                  pl.BlockSpec(memory_space=pl.ANY),
                  pl.BlockSpec(memory_space=pl.ANY),
                  pl.BlockSpec((1, 1, two_ff), lambda i, be, na, ne: (be[i], 0, 0)),
                  pl.BlockSpec(memory_space=pl.ANY),
                  pl.BlockSpec((1, 1, d), lambda i, be, na, ne: (be[i], 0, 0))],
        out_specs=pl.BlockSpec(memory_space=pl.ANY),
        scratch_shapes=[pltpu.VMEM((2, blk * SUBLANES, LANES), F32),
                        pltpu.VMEM((Y_SLOTS, blk * SUBLANES, LANES), F32),
                        pltpu.VMEM((2, d, two_ff), F32),
                        pltpu.VMEM((2, d_ff, d), F32),
                        pltpu.VMEM((d, two_ff), BF16),
                        pltpu.VMEM((d_ff, d), BF16),
                        pltpu.SMEM((1,), jnp.int32),
                        pltpu.SemaphoreType.DMA((2,)),
                        pltpu.SemaphoreType.DMA(()),
                        pltpu.SemaphoreType.DMA((2,))],
    )
    return pl.pallas_call(
        kern,
        grid_spec=grid_spec,
        out_shape=jax.ShapeDtypeStruct(((n_slots + blk) * SUBLANES, LANES), F32),
        compiler_params=_params(("arbitrary",)),
        name="expert_ffn",
    )(blk_e, nact, next_e, row_slot, row_slot, row_slot, hf_tiles, w1, b1.reshape(N_EXPERTS, 1, two_ff), w2,
      b2.reshape(N_EXPERTS, 1, d))


def _combine_kernel(x2_ref, tg_ref, gfin_ref, *rest, tm, final_norm):
    y_refs, o_ref = rest[:TOP_K], rest[TOP_K]
    gates = tg_ref[...]
    pieces = []
    for j in range(SUBLANES):
        acc = x2_ref[:, j * LANES:(j + 1) * LANES]
        for k in range(TOP_K):
            acc = acc + gates[:, k:k + 1] * y_refs[k][pl.ds(j, tm, stride=SUBLANES), :]
        pieces.append(acc)
    out = jnp.concatenate(pieces, axis=1)
    if final_norm:
        out = _rms(out, gfin_ref[...])
    o_ref[...] = out


def _combine(x2, gates, g_final, y_slots, final_norm):
    n, d = x2.shape
    tm = COMB_TM
    kern = functools.partial(_combine_kernel, tm=tm, final_norm=final_norm)
    tiles = n // tm

    def choice_spec(k):
        return pl.BlockSpec((tm * SUBLANES, LANES), lambda i: (k * tiles + i, 0))

    return pl.pallas_call(
        kern,
        grid=(tiles,),
        in_specs=[pl.BlockSpec((tm, d), lambda i: (i, 0)),
                  pl.BlockSpec((tm, SUBLANES), lambda i: (i, 0)),
                  pl.BlockSpec((1, d), lambda i: (0, 0))] + [choice_spec(k) for k in range(TOP_K)],
        out_specs=pl.BlockSpec((tm, d), lambda i: (i, 0)),
        out_shape=jax.ShapeDtypeStruct((n, d), F32),
        compiler_params=_params(("parallel",)),
        name="combine",
    )(x2, gates, g_final.reshape(1, d), *([y_slots] * TOP_K))


def _invert_kernel(pad_lo_ref, pad_hi_ref, dest_ref, slot_ref, *, chunk, n_tok, blk):
    n_assign = n_tok * TOP_K
    c = pl.program_id(0)

    @pl.when(c == 0)
    def _():
        def expert(e, carry):
            def pad_row(r, carry2):
                slot_ref[r] = n_assign + (r & (blk - 1))
                return carry2
            lax.fori_loop(pad_lo_ref[e], pad_hi_ref[e], pad_row, 0)
            return carry
        lax.fori_loop(0, N_EXPERTS, expert, 0)

    def body(g, carry):
        a0 = g * INVERT_UNROLL
        tok0 = lax.shift_right_logical(c * chunk + a0, TOP_K.bit_length() - 1)
        for u in range(INVERT_UNROLL):
            slot_ref[dest_ref[a0 + u]] = tok0 + u // TOP_K + (u % TOP_K) * n_tok
        return carry
    lax.fori_loop(0, chunk // INVERT_UNROLL, body, 0)


def _invert_routing(dest, pad_lo, pad_hi, p_rows):
    n_assign = dest.shape[0]
    chunk = INVERT_CHUNK
    blk = FFN_BLK
    assert n_assign % chunk == 0 and chunk % INVERT_UNROLL == 0 and blk & (blk - 1) == 0
    assert INVERT_UNROLL % TOP_K == 0 and TOP_K & (TOP_K - 1) == 0
    kern = functools.partial(_invert_kernel, chunk=chunk, n_tok=n_assign // TOP_K, blk=blk)
    whole_smem = pl.BlockSpec(memory_space=pltpu.SMEM)
    grid_spec = pltpu.PrefetchScalarGridSpec(
        num_scalar_prefetch=2,
        grid=(n_assign // chunk,),
        in_specs=[pl.BlockSpec((chunk,), lambda c, lo, hi: (c,), memory_space=pltpu.SMEM)],
        out_specs=whole_smem,
    )
    return pl.pallas_call(
        kern,
        grid_spec=grid_spec,
        out_shape=jax.ShapeDtypeStruct((p_rows,), jnp.int32),
        compiler_params=_params(("arbitrary",)),
        name="invert_routing",
    )(pad_lo, pad_hi, dest)


def _routing_tables(counts_f32, top_i, rank, n_tok):
    blk = FFN_BLK
    n_assign = n_tok * TOP_K
    counts = counts_f32.astype(jnp.int32)
    pcounts = ((counts + blk - 1) // blk) * blk
    pends = jnp.cumsum(pcounts)
    pstarts = pends - pcounts
    dest = (pstarts[top_i] + rank).reshape(-1)
    p_rows = n_assign + N_EXPERTS * blk
    nblk = p_rows // blk
    pad_lo = (pstarts + counts).astype(jnp.int32)
    pad_hi = pends.at[N_EXPERTS - 1].set(p_rows).astype(jnp.int32)
    row_slot = _invert_routing(dest.astype(jnp.int32), pad_lo, pad_hi, p_rows)
    blk_start = jnp.arange(nblk, dtype=jnp.int32) * blk
    blk_e = jnp.minimum(jnp.sum(blk_start[:, None] >= pends[None, :], axis=1), N_EXPERTS - 1).astype(jnp.int32)
    nact = (pends[-1] // blk).astype(jnp.int32).reshape(1)
    e_ids = jnp.arange(N_EXPERTS, dtype=jnp.int32)
    later_active = jnp.where((counts[None, :] > 0) & (e_ids[None, :] > e_ids[:, None]), e_ids[None, :], N_EXPERTS)
    next_e = jnp.min(later_active, axis=1)
    next_e = jnp.where(next_e == N_EXPERTS, -1, next_e).astype(jnp.int32)
    return row_slot, blk_e, nact, next_e


def _lambda_init(layer):
    return 0.8 - 0.6 * math.exp(-0.3 * layer)


def kernel(x, mem, norm_mix_g, w_in, lambda_q1, lambda_k1, lambda_q2, lambda_k2, subln_g, rpb, w_out,
           norm_mem_q_g, norm_mem_kv_g, w_mq, w_mkv, w_mo, norm_ffn_g, w_router, b_router, w1, b1, w2, b2,
           norm_final_g):
    b, s, d = x.shape
    n = b * s
    depth = w_in.shape[0]
    assert d == SUBLANES * LANES and s % GRID_W == 0
    slopes = jnp.asarray([2.0 ** (-8.0 * (i + 1) / DIFF_HEADS) for i in range(DIFF_HEADS)], F32)
    x2d = x.reshape(n, d)
    for l in range(depth):
        lam_init = _lambda_init(l)
        lam = (jnp.exp(jnp.sum(lambda_q1[l] * lambda_k1[l])) - jnp.exp(jnp.sum(lambda_q2[l] * lambda_k2[l]))
               + lam_init).astype(F32)
        qkv = _qkv_proj(x2d, norm_mix_g[l], w_in[l].astype(BF16))
        qkv3 = qkv.reshape(b, s, qkv.shape[1])
        o_diff = _diff_attn(qkv3, lam, slopes, subln_g[l], lam_init).reshape(n, DIFF_WIDTH)
        o_na = _na_attn(qkv3, _na_bias_table(rpb[l], s // GRID_W)).reshape(n, NA_WIDTH)
        kv = _mem_kv(mem, norm_mem_kv_g[l], w_mkv[l].astype(BF16))
        wr_t = w_router[l].T
        wr_hi_t = wr_t.astype(BF16)
        wr_lo_t = (wr_t - wr_hi_t.astype(F32)).astype(BF16)
        x2, hf_tiles, ti, tg, tr, counts = _post_attn(
            x2d, o_diff, o_na, w_out[l].astype(BF16), norm_mem_q_g[l], w_mq[l].astype(BF16), kv,
            w_mo[l].astype(BF16), norm_ffn_g[l], wr_hi_t, wr_lo_t, b_router[l], s)
        row_slot, blk_e, nact, next_e = _routing_tables(counts[:, 0], ti[:TOP_K].T, tr[:TOP_K].T, n)
        y_slots = _expert_ffn(blk_e, nact, next_e, row_slot, hf_tiles, w1[l], b1[l], w2[l], b2[l], n)
        x2d = _combine(x2, tg.T, norm_final_g, y_slots, final_norm=(l == depth - 1))
    return x2d.reshape(b, s, d)
```

```python
import functools
import math

import jax
import jax.numpy as jnp
import numpy as np
from jax import lax
from jax.experimental import pallas as pl
from jax.experimental.pallas import tpu as pltpu

HEAD_DIM = 64
DIFF_HEADS = 4
DIFF_VDIM = 2 * HEAD_DIM
DIFF_QK_WIDTH = DIFF_HEADS * 2 * HEAD_DIM
DIFF_WIDTH = DIFF_HEADS * DIFF_VDIM
NA_HEADS = 8
NA_WIDTH = NA_HEADS * HEAD_DIM
GRID_W = 64
NA_ROWS = 8
NA_COLS = 16
MEM_HEADS = 4
MEM_HEAD_DIM = 64
MEM_WIDTH = MEM_HEADS * MEM_HEAD_DIM
N_EXPERTS = 32
TOP_K = 4
SWIGLU_LIMIT = 7.0
SWIGLU_ALPHA = 1.702
RMS_EPS = 1e-5

LANES = 128
SUBLANES = 8
NEG_BIG = -1e30

F32 = jnp.float32
BF16 = jnp.bfloat16

QKV_TM = 512
DIFF_TQ = 512
DIFF_TK = 512
DIFF_LANE_PARTS = 4
NA_R = 16
POST_TM = 512
POST_PARTS = 2
FFN_BLK = 256
COMB_TM = 256
WAITS_PER_TRIP = 16
INVERT_CHUNK = 8192
INVERT_UNROLL = 16
Y_SLOTS = 3
VMEM_LIMIT = 56 * 1024 * 1024


def _params(sem):
    return pltpu.CompilerParams(dimension_semantics=sem, vmem_limit_bytes=VMEM_LIMIT)


def _rms(x, g):
    return x * lax.rsqrt(jnp.mean(x * x, axis=-1, keepdims=True) + RMS_EPS) * g


def _dot(a, b):
    return jnp.dot(a, b, preferred_element_type=F32)


def _dot_nt(a, b):
    return lax.dot_general(a, b, (((1,), (1,)), ((), ())), preferred_element_type=F32)


def _rows_to_tiles(ref, val, rows):
    for j in range(SUBLANES):
        ref[pl.ds(j, rows, stride=SUBLANES), :] = val[:, j * LANES:(j + 1) * LANES]


def _tiles_to_rows(ref, rows, lead=None):
    if lead is None:
        parts = [ref[pl.ds(j, rows, stride=SUBLANES), :] for j in range(SUBLANES)]
    else:
        parts = [ref[lead, pl.ds(j, rows, stride=SUBLANES), :] for j in range(SUBLANES)]
    return jnp.concatenate(parts, axis=1)


def _qkv_kernel(x_ref, g_ref, w_ref, o_ref):
    h = _rms(x_ref[...], g_ref[...]).astype(BF16)
    cols = o_ref.shape[1]
    step = 512
    for c in range(cols // step):
        o_ref[:, c * step:(c + 1) * step] = _dot(h, w_ref[:, c * step:(c + 1) * step]).astype(o_ref.dtype)


def _qkv_proj(x2d, g, w_bf16):
    n, d = x2d.shape
    cols = w_bf16.shape[1]
    return pl.pallas_call(
        _qkv_kernel,
        grid=(n // QKV_TM,),
        in_specs=[pl.BlockSpec((QKV_TM, d), lambda i: (i, 0)),
                  pl.BlockSpec((1, d), lambda i: (0, 0)),
                  pl.BlockSpec((d, cols), lambda i: (0, 0))],
        out_specs=pl.BlockSpec((QKV_TM, cols), lambda i: (i, 0)),
        out_shape=jax.ShapeDtypeStruct((n, cols), BF16),
        compiler_params=_params(("parallel",)),
        name="qkv_proj",
    )(x2d, g.reshape(1, d), w_bf16)


ALIBI_AUG_LEFT, ALIBI_AUG_RIGHT, ALIBI_AUG_NONE = 0, 1, 2
ONES_ROWS = 16
EXP_UNDERFLOW = 105.0
NORM_BOUND_SLACK = 1.02


def _bf16_exact_split(n):
    lo_bits = max(0, int(n.max()).bit_length() - 8)
    lo = n % (1 << lo_bits)
    return n - lo, lo


def _alibi_aug_tables(tq, tk):
    ii_hi, ii_lo = _bf16_exact_split(np.arange(tq))
    jj_hi, jj_lo = _bf16_exact_split(np.arange(tk))
    qa = np.zeros((DIFF_HEADS, tq, LANES), np.float32)
    ka = np.zeros((DIFF_HEADS, 3, tk, LANES), np.float32)
    for h in range(DIFF_HEADS):
        sl = 2.0 ** (-8.0 * (h + 1) / DIFF_HEADS)
        qa[h, :, 0], qa[h, :, 1], qa[h, :, 2], qa[h, :, 3] = -sl * ii_hi, -sl * ii_lo, 1.0, 1.0
        qa[h, :, 4], qa[h, :, 5], qa[h, :, 6], qa[h, :, 7] = sl * ii_hi, sl * ii_lo, 1.0, 1.0
        ka[h, ALIBI_AUG_LEFT, :, 0], ka[h, ALIBI_AUG_LEFT, :, 1] = 1.0, 1.0
        ka[h, ALIBI_AUG_LEFT, :, 2], ka[h, ALIBI_AUG_LEFT, :, 3] = sl * jj_hi, sl * jj_lo
        ka[h, ALIBI_AUG_RIGHT, :, 4], ka[h, ALIBI_AUG_RIGHT, :, 5] = 1.0, 1.0
        ka[h, ALIBI_AUG_RIGHT, :, 6], ka[h, ALIBI_AUG_RIGHT, :, 7] = -sl * jj_hi, -sl * jj_lo
    return jnp.asarray(qa, BF16), jnp.asarray(ka, BF16)


def _diff_kernel(lam_ref, slope_ref, q_ref, k_ref, vt_ref, qaug_ref, kaug_ref, g_ref, o_ref,
                 qa_sc, s0_sc, s1_sc, mb_sc, m_sc, acc_sc, ksq_sc, *, tq, tk, seq, out_scale):
    h = pl.program_id(1)
    i = pl.program_id(2)
    slope = slope_ref[h]
    lam = lam_ref[0]
    nk = seq // tk
    i0 = i * tq
    jd = i0 // tk

    sel_r = lax.broadcasted_iota(jnp.int32, (LANES, LANES), 0) // HEAD_DIM
    sel_c = lax.broadcasted_iota(jnp.int32, (LANES, LANES), 1)
    sel = (sel_r == sel_c).astype(BF16)

    def max_sq_norm(x):
        xf = x.astype(F32)
        return jnp.max(_dot((xf * xf).astype(BF16), sel), axis=0, keepdims=True)

    @pl.when(i == 0)
    def _():
        ksq_sc[...] = max_sq_norm(k_ref[0])

    q = q_ref[0] * jnp.asarray(1.0 / math.sqrt(HEAD_DIM), BF16)
    lane = lax.broadcasted_iota(jnp.int32, (tq, LANES), 1)
    zero = jnp.zeros_like(q)
    qa_sc[:tq, :LANES] = jnp.where(lane < HEAD_DIM, q, zero)
    qa_sc[tq:, :LANES] = jnp.where(lane >= HEAD_DIM, q, zero)
    qa_sc[:tq, LANES:] = qaug_ref[0]
    qa_sc[tq:, LANES:] = qaug_ref[0]
    m_sc[...] = jnp.full(m_sc.shape, NEG_BIG, F32)
    acc_sc[...] = jnp.zeros(acc_sc.shape, F32)

    part = 2 * tq // DIFF_LANE_PARTS
    maps = tuple(slice(c * part, (c + 1) * part) for c in range(DIFF_LANE_PARTS))

    def scores(j, side, mp):
        start = pl.multiple_of(j * tk, tk)
        kba = jnp.concatenate([k_ref[0, pl.ds(start, tk), :], kaug_ref[0, side]], axis=1)
        return _dot_nt(kba, qa_sc[mp, :])

    s_bufs = (s0_sc, s1_sc)

    def stash(slot, mp, st):
        s_bufs[slot][:, mp] = st
        mb_sc[slot, :, mp] = jnp.max(st, axis=0, keepdims=True)

    def consume(slot, mp, j, c):
        m_prev = m_sc[:, mp]
        m_new = jnp.maximum(m_prev, mb_sc[slot, :, mp] + c)
        alpha = jnp.exp(m_prev - m_new)
        p_t = jnp.exp(s_bufs[slot][:, mp] - (m_new - c)).astype(BF16)
        start = pl.multiple_of(j * tk, tk)
        vta = jnp.concatenate([vt_ref[0, :, pl.ds(start, tk)], jnp.ones((ONES_ROWS, tk), BF16)], axis=0)
        acc_sc[:, mp] = alpha * acc_sc[:, mp] + _dot(vta, p_t)
        m_sc[:, mp] = m_new

    def block_const(j):
        return -slope * jnp.abs(i0 - j * tk).astype(F32)

    rel_t = (lax.broadcasted_iota(jnp.int32, (tk, tq), 0) - lax.broadcasted_iota(jnp.int32, (tk, tq), 1)).astype(F32)
    bias_t = -slope * jnp.abs(rel_t + (jd * tk - i0).astype(F32))
    for mp in maps:
        stash(0, mp, scores(jd, ALIBI_AUG_NONE, mp) + bias_t[:, mp.start % tq:mp.start % tq + part])

    qk_bound = jnp.sqrt(jnp.max(max_sq_norm(q) * ksq_sc[...])) * NORM_BOUND_SLACK
    reach = (qk_bound - jnp.min(mb_sc[0]) + EXP_UNDERFLOW) / slope
    reach = jnp.minimum(reach, 2.0 * seq)
    j_lo = jnp.clip(jnp.floor((i0 - tk + 1 - reach) / tk).astype(jnp.int32) + 1, 0, jd)
    j_hi = jnp.clip(jnp.ceil((reach + i0 + tq - 1) / tk).astype(jnp.int32) - 1, jd, nk - 1)
    n_blocks = j_hi - j_lo + 1

    def block_of(t):
        return j_lo + jnp.where(t - 1 < jd - j_lo, t - 1, t)

    def produce(t, slot, mp):
        j = block_of(t)
        stash(slot, mp, scores(j, jnp.where(j < jd, ALIBI_AUG_LEFT, ALIBI_AUG_RIGHT), mp))

    def consume_step(t, slot, mp):
        j = jnp.where(t == 0, jd, block_of(t))
        consume(slot, mp, j, jnp.where(t == 0, 0.0, block_const(j)))

    def stage(t_produce, t_consume, slot_produce, slot_consume):
        for mp in maps:
            if t_produce is not None:
                produce(t_produce, slot_produce, mp)
            consume_step(t_consume, slot_consume, mp)

    def body(u, carry):
        t = 2 * u
        stage(t + 1, t, 1, 0)
        stage(t + 2, t + 1, 0, 1)
        return carry

    def body4(u, carry):
        body(2 * u, carry)
        body(2 * u + 1, carry)
        return carry

    quads = (n_blocks - 1) // 4
    lax.fori_loop(0, quads, body4, 0)
    lax.fori_loop(2 * quads, (n_blocks - 1) // 2, body, 0)

    @pl.when(n_blocks % 2 == 0)
    def _():
        stage(n_blocks - 1, n_blocks - 2, 1, 0)
        stage(None, n_blocks - 1, None, 1)

    @pl.when(n_blocks % 2 == 1)
    def _():
        stage(None, n_blocks - 1, None, 0)

    acc = acc_sc[...]
    o = acc[:DIFF_VDIM] * (1.0 / acc[DIFF_VDIM:DIFF_VDIM + 1])
    d = o[:, :tq] - lam * o[:, tq:]
    y = d * lax.rsqrt(jnp.mean(d * d, axis=0, keepdims=True) + RMS_EPS) * g_ref[...] * out_scale
    o_ref[0] = y.T.astype(o_ref.dtype)


def _diff_attn(qkv3, lam, slopes, subln_g, lam_init):
    b, s, _ = qkv3.shape
    tq, tk = DIFF_TQ, DIFF_TK
    assert tk % tq == 0 and s % (2 * tk) == 0
    kcol = DIFF_QK_WIDTH // LANES
    v_t = qkv3[:, :, 2 * DIFF_QK_WIDTH:2 * DIFF_QK_WIDTH + DIFF_WIDTH].transpose(0, 2, 1)
    qaug, kaug = _alibi_aug_tables(tq, tk)
    kern = functools.partial(_diff_kernel, tq=tq, tk=tk, seq=s, out_scale=1.0 - lam_init)
    return pl.pallas_call(
        kern,
        grid=(b, DIFF_HEADS, s // tq),
        in_specs=[pl.BlockSpec(memory_space=pltpu.SMEM),
                  pl.BlockSpec(memory_space=pltpu.SMEM),
                  pl.BlockSpec((1, tq, LANES), lambda bi, h, i: (bi, i, h)),
                  pl.BlockSpec((1, s, LANES), lambda bi, h, i: (bi, 0, kcol + h)),
                  pl.BlockSpec((1, DIFF_VDIM, s), lambda bi, h, i: (bi, h, 0)),
                  pl.BlockSpec((1, tq, LANES), lambda bi, h, i: (h, 0, 0)),
                  pl.BlockSpec((1, 3, tk, LANES), lambda bi, h, i: (h, 0, 0, 0)),
                  pl.BlockSpec((DIFF_VDIM, 1), lambda bi, h, i: (0, 0))],
        out_specs=pl.BlockSpec((1, tq, LANES), lambda bi, h, i: (bi, i, h)),
        out_shape=jax.ShapeDtypeStruct((b, s, DIFF_WIDTH), BF16),
        scratch_shapes=[pltpu.VMEM((2 * tq, 2 * LANES), BF16),
                        pltpu.VMEM((tk, 2 * tq), F32),
                        pltpu.VMEM((tk, 2 * tq), F32),
                        pltpu.VMEM((2, 1, 2 * tq), F32),
                        pltpu.VMEM((1, 2 * tq), F32),
                        pltpu.VMEM((DIFF_VDIM + ONES_ROWS, 2 * tq), F32),
                        pltpu.VMEM((1, LANES), F32)],
        compiler_params=_params(("parallel", "parallel", "arbitrary")),
        name="diff_attn",
    )(lam.reshape(1), slopes, qkv3, qkv3, v_t, qaug, kaug, subln_g.reshape(DIFF_VDIM, 1))


def _na_bias_table(rpb, rows):
    kr = min(NA_ROWS, rows)
    q = np.arange(GRID_W)[:, None]
    j = np.arange(GRID_W)[None, :]
    cs = np.clip(q - NA_COLS // 2, 0, GRID_W - NA_COLS)
    valid = (j >= cs) & (j < cs + NA_COLS)
    dc = j - q + (NA_COLS - 1)
    sel = ((dc[:, :, None] == np.arange(2 * NA_COLS - 1)[None, None, :]) & valid[:, :, None]).astype(np.float32)
    col = jnp.einsum('hrd,qjd->hqrj', rpb.astype(F32), jnp.asarray(sel), precision=lax.Precision.HIGHEST)
    col = jnp.where(valid[None, :, None, :], col, NEG_BIG)
    cases = [col[:, :, NA_ROWS - 1 - c:NA_ROWS - 1 - c + kr] for c in range(kr)]
    return jnp.stack(cases, axis=0).reshape(kr, NA_HEADS // 2, 2 * GRID_W, kr * GRID_W)


def _na_kernel(q_ref, k_ref, v_ref, bias_ref, o_ref, *, rows_per_step, rows, kr):
    i = pl.program_id(2)
    lane = lax.broadcasted_iota(jnp.int32, (GRID_W, LANES), 1)
    win = kr * GRID_W

    starts, scores, probs = [], [], []
    for rr in range(rows_per_step):
        r = i * rows_per_step + rr
        rs = jnp.clip(r - kr // 2, 0, rows - kr)
        q = q_ref[0, rr * GRID_W:(rr + 1) * GRID_W, :]
        q = q * jnp.asarray(1.0 / math.sqrt(HEAD_DIM), BF16)
        zero = jnp.zeros_like(q)
        qs = jnp.concatenate([jnp.where(lane < HEAD_DIM, q, zero), jnp.where(lane >= HEAD_DIM, q, zero)], axis=0)
        start = pl.multiple_of(rs * GRID_W, GRID_W)
        starts.append(start)
        scores.append(_dot_nt(qs, k_ref[0, pl.ds(start, win), :]) + bias_ref[r - rs, 0])
    for s in scores:
        p = jnp.exp(s - jnp.max(s, axis=1, keepdims=True))
        probs.append((p.astype(BF16), jnp.sum(p, axis=1, keepdims=True)))
    for rr, (p, l) in enumerate(probs):
        o = _dot(p, v_ref[0, pl.ds(starts[rr], win), :]) / l
        out = jnp.where(lane < HEAD_DIM, o[:GRID_W], o[GRID_W:])
        o_ref[0, rr * GRID_W:(rr + 1) * GRID_W, :] = out.astype(o_ref.dtype)


def _na_attn(qkv3, bias_tbl):
    b, s, _ = qkv3.shape
    rows = s // GRID_W
    kr = min(NA_ROWS, rows)
    base = (2 * DIFF_QK_WIDTH + DIFF_WIDTH) // LANES
    pairs = NA_HEADS // 2
    rstep = NA_R
    kern = functools.partial(_na_kernel, rows_per_step=rstep, rows=rows, kr=kr)
    return pl.pallas_call(
        kern,
        grid=(b, pairs, rows // rstep),
        in_specs=[pl.BlockSpec((1, rstep * GRID_W, LANES), lambda bi, p, i: (bi, i, base + p)),
                  pl.BlockSpec((1, s, LANES), lambda bi, p, i: (bi, 0, base + pairs + p)),
                  pl.BlockSpec((1, s, LANES), lambda bi, p, i: (bi, 0, base + 2 * pairs + p)),
                  pl.BlockSpec((kr, 1, 2 * GRID_W, kr * GRID_W), lambda bi, p, i: (0, p, 0, 0))],
        out_specs=pl.BlockSpec((1, rstep * GRID_W, LANES), lambda bi, p, i: (bi, i, p)),
        out_shape=jax.ShapeDtypeStruct((b, s, NA_WIDTH), BF16),
        compiler_params=_params(("parallel", "parallel", "parallel")),
        name="na_attn",
    )(qkv3, qkv3, qkv3, bias_tbl)


def _memkv_kernel(mem_ref, g_ref, w_ref, o_ref):
    h = _rms(mem_ref[0], g_ref[...]).astype(BF16)
    o_ref[0] = _dot(h, w_ref[...]).astype(o_ref.dtype)


def _mem_kv(mem, g, w_bf16):
    b, m, d = mem.shape
    cols = w_bf16.shape[1]
    return pl.pallas_call(
        _memkv_kernel,
        grid=(b,),
        in_specs=[pl.BlockSpec((1, m, d), lambda i: (i, 0, 0)),
                  pl.BlockSpec((1, d), lambda i: (0, 0)),
                  pl.BlockSpec((d, cols), lambda i: (0, 0))],
        out_specs=pl.BlockSpec((1, m, cols), lambda i: (i, 0, 0)),
        out_shape=jax.ShapeDtypeStruct((b, m, cols), BF16),
        compiler_params=_params(("parallel",)),
        name="mem_kv",
    )(mem, g.reshape(1, d), w_bf16)


def _sublane_pack(rows, tm):
    sub = lax.broadcasted_iota(jnp.int32, (SUBLANES, tm), 0)
    out = jnp.zeros((SUBLANES, tm), rows[0].dtype)
    for k, r in enumerate(rows):
        out = jnp.where(sub == k, r, out)
    return out


def _post_kernel(x_ref, od_ref, on_ref, wout_ref, gq_ref, wmq_ref, kv_ref, wmo_ref, gf_ref,
                 wrh_ref, wrl_ref, br_ref,
                 x2_ref, hf_ref, ti_ref, tg_ref, tr_ref, cnt_ref, carry_sc, *, tm, parts):
    step = pl.program_id(0)
    rows = tm // parts

    @pl.when(step == 0)
    def _():
        carry_sc[...] = jnp.zeros(carry_sc.shape, F32)

    head_of_lane = lax.broadcasted_iota(jnp.int32, (rows, MEM_WIDTH), 1) // MEM_HEAD_DIM
    kv = kv_ref[0]

    x1s, qss = [], []
    for g in range(parts):
        rs = slice(g * rows, (g + 1) * rows)
        x1 = (x_ref[rs, :] + _dot(od_ref[rs, :], wout_ref[:DIFF_WIDTH, :])
              + _dot(on_ref[rs, :], wout_ref[DIFF_WIDTH:, :]))
        hm = _rms(x1, gq_ref[...]).astype(BF16)
        q = (_dot(hm, wmq_ref[...]) * (1.0 / math.sqrt(MEM_HEAD_DIM))).astype(BF16)
        zero = jnp.zeros_like(q)
        qss.append(jnp.concatenate([jnp.where(head_of_lane == h, q, zero) for h in range(MEM_HEADS)], axis=0))
        x1s.append(x1)

    probs = []
    for g in range(parts):
        s = _dot_nt(qss[g], kv[:, :MEM_WIDTH])
        p = jnp.exp(s - jnp.max(s, axis=1, keepdims=True))
        probs.append((p.astype(BF16), jnp.sum(p, axis=1, keepdims=True)))

    hfs = []
    for g in range(parts):
        rs = slice(g * rows, (g + 1) * rows)
        p, l = probs[g]
        o4 = _dot(p, kv[:, MEM_WIDTH:]) / l
        o = jnp.zeros((rows, MEM_WIDTH), F32)
        for h in range(MEM_HEADS):
            o = jnp.where(head_of_lane == h, o4[h * rows:(h + 1) * rows], o)
        x2 = x1s[g] + _dot(o.astype(BF16), wmo_ref[...])
        x2_ref[rs, :] = x2
        hf = _rms(x2, gf_ref[...])
        _rows_to_tiles(hf_ref.at[pl.ds(g * rows * SUBLANES, rows * SUBLANES), :], hf, rows)
        hfs.append(hf)

    e_iota = lax.broadcasted_iota(jnp.int32, (N_EXPERTS, rows), 0)
    r_i = lax.broadcasted_iota(jnp.int32, (rows, rows), 0)
    c_i = lax.broadcasted_iota(jnp.int32, (rows, rows), 1)
    earlier = (r_i < c_i).astype(BF16)
    carry = carry_sc[...]
    idx_rows, gate_rows, rank_rows = [], [], []
    for g in range(parts):
        hi = hfs[g].astype(BF16)
        lo = (hfs[g] - hi.astype(F32)).astype(BF16)
        logits = _dot_nt(wrh_ref[...], hi) + _dot_nt(wrl_ref[...], hi) + _dot_nt(wrh_ref[...], lo) + br_ref[...]
        vals, idxs, hots = [], [], []
        cur = logits
        for _ in range(TOP_K):
            mx = jnp.max(cur, axis=0, keepdims=True)
            idx = jnp.min(jnp.where(cur == mx, e_iota, N_EXPERTS), axis=0, keepdims=True)
            hot = e_iota == idx
            vals.append(mx)
            idxs.append(idx)
            hots.append(hot)
            cur = jnp.where(hot, -jnp.inf, cur)
        exps = [jnp.exp(v - vals[0]) for v in vals]
        den = exps[0] + exps[1] + exps[2] + exps[3]
        chosen = jnp.zeros((N_EXPERTS, rows), F32)
        for hot in hots:
            chosen = chosen + hot.astype(F32)
        before = _dot(chosen.astype(BF16), earlier) + carry
        carry = carry + jnp.sum(chosen, axis=1, keepdims=True)
        idx_rows.append(idxs)
        gate_rows.append([e / den for e in exps])
        rank_rows.append([jnp.sum(jnp.where(hot, before, 0.0), axis=0, keepdims=True).astype(jnp.int32)
                          for hot in hots])
    carry_sc[...] = carry
    cnt_ref[...] = carry

    def tile_rows(per_part):
        return [jnp.concatenate([per_part[g][k] for g in range(parts)], axis=1) for k in range(TOP_K)]

    ti_ref[...] = _sublane_pack(tile_rows(idx_rows), tm)
    tg_ref[...] = _sublane_pack(tile_rows(gate_rows), tm)
    tr_ref[...] = _sublane_pack(tile_rows(rank_rows), tm)


def _post_attn(x2d, o_diff, o_na, w_out, gq, w_mq, kv, w_mo, gf, wr_hi_t, wr_lo_t, b_router, seq):
    n, d = x2d.shape
    tm = POST_TM
    m_tok = kv.shape[1]
    const = lambda i: (0, 0)
    row = lambda i: (i, 0)
    col = lambda i: (0, i)
    kern = functools.partial(_post_kernel, tm=tm, parts=POST_PARTS)
    return pl.pallas_call(
        kern,
        grid=(n // tm,),
        in_specs=[pl.BlockSpec((tm, d), row),
                  pl.BlockSpec((tm, DIFF_WIDTH), row),
                  pl.BlockSpec((tm, NA_WIDTH), row),
                  pl.BlockSpec(w_out.shape, const),
                  pl.BlockSpec((1, d), const),
                  pl.BlockSpec(w_mq.shape, const),
                  pl.BlockSpec((1, m_tok, 2 * MEM_WIDTH), lambda i: ((i * tm) // seq, 0, 0)),
                  pl.BlockSpec(w_mo.shape, const),
                  pl.BlockSpec((1, d), const),
                  pl.BlockSpec(wr_hi_t.shape, const),
                  pl.BlockSpec(wr_lo_t.shape, const),
                  pl.BlockSpec((N_EXPERTS, 1), const)],
        out_specs=[pl.BlockSpec((tm, d), row),
                   pl.BlockSpec((tm * SUBLANES, LANES), row),
                   pl.BlockSpec((SUBLANES, tm), col),
                   pl.BlockSpec((SUBLANES, tm), col),
                   pl.BlockSpec((SUBLANES, tm), col),
                   pl.BlockSpec((N_EXPERTS, 1), const)],
        out_shape=[jax.ShapeDtypeStruct((n, d), F32),
                   jax.ShapeDtypeStruct((n * SUBLANES, LANES), F32),
                   jax.ShapeDtypeStruct((SUBLANES, n), jnp.int32),
                   jax.ShapeDtypeStruct((SUBLANES, n), F32),
                   jax.ShapeDtypeStruct((SUBLANES, n), jnp.int32),
                   jax.ShapeDtypeStruct((N_EXPERTS, 1), F32)],
        scratch_shapes=[pltpu.VMEM((N_EXPERTS, 1), F32)],
        compiler_params=_params(("arbitrary",)),
        name="post_attn",
    )(x2d, o_diff, o_na, w_out, gq.reshape(1, d), w_mq, kv, w_mo, gf.reshape(1, d), wr_hi_t, wr_lo_t,
      b_router.reshape(N_EXPERTS, 1))


def _row_copy(src, src_row, dst, dst_row, sem):
    return pltpu.make_async_copy(
        src.at[pl.ds(pl.multiple_of(src_row * SUBLANES, SUBLANES), SUBLANES), :],
        dst.at[pl.ds(pl.multiple_of(dst_row * SUBLANES, SUBLANES), SUBLANES), :],
        sem)


def _ffn_kernel(blk_e_ref, nact_ref, next_e_ref, slot_cur_ref, slot_next_ref, slot_prev_ref, hf_ref,
                w1_ref, b1_ref, w2_ref, b2_ref, yc_ref,
                xbuf, ybuf, w1f_sc, w2f_sc, w1b_sc, w2b_sc, wslot_sc, gsem, ssem, wsem, *, blk, d_ff, n_tok):
    i = pl.program_id(0)
    nact = nact_ref[0]
    slot = i % 2
    y_slot = i % Y_SLOTS
    y_prev = (i + Y_SLOTS - 1) % Y_SLOTS

    trash_row = n_tok * TOP_K

    def gather(slot_ref, t, to_slot):
        tok = slot_ref[t] & (n_tok - 1)
        return _row_copy(hf_ref, tok, xbuf.at[to_slot], t, gsem.at[to_slot])

    def scatter(t):
        dst = jnp.where(i == 0, trash_row + t, slot_prev_ref[t])
        return _row_copy(ybuf.at[y_prev], t, yc_ref, dst, ssem)

    def wait_all(copy_of_row):
        def drain(g, carry):
            for u in range(WAITS_PER_TRIP):
                copy_of_row(g * WAITS_PER_TRIP + u).wait()
            return carry
        lax.fori_loop(0, blk // WAITS_PER_TRIP, drain, 0)

    @pl.when(i == 0)
    def _():
        ybuf[...] = jnp.zeros(ybuf.shape, F32)
        for t in range(blk):
            gather(slot_cur_ref, t, 0).start(priority=t % 2)

    def weight_copies(e, s):
        return (pltpu.make_async_copy(w1_ref.at[e], w1f_sc.at[s], wsem.at[s]),
                pltpu.make_async_copy(w2_ref.at[e], w2f_sc.at[s], wsem.at[s]))

    @pl.when(i == 0)
    def _():
        wslot_sc[0] = 0
        for cp in weight_copies(blk_e_ref[0], 0):
            cp.start(priority=1)

    @pl.when((i < nact) & ((i == 0) | (blk_e_ref[i] != blk_e_ref[jnp.maximum(i - 1, 0)])))
    def _():
        e = blk_e_ref[i]
        s = wslot_sc[0]
        for cp in weight_copies(e, s):
            cp.wait()
        nxt = next_e_ref[e]

        @pl.when(nxt >= 0)
        def _():
            for cp in weight_copies(nxt, 1 - s):
                cp.start(priority=1)

        rows = 256
        for c in range(w1b_sc.shape[0] // rows):
            w1b_sc[c * rows:(c + 1) * rows, :] = w1f_sc[s, c * rows:(c + 1) * rows, :].astype(BF16)
        for c in range(w2b_sc.shape[0] // rows):
            w2b_sc[c * rows:(c + 1) * rows, :] = w2f_sc[s, c * rows:(c + 1) * rows, :].astype(BF16)
        wslot_sc[0] = 1 - s

    @pl.when(i <= nact)
    def _():
        wait_all(lambda t: gather(slot_cur_ref, t, slot))

    @pl.when((i >= 1) & (i <= nact))
    def _():
        wait_all(scatter)

    @pl.when(i < nact)
    def _():
        for t in range(blk):
            gather(slot_next_ref, t, 1 - slot).start(priority=0)
            scatter(t).start(priority=1)
        x = _tiles_to_rows(xbuf, blk, lead=slot).astype(BF16)
        gu = _dot(x, w1b_sc[...]) + b1_ref[0]
        gate = jnp.minimum(gu[:, :d_ff], SWIGLU_LIMIT)
        lin = jnp.clip(gu[:, d_ff:], -SWIGLU_LIMIT, SWIGLU_LIMIT)
        act = gate * jax.nn.sigmoid(SWIGLU_ALPHA * gate) * (lin + 1.0)
        y = _dot(act.astype(BF16), w2b_sc[...]) + b2_ref[0]
        _rows_to_tiles(ybuf.at[y_slot], y, blk)

    @pl.when(i == nact)
    def _():
        def issue(t, carry):
            scatter(t).start()
            return carry
        lax.fori_loop(0, blk, issue, 0)
        wait_all(scatter)


def _expert_ffn(blk_e, nact, next_e, row_slot, hf_tiles, w1, b1, w2, b2, n_tok):
    blk = FFN_BLK
    nblk = row_slot.shape[0] // blk
    n_slots = n_tok * TOP_K
    assert n_tok & (n_tok - 1) == 0
    _, d, two_ff = w1.shape
    d_ff = two_ff // 2
    kern = functools.partial(_ffn_kernel, blk=blk, d_ff=d_ff, n_tok=n_tok)
    smem_blk = lambda f: pl.BlockSpec((blk,), f, memory_space=pltpu.SMEM)
    grid_spec = pltpu.PrefetchScalarGridSpec(
        num_scalar_prefetch=3,
        grid=(nblk,),
        in_specs=[smem_blk(lambda i, be, na, ne: (i,)),
                  smem_blk(lambda i, be, na, ne: (jnp.minimum(i + 1, nblk - 1),)),
                  smem_blk(lambda i, be, na, ne: (jnp.maximum(i - 1, 0),)),
                  pl.BlockSpec(memory_space=pl.ANY),
                  pl.BlockSpec(memory_space=pl.ANY),
                  pl.BlockSpec((1, 1, two_ff), lambda i, be, na, ne: (be[i], 0, 0)),
                  pl.BlockSpec(memory_space=pl.ANY),
                  pl.BlockSpec((1, 1, d), lambda i, be, na, ne: (be[i], 0, 0))],
        out_specs=pl.BlockSpec(memory_space=pl.ANY),
        scratch_shapes=[pltpu.VMEM((2, blk * SUBLANES, LANES), F32),
                        pltpu.VMEM((Y_SLOTS, blk * SUBLANES, LANES), F32),
                        pltpu.VMEM((2, d, two_ff), F32),
                        pltpu.VMEM((2, d_ff, d), F32),
                        pltpu.VMEM((d, two_ff), BF16),
                        pltpu.VMEM((d_ff, d), BF16),
                        pltpu.SMEM((1,), jnp.int32),
                        pltpu.SemaphoreType.DMA((2,)),
                        pltpu.SemaphoreType.DMA(()),
                        pltpu.SemaphoreType.DMA((2,))],
    )
    return pl.pallas_call(
        kern,
        grid_spec=grid_spec,
        out_shape=jax.ShapeDtypeStruct(((n_slots + blk) * SUBLANES, LANES), F32),
        compiler_params=_params(("arbitrary",)),
        name="expert_ffn",
    )(blk_e, nact, next_e, row_slot, row_slot, row_slot, hf_tiles, w1, b1.reshape(N_EXPERTS, 1, two_ff), w2,
      b2.reshape(N_EXPERTS, 1, d))


def _combine_kernel(x2_ref, tg_ref, gfin_ref, *rest, tm, final_norm):
    y_refs, o_ref = rest[:TOP_K], rest[TOP_K]
    gates = tg_ref[...]
    pieces = []
    for j in range(SUBLANES):
        acc = x2_ref[:, j * LANES:(j + 1) * LANES]
        for k in range(TOP_K):
            acc = acc + gates[:, k:k + 1] * y_refs[k][pl.ds(j, tm, stride=SUBLANES), :]
        pieces.append(acc)
    out = jnp.concatenate(pieces, axis=1)
    if final_norm:
        out = _rms(out, gfin_ref[...])
    o_ref[...] = out


def _combine(x2, gates, g_final, y_slots, final_norm):
    n, d = x2.shape
    tm = COMB_TM
    kern = functools.partial(_combine_kernel, tm=tm, final_norm=final_norm)
    tiles = n // tm

    def choice_spec(k):
        return pl.BlockSpec((tm * SUBLANES, LANES), lambda i: (k * tiles + i, 0))

    return pl.pallas_call(
        kern,
        grid=(tiles,),
        in_specs=[pl.BlockSpec((tm, d), lambda i: (i, 0)),
                  pl.BlockSpec((tm, SUBLANES), lambda i: (i, 0)),
                  pl.BlockSpec((1, d), lambda i: (0, 0))] + [choice_spec(k) for k in range(TOP_K)],
        out_specs=pl.BlockSpec((tm, d), lambda i: (i, 0)),
        out_shape=jax.ShapeDtypeStruct((n, d), F32),
        compiler_params=_params(("parallel",)),
        name="combine",
    )(x2, gates, g_final.reshape(1, d), *([y_slots] * TOP_K))


def _invert_kernel(pad_lo_ref, pad_hi_ref, dest_ref, slot_ref, *, chunk, n_tok, blk):
    n_assign = n_tok * TOP_K
    c = pl.program_id(0)

    @pl.when(c == 0)
    def _():
        def expert(e, carry):
            def pad_row(r, carry2):
                slot_ref[r] = n_assign + (r & (blk - 1))
                return carry2
            lax.fori_loop(pad_lo_ref[e], pad_hi_ref[e], pad_row, 0)
            return carry
        lax.fori_loop(0, N_EXPERTS, expert, 0)

    def body(g, carry):
        a0 = g * INVERT_UNROLL
        tok0 = lax.shift_right_logical(c * chunk + a0, TOP_K.bit_length() - 1)
        for u in range(INVERT_UNROLL):
            slot_ref[dest_ref[a0 + u]] = tok0 + u // TOP_K + (u % TOP_K) * n_tok
        return carry
    lax.fori_loop(0, chunk // INVERT_UNROLL, body, 0)


def _invert_routing(dest, pad_lo, pad_hi, p_rows):
    n_assign = dest.shape[0]
    chunk = INVERT_CHUNK
    blk = FFN_BLK
    assert n_assign % chunk == 0 and chunk % INVERT_UNROLL == 0 and blk & (blk - 1) == 0
    assert INVERT_UNROLL % TOP_K == 0 and TOP_K & (TOP_K - 1) == 0
    kern = functools.partial(_invert_kernel, chunk=chunk, n_tok=n_assign // TOP_K, blk=blk)
    whole_smem = pl.BlockSpec(memory_space=pltpu.SMEM)
    grid_spec = pltpu.PrefetchScalarGridSpec(
        num_scalar_prefetch=2,
        grid=(n_assign // chunk,),
        in_specs=[pl.BlockSpec((chunk,), lambda c, lo, hi: (c,), memory_space=pltpu.SMEM)],
        out_specs=whole_smem,
    )
    return pl.pallas_call(
        kern,
        grid_spec=grid_spec,
        out_shape=jax.ShapeDtypeStruct((p_rows,), jnp.int32),
        compiler_params=_params(("arbitrary",)),
        name="invert_routing",
    )(pad_lo, pad_hi, dest)


def _routing_tables(counts_f32, top_i, rank, n_tok):
    blk = FFN_BLK
    n_assign = n_tok * TOP_K
    counts = counts_f32.astype(jnp.int32)
    pcounts = ((counts + blk - 1) // blk) * blk
    pends = jnp.cumsum(pcounts)
    pstarts = pends - pcounts
    dest = (pstarts[top_i] + rank).reshape(-1)
    p_rows = n_assign + N_EXPERTS * blk
    nblk = p_rows // blk
    pad_lo = (pstarts + counts).astype(jnp.int32)
    pad_hi = pends.at[N_EXPERTS - 1].set(p_rows).astype(jnp.int32)
    row_slot = _invert_routing(dest.astype(jnp.int32), pad_lo, pad_hi, p_rows)
    blk_start = jnp.arange(nblk, dtype=jnp.int32) * blk
    blk_e = jnp.minimum(jnp.sum(blk_start[:, None] >= pends[None, :], axis=1), N_EXPERTS - 1).astype(jnp.int32)
    nact = (pends[-1] // blk).astype(jnp.int32).reshape(1)
    e_ids = jnp.arange(N_EXPERTS, dtype=jnp.int32)
    later_active = jnp.where((counts[None, :] > 0) & (e_ids[None, :] > e_ids[:, None]), e_ids[None, :], N_EXPERTS)
    next_e = jnp.min(later_active, axis=1)
    next_e = jnp.where(next_e == N_EXPERTS, -1, next_e).astype(jnp.int32)
    return row_slot, blk_e, nact, next_e


def _lambda_init(layer):
    return 0.8 - 0.6 * math.exp(-0.3 * layer)


def kernel(x, mem, norm_mix_g, w_in, lambda_q1, lambda_k1, lambda_q2, lambda_k2, subln_g, rpb, w_out,
           norm_mem_q_g, norm_mem_kv_g, w_mq, w_mkv, w_mo, norm_ffn_g, w_router, b_router, w1, b1, w2, b2,
           norm_final_g):
    b, s, d = x.shape
    n = b * s
    depth = w_in.shape[0]
    assert d == SUBLANES * LANES and s % GRID_W == 0
    slopes = jnp.asarray([2.0 ** (-8.0 * (i + 1) / DIFF_HEADS) for i in range(DIFF_HEADS)], F32)
    x2d = x.reshape(n, d)
    for l in range(depth):
        lam_init = _lambda_init(l)
        lam = (jnp.exp(jnp.sum(lambda_q1[l] * lambda_k1[l])) - jnp.exp(jnp.sum(lambda_q2[l] * lambda_k2[l]))
               + lam_init).astype(F32)
        qkv = _qkv_proj(x2d, norm_mix_g[l], w_in[l].astype(BF16))
        qkv3 = qkv.reshape(b, s, qkv.shape[1])
        o_diff = _diff_attn(qkv3, lam, slopes, subln_g[l], lam_init).reshape(n, DIFF_WIDTH)
        o_na = _na_attn(qkv3, _na_bias_table(rpb[l], s // GRID_W)).reshape(n, NA_WIDTH)
        kv = _mem_kv(mem, norm_mem_kv_g[l], w_mkv[l].astype(BF16))
        wr_t = w_router[l].T
        wr_hi_t = wr_t.astype(BF16)
        wr_lo_t = (wr_t - wr_hi_t.astype(F32)).astype(BF16)
        x2, hf_tiles, ti, tg, tr, counts = _post_attn(
            x2d, o_diff, o_na, w_out[l].astype(BF16), norm_mem_q_g[l], w_mq[l].astype(BF16), kv,
            w_mo[l].astype(BF16), norm_ffn_g[l], wr_hi_t, wr_lo_t, b_router[l], s)
        row_slot, blk_e, nact, next_e = _routing_tables(counts[:, 0], ti[:TOP_K].T, tr[:TOP_K].T, n)
        y_slots = _expert_ffn(blk_e, nact, next_e, row_slot, hf_tiles, w1[l], b1[l], w2[l], b2[l], n)
        x2d = _combine(x2, tg.T, norm_final_g, y_slots, final_norm=(l == depth - 1))
    return x2d.reshape(b, s, d)
```
